```python
import math
import jax, jax.numpy as jnp
from jax import lax
import numpy as np

D_MODEL = 1024
BATCH = 8
SEQ = 2048
DEPTH = 2
DEC_BATCH = 128
DEC_SEQ = 8
PAST_LEN = 16384
PAGE_SIZE = 128

SSD_HEAD_DIM = 64
SSD_WIDTH = D_MODEL
SSD_HEADS = SSD_WIDTH // SSD_HEAD_DIM
SSD_GROUPS = 2
SSD_STATE = 128
CONV_WIDTH = 4
CONV_DIM = SSD_WIDTH + 2 * SSD_GROUPS * SSD_STATE
SSD_CHUNK = 128
RET_HEADS = 4
RET_WIDTH = D_MODEL
RET_V_DIM = RET_WIDTH // RET_HEADS
RET_QK_DIM = RET_V_DIM // 2
RET_CHUNK = 128
ROPE_BASE = 10000.0
MIX_WIDTH = SSD_WIDTH + RET_WIDTH
IN_SIZES = (SSD_WIDTH, CONV_DIM, SSD_HEADS, RET_HEADS * RET_QK_DIM, RET_HEADS * RET_QK_DIM, RET_WIDTH, RET_WIDTH)
IN_DIM = SSD_WIDTH + CONV_DIM + SSD_HEADS + 2 * RET_HEADS * RET_QK_DIM + 2 * RET_WIDTH
N_MEM = 256
XA_HEADS = 4
XA_HEAD_DIM = D_MODEL // XA_HEADS
N_EGROUPS = 4
EXPERTS_PER_GROUP = 4
N_EXPERTS = N_EGROUPS * EXPERTS_PER_GROUP
TOP_K = 2
EXPERT_FF = D_MODEL // 2
RMS_EPS = 1e-6

kernel_name = 'hymba_ssd_retention_hmoe_decode_step'


def _rms(x):
    xf = x.astype(jnp.float32)
    return (xf * lax.rsqrt(jnp.mean(xf * xf, axis=-1, keepdims=True) + RMS_EPS)).astype(x.dtype)


def rmsnorm(x, gain):
    return _rms(x) * gain


def _chunk_len(L, c):
    return c if L % c == 0 else L


def _causal_mask(n):
    return jnp.tril(jnp.ones((n, n), dtype=bool))


def retention_log_decay():
    return jnp.log(1.0 - jnp.exp2(-5.0 - jnp.arange(RET_HEADS, dtype=jnp.float32)))


def causal_conv(xbc, prev, w, b):
    L = xbc.shape[1]
    xp = jnp.concatenate([prev.astype(xbc.dtype), xbc], axis=1)
    out = b + xp[:, 0:L] * w[0]
    for j in range(1, CONV_WIDTH):
        out = out + xp[:, j:j + L] * w[j]
    return out, xp[:, L:]


def ssd_scan(x, dt, A, Bm, Cm, state0):
    b, L, H, P = x.shape
    G, N = Bm.shape[2], Bm.shape[3]
    HG = H // G
    cl = _chunk_len(L, SSD_CHUNK)
    nc = L // cl
    f32 = jnp.float32
    xdt = (x.astype(f32) * dt[..., None]).reshape(b, nc, cl, G, HG, P)
    a = (dt * A).reshape(b, nc, cl, G, HG)
    Bc = Bm.astype(f32).reshape(b, nc, cl, G, N)
    Cc = Cm.astype(f32).reshape(b, nc, cl, G, N)
    a_cum = jnp.cumsum(a, axis=2)
    a_t = jnp.moveaxis(a_cum, 2, -1)
    seg = a_t[..., :, None] - a_t[..., None, :]
    Lmat = jnp.exp(jnp.where(_causal_mask(cl), seg, -jnp.inf))
    CB = jnp.einsum('bclgn,bcsgn->bcgls', Cc, Bc)
    y_diag = jnp.einsum('bcgls,bcghls,bcsghp->bclghp', CB, Lmat, xdt)
    decay_to_end = jnp.exp(a_cum[:, :, -1:] - a_cum)
    chunk_states = jnp.einsum('bclgn,bclgh,bclghp->bcghpn', Bc, decay_to_end, xdt)
    chunk_decay = jnp.exp(a_cum[:, :, -1])
    s0 = state0.astype(f32).reshape(b, G, HG, P, N)

    def step(s, inp):
        cs, cd = inp
        return cd[..., None, None] * s + cs, s

    s_final, s_in = lax.scan(step, s0, (jnp.moveaxis(chunk_states, 1, 0), jnp.moveaxis(chunk_decay, 1, 0)))
    s_in = jnp.moveaxis(s_in, 0, 1)
    y_off = jnp.einsum('bclgn,bcghpn,bclgh->bclghp', Cc, s_in, jnp.exp(a_cum))
    y = (y_diag + y_off).reshape(b, L, H, P)
    return y, s_final.reshape(b, H, P, N)


def rotary(x, pos):
    half = x.shape[-1] // 2
    inv = ROPE_BASE ** (-jnp.arange(half, dtype=jnp.float32) / half)
    ang = pos[:, None] * inv[None, :]
    cos = jnp.cos(ang)[None, :, None, :]
    sin = jnp.sin(ang)[None, :, None, :]
    x1, x2 = x[..., :half], x[..., half:]
    return jnp.concatenate([x1 * cos - x2 * sin, x1 * sin + x2 * cos], axis=-1)


def retention_chunkwise(q, k, v, state0, log_gamma):
    b, L, H, dk = q.shape
    dv = v.shape[-1]
    cl = _chunk_len(L, RET_CHUNK)
    nc = L // cl
    q = q.reshape(b, nc, cl, H, dk)
    k = k.reshape(b, nc, cl, H, dk)
    v = v.reshape(b, nc, cl, H, dv)
    idx = jnp.arange(cl, dtype=jnp.float32)
    rel = idx[:, None] - idx[None, :]
    decay = jnp.exp(jnp.where(_causal_mask(cl), rel[None] * log_gamma[:, None, None], -jnp.inf))
    scores = jnp.einsum('bclhd,bcshd->bchls', q, k) * decay
    o_intra = jnp.einsum('bchls,bcshe->bclhe', scores, v)
    k_end = k * jnp.exp((cl - 1 - idx)[:, None] * log_gamma)[:, :, None]
    chunk_kv = jnp.einsum('bclhd,bclhe->bchde', k_end, v)
    chunk_decay = jnp.exp(cl * log_gamma)[:, None, None]

    def step(s, kv):
        return chunk_decay * s + kv, s

    s_final, s_in = lax.scan(step, state0.astype(jnp.float32), jnp.moveaxis(chunk_kv, 1, 0))
    s_in = jnp.moveaxis(s_in, 0, 1)
    q_start = q * jnp.exp((idx + 1.0)[:, None] * log_gamma)[:, :, None]
    o_cross = jnp.einsum('bclhd,bchde->bclhe', q_start, s_in)
    return (o_intra + o_cross).reshape(b, L, H, dv), s_final


def parallel_mixer(h, conv_prev, ssm_prev, ret_prev, pos0, w_in, conv_w, conv_b, dt_bias, a_log, d_skip,
                   ssd_gain, ret_gain, w_out):
    b, L, _ = h.shape
    proj = h @ w_in
    offs = np.cumsum(np.array(IN_SIZES))[:-1].tolist()
    z, xbc, dt_raw, q, k, v, g = jnp.split(proj, offs, axis=-1)
    xbc, conv_new = causal_conv(xbc, conv_prev, conv_w, conv_b)
    xbc = jax.nn.silu(xbc)
    xs, Bm, Cm = jnp.split(xbc, [SSD_WIDTH, SSD_WIDTH + SSD_GROUPS * SSD_STATE], axis=-1)
    xs = xs.reshape(b, L, SSD_HEADS, SSD_HEAD_DIM)
    Bm = Bm.reshape(b, L, SSD_GROUPS, SSD_STATE)
    Cm = Cm.reshape(b, L, SSD_GROUPS, SSD_STATE)
    dt = jax.nn.softplus(dt_raw.astype(jnp.float32) + dt_bias)
    A = -jnp.exp(a_log.astype(jnp.float32))
    y, ssm_new = ssd_scan(xs, dt, A, Bm, Cm, ssm_prev)
    y = (y + d_skip[:, None] * xs).reshape(b, L, SSD_WIDTH) * jax.nn.silu(z)
    y_ssd = _rms(y.reshape(b, L, SSD_GROUPS, SSD_WIDTH // SSD_GROUPS)).reshape(b, L, SSD_WIDTH) * ssd_gain
    pos = (pos0 + jnp.arange(L)).astype(jnp.float32)
    q = rotary(q.reshape(b, L, RET_HEADS, RET_QK_DIM), pos)
    k = rotary(k.reshape(b, L, RET_HEADS, RET_QK_DIM), pos) * (RET_QK_DIM ** -0.5)
    v = v.reshape(b, L, RET_HEADS, RET_V_DIM)
    o, ret_new = retention_chunkwise(q, k, v, ret_prev, retention_log_decay())
    y_ret = _rms(o).reshape(b, L, RET_WIDTH) * ret_gain * jax.nn.silu(g)
    out = jnp.concatenate([y_ssd, y_ret], axis=-1) @ w_out
    return out, conv_new, ssm_new, ret_new


def mem_proj(mem, w):
    b, m, _ = mem.shape
    return (mem @ w).reshape(b, m, XA_HEADS, XA_HEAD_DIM)


def cross_attend(h, mem_k, mem_v, w_q, w_o):
    b, L, _ = h.shape
    q = (h @ w_q).reshape(b, L, XA_HEADS, XA_HEAD_DIM)
    s = jnp.einsum('blhd,bmhd->bhlm', q, mem_k).astype(jnp.float32) * (XA_HEAD_DIM ** -0.5)
    p = jax.nn.softmax(s, axis=-1).astype(mem_v.dtype)
    o = jnp.einsum('bhlm,bmhd->blhd', p, mem_v).reshape(b, L, D_MODEL)
    return o @ w_o


def hier_moe(h, w_rg, b_rg, w_re, b_re, w_gate, w_up, w_down):
    shp = h.shape
    hf = h.reshape(-1, shp[-1])
    g_logits = (hf @ w_rg).astype(jnp.float32) + b_rg
    g_idx = jnp.argmax(g_logits, axis=-1)
    g_prob = jnp.max(jax.nn.softmax(g_logits, axis=-1), axis=-1)
    e_logits = ((hf @ w_re).astype(jnp.float32) + b_re).reshape(-1, N_EGROUPS, EXPERTS_PER_GROUP)
    e_logits = jnp.einsum('tg,tge->te', jax.nn.one_hot(g_idx, N_EGROUPS, dtype=jnp.float32), e_logits)
    e_prob = jax.nn.softmax(e_logits, axis=-1)
    top_p, top_i = lax.top_k(e_prob, TOP_K)
    weight = g_prob[:, None] * top_p / jnp.sum(top_p, axis=-1, keepdims=True)
    expert_id = g_idx[:, None] * EXPERTS_PER_GROUP + top_i
    gates = jnp.einsum('tk,tke->te', weight, jax.nn.one_hot(expert_id, N_EXPERTS, dtype=jnp.float32))
    y = jnp.zeros(hf.shape, jnp.float32)
    for e in range(N_EXPERTS):
        a = jax.nn.silu(hf @ w_gate[e]) * (hf @ w_up[e])
        y = y + gates[:, e:e + 1] * (a @ w_down[e])
    return y.reshape(shp).astype(h.dtype)


def trunk(x, pos0, mem_k, mem_v, conv_prev, ssm_prev, ret_prev, norm_mix, w_in, conv_w, conv_b, dt_bias, a_log,
          d_skip, ssd_gain, ret_gain, w_out, norm_mem, w_mq, w_mo, norm_ffn, w_rg, b_rg, w_re, b_re, w_gate, w_up,
          w_down, norm_final):
    convs, ssms, rets = [], [], []
    for i in range(DEPTH):
        m, c_new, s_new, r_new = parallel_mixer(rmsnorm(x, norm_mix[i]), conv_prev[i], ssm_prev[i], ret_prev[i],
                                                pos0, w_in[i], conv_w[i], conv_b[i], dt_bias[i], a_log[i],
                                                d_skip[i], ssd_gain[i], ret_gain[i], w_out[i])
        x = x + m
        x = x + cross_attend(rmsnorm(x, norm_mem[i]), mem_k[i], mem_v[i], w_mq[i], w_mo[i])
        x = x + hier_moe(rmsnorm(x, norm_ffn[i]), w_rg[i], b_rg[i], w_re[i], b_re[i], w_gate[i], w_up[i], w_down[i])
        convs.append(c_new)
        ssms.append(s_new)
        rets.append(r_new)
    return rmsnorm(x, norm_final), jnp.stack(ssms), jnp.stack(convs), jnp.stack(rets)


def setup_inputs(seed: int = 0) -> dict:
    key = jax.random.key(seed)
    k = jax.random.split(key, 32)
    f32 = jnp.float32

    def nrm(kk, shape, scale):
        return scale * jax.random.normal(kk, shape, f32)

    def gain(kk, shape):
        return 1.0 + 0.01 * jax.random.normal(kk, shape, f32)

    dt0 = jnp.exp(jax.random.uniform(k[12], (DEPTH, SSD_HEADS), f32, math.log(1e-3), math.log(1e-1)))
    return {
        'x_prompt': nrm(k[0], (BATCH, SEQ, D_MODEL), 1.0),
        'x_sample': nrm(k[1], (DEC_BATCH, DEC_SEQ, D_MODEL), 1.0),
        'mem_prompt': nrm(k[2], (BATCH, N_MEM, D_MODEL), 1.0),
        'state_ssm': nrm(k[3], (DEPTH, DEC_BATCH, SSD_HEADS, SSD_HEAD_DIM, SSD_STATE), 0.1),
        'state_conv': nrm(k[4], (DEPTH, DEC_BATCH, CONV_WIDTH - 1, CONV_DIM), 1.0),
        'state_ret': nrm(k[5], (DEPTH, DEC_BATCH, RET_HEADS, RET_QK_DIM, RET_V_DIM), 1.0),
        'cache_mem_k': nrm(k[6], (DEPTH, DEC_BATCH, N_MEM, XA_HEADS, XA_HEAD_DIM), 1.0),
        'cache_mem_v': nrm(k[7], (DEPTH, DEC_BATCH, N_MEM, XA_HEADS, XA_HEAD_DIM), 1.0),
        'norm_mix': gain(k[8], (DEPTH, D_MODEL)),
        'w_in': nrm(k[9], (DEPTH, D_MODEL, IN_DIM), D_MODEL ** -0.5),
        'conv_w': nrm(k[10], (DEPTH, CONV_WIDTH, CONV_DIM), CONV_WIDTH ** -0.5),
        'conv_b': nrm(k[11], (DEPTH, CONV_DIM), 0.01),
        'dt_bias': dt0 + jnp.log(-jnp.expm1(-dt0)),
        'a_log': jnp.log(jax.random.uniform(k[13], (DEPTH, SSD_HEADS), f32, 1.0, 16.0)),
        'd_skip': gain(k[14], (DEPTH, SSD_HEADS)),
        'ssd_gain': gain(k[15], (DEPTH, SSD_WIDTH)),
        'ret_gain': gain(k[16], (DEPTH, RET_WIDTH)),
        'w_out': nrm(k[17], (DEPTH, MIX_WIDTH, D_MODEL), MIX_WIDTH ** -0.5),
        'norm_mem': gain(k[18], (DEPTH, D_MODEL)),
        'w_mq': nrm(k[19], (DEPTH, D_MODEL, D_MODEL), D_MODEL ** -0.5),
        'w_mk': nrm(k[20], (DEPTH, D_MODEL, D_MODEL), D_MODEL ** -0.5),
        'w_mv': nrm(k[21], (DEPTH, D_MODEL, D_MODEL), D_MODEL ** -0.5),
        'w_mo': nrm(k[22], (DEPTH, D_MODEL, D_MODEL), D_MODEL ** -0.5),
        'norm_ffn': gain(k[23], (DEPTH, D_MODEL)),
        'w_rg': nrm(k[24], (DEPTH, D_MODEL, N_EGROUPS), D_MODEL ** -0.5),
        'b_rg': nrm(k[25], (DEPTH, N_EGROUPS), 0.01),
        'w_re': nrm(k[26], (DEPTH, D_MODEL, N_EXPERTS), D_MODEL ** -0.5),
        'b_re': nrm(k[27], (DEPTH, N_EXPERTS), 0.01),
        'w_gate': nrm(k[28], (DEPTH, N_EXPERTS, D_MODEL, EXPERT_FF), D_MODEL ** -0.5),
        'w_up': nrm(k[29], (DEPTH, N_EXPERTS, D_MODEL, EXPERT_FF), D_MODEL ** -0.5),
        'w_down': nrm(k[30], (DEPTH, N_EXPERTS, EXPERT_FF, D_MODEL), EXPERT_FF ** -0.5),
        'norm_final': gain(k[31], (D_MODEL,)),
    }


def reference(x_prompt, x_sample, mem_prompt, state_ssm, state_conv, state_ret, cache_mem_k, cache_mem_v,
              norm_mix, w_in, conv_w, conv_b, dt_bias, a_log, d_skip, ssd_gain, ret_gain, w_out,
              norm_mem, w_mq, w_mk, w_mv, w_mo, norm_ffn, w_rg, b_rg, w_re, b_re, w_gate, w_up, w_down,
              norm_final):
    weights = (norm_mix, w_in, conv_w, conv_b, dt_bias, a_log, d_skip, ssd_gain, ret_gain, w_out,
               norm_mem, w_mq, w_mo, norm_ffn, w_rg, b_rg, w_re, b_re, w_gate, w_up, w_down, norm_final)
    bp = x_prompt.shape[0]
    mem_k_prompt = jnp.stack([mem_proj(mem_prompt, w_mk[i]) for i in range(DEPTH)])
    mem_v_prompt = jnp.stack([mem_proj(mem_prompt, w_mv[i]) for i in range(DEPTH)])
    conv0 = jnp.zeros((DEPTH, bp, CONV_WIDTH - 1, CONV_DIM), x_prompt.dtype)
    ssm0 = jnp.zeros((DEPTH, bp, SSD_HEADS, SSD_HEAD_DIM, SSD_STATE), jnp.float32)
    ret0 = jnp.zeros((DEPTH, bp, RET_HEADS, RET_QK_DIM, RET_V_DIM), jnp.float32)
    y_prompt, ssm_prompt, conv_prompt, ret_prompt = trunk(x_prompt, 0, mem_k_prompt, mem_v_prompt,
                                                          conv0, ssm0, ret0, *weights)
    y_sample, ssm_sample, conv_sample, ret_sample = trunk(x_sample, PAST_LEN, cache_mem_k, cache_mem_v,
                                                          state_conv, state_ssm, state_ret, *weights)
    return (y_prompt, y_sample, ssm_prompt, conv_prompt, ret_prompt, mem_k_prompt, mem_v_prompt,
            ssm_sample, conv_sample, ret_sample)
```

```python
import functools
import math

import numpy as np
import jax
import jax.numpy as jnp
from jax import lax
from jax.experimental import pallas as pl
from jax.experimental.pallas import tpu as pltpu

F32 = jnp.float32
BF16 = jnp.bfloat16

D_MODEL = 1024
DEPTH = 2
PAST_LEN = 16384
SSD_HEAD_DIM = 64
SSD_HEADS = 16
SSD_GROUPS = 2
SSD_STATE = 128
SSD_WIDTH = 1024
GROUP_WIDTH = SSD_WIDTH // SSD_GROUPS
CONV_WIDTH = 4
CONV_DIM = SSD_WIDTH + 2 * SSD_GROUPS * SSD_STATE
RET_HEADS = 4
RET_V_DIM = 256
RET_QK_DIM = 128
RET_WIDTH = 1024
RET_QK_WIDTH = RET_HEADS * RET_QK_DIM
ROPE_BASE = 10000.0
N_MEM = 256
XA_HEADS = 4
XA_HEAD_DIM = 256
N_EGROUPS = 4
EXPERTS_PER_GROUP = 4
N_EXPERTS = 16
EXPERT_FF = 512
RMS_EPS = 1e-6

LANES = 128
CHUNK = 128
OFF_Z = 0
OFF_XBC = OFF_Z + SSD_WIDTH
OFF_Q = OFF_XBC + CONV_DIM
OFF_K = OFF_Q + RET_QK_WIDTH
OFF_V = OFF_K + RET_QK_WIDTH
OFF_G = OFF_V + RET_WIDTH
PROJ_MAIN = OFF_G + RET_WIDTH
INPROJ_TN = 512
ROUTER_OFF = N_EGROUPS

VMEM_LIMIT = 56 * 1024 * 1024


def _cparams(sem):
    return pltpu.CompilerParams(dimension_semantics=sem, vmem_limit_bytes=VMEM_LIMIT)


def _const_spec(shape):
    nd = len(shape)
    return pl.BlockSpec(shape, lambda *_: (0,) * nd, pipeline_mode=pl.Buffered(1))


def _rms(x):
    return x * lax.rsqrt(jnp.mean(x * x, axis=-1, keepdims=True) + RMS_EPS)


def _silu(x):
    return x * jax.nn.sigmoid(x)


def _softplus(x):
    return jnp.maximum(x, 0.0) + jnp.log1p(jnp.exp(-jnp.abs(x)))


def _split3(x):
    hi = x.astype(BF16)
    r = x - hi.astype(F32)
    mid = r.astype(BF16)
    lo = (r - mid.astype(F32)).astype(BF16)
    return hi, mid, lo


def _dot(a, b):
    return jnp.dot(a, b, preferred_element_type=F32)


def _dot_nt(a, b):
    return lax.dot_general(a, b, (((1,), (1,)), ((), ())), preferred_element_type=F32)


def _dot_sel(x, sel):
    hi, mid, lo = _split3(x)
    return _dot(hi, sel) + _dot(mid, sel) + _dot(lo, sel)


def _sel_dot(sel, x):
    hi, mid, lo = _split3(x)
    return _dot(sel, hi) + _dot(sel, mid) + _dot(sel, lo)


def _inproj_kernel(x_ref, g_ref, w_ref, wdt_ref, o_ref, odt_ref):
    h = (_rms(x_ref[...]) * g_ref[...]).astype(BF16)
    for j in range(PROJ_MAIN // INPROJ_TN):
        sl = slice(j * INPROJ_TN, (j + 1) * INPROJ_TN)
        o_ref[:, sl] = _dot(h, w_ref[:, sl]).astype(o_ref.dtype)
    odt_ref[...] = _dot(h, wdt_ref[...])


def _inproj(x, gain, w_main, w_dt, out_dtype, tm):
    t = x.shape[0]
    return pl.pallas_call(
        _inproj_kernel,
        grid=(t // tm,),
        in_specs=[
            pl.BlockSpec((tm, D_MODEL), lambda i: (i, 0)),
            _const_spec((1, D_MODEL)),
            _const_spec((D_MODEL, PROJ_MAIN)),
            _const_spec((D_MODEL, LANES)),
        ],
        out_specs=[
            pl.BlockSpec((tm, PROJ_MAIN), lambda i: (i, 0)),
            pl.BlockSpec((tm, LANES), lambda i: (i, 0)),
        ],
        out_shape=[
            jax.ShapeDtypeStruct((t, PROJ_MAIN), out_dtype),
            jax.ShapeDtypeStruct((t, LANES), F32),
        ],
        compiler_params=_cparams(("parallel",)),
        name="inproj",
    )(x, gain, w_main, w_dt)


def _dt_terms(dt_raw, dtb, alog, tri, expand):
    dt = _softplus(dt_raw + dtb)
    a = dt * (-jnp.exp(alog))
    acum = _sel_dot(tri, a)
    return dt, acum


def _ssd_out(y, xs, z, dskip, gain):
    y = (y + dskip * xs) * _silu(z)
    parts = []
    for g in range(SSD_GROUPS):
        parts.append(_rms(y[:, g * GROUP_WIDTH:(g + 1) * GROUP_WIDTH]))
    return jnp.concatenate(parts, axis=-1) * gain


def _rotary(x, cos, sin_signed):
    parts = []
    for h in range(RET_HEADS):
        xh = x[:, h * RET_QK_DIM:(h + 1) * RET_QK_DIM]
        parts.append(xh * cos + pltpu.roll(xh, RET_QK_DIM // 2, axis=1) * sin_signed)
    return parts


def _ret_out(o_heads, g, gain):
    o = jnp.concatenate([_rms(o) for o in o_heads], axis=-1)
    return o * gain * _silu(g)


def _mixer_prompt_kernel(proj_ref, dt_ref, convw_ref, convb_ref, dtb_ref, alog_ref, exp_ref, dskip_ref,
                         sgain_ref, rgain_ref, cos_ref, sin_ref, dmat_ref, kdec_ref, qdec_ref,
                         y_ref, conv_ref, ssm_ref, ret_ref, xp_scr, st_scr, *, ret_chunk_decay):
    cl = CHUNK
    c = pl.program_id(1)
    nc = pl.num_programs(1)

    @pl.when(c == 0)
    def _():
        xp_scr[0:8, :] = jnp.zeros((8, CONV_DIM), F32)
        st_scr[...] = jnp.zeros_like(st_scr)
        ret_ref[...] = jnp.zeros_like(ret_ref)

    xp_scr[8:8 + cl, :] = proj_ref[:, OFF_XBC:OFF_XBC + CONV_DIM].astype(F32)
    acc = convb_ref[...] + xp_scr[8 - (CONV_WIDTH - 1):8 - (CONV_WIDTH - 1) + cl, :] * convw_ref[0:1, :]
    for j in range(1, CONV_WIDTH):
        s = 8 - (CONV_WIDTH - 1) + j
        acc = acc + xp_scr[s:s + cl, :] * convw_ref[j:j + 1, :]
    conv_ref[0] = xp_scr[8 + cl - (CONV_WIDTH - 1):8 + cl, :]
    xp_scr[0:8, :] = xp_scr[cl:cl + 8, :]
    xbc = _silu(acc)
    xs = xbc[:, 0:SSD_WIDTH]
    bm = xbc[:, SSD_WIDTH:SSD_WIDTH + SSD_GROUPS * SSD_STATE]
    cm = xbc[:, SSD_WIDTH + SSD_GROUPS * SSD_STATE:CONV_DIM]

    row = lax.broadcasted_iota(jnp.int32, (cl, cl), 0)
    col = lax.broadcasted_iota(jnp.int32, (cl, cl), 1)
    causal = row >= col
    tri = jnp.where(causal, 1.0, 0.0).astype(BF16)
    expand = exp_ref[...]
    dt, acum = _dt_terms(dt_ref[...], dtb_ref[...], alog_ref[...], tri, expand)
    acum_t = acum.T
    eacum = jnp.exp(acum)
    dt_x = _dot_sel(dt, expand)
    eacum_x = _dot_sel(eacum, expand)
    dte_x = _dot_sel(jnp.exp(acum[cl - 1:cl, :] - acum), expand)
    xdt = xs * dt_x
    xdt_b = xdt.astype(BF16)
    xdtd_b = (xdt * dte_x).astype(BF16)
    lane = lax.broadcasted_iota(jnp.int32, (cl, LANES), 1)

    y_parts = []
    upd_parts = []
    for g in range(SSD_GROUPS):
        bg = bm[:, g * SSD_STATE:(g + 1) * SSD_STATE]
        cg_b = cm[:, g * SSD_STATE:(g + 1) * SSD_STATE].astype(BF16)
        bg_b = bg.astype(BF16)
        cb = _dot_nt(cg_b, bg_b)
        gsl = slice(g * GROUP_WIDTH, (g + 1) * GROUP_WIDTH)
        y_off = _dot(cg_b, st_scr[:, gsl].astype(BF16)) * eacum_x[:, gsl]
        upd_parts.append(_dot(bg.T.astype(BF16), xdtd_b[:, gsl]))
        for j in range(GROUP_WIDTH // LANES):
            h0 = g * (SSD_HEADS // SSD_GROUPS) + 2 * j
            psl = slice(h0 * SSD_HEAD_DIM, (h0 + 2) * SSD_HEAD_DIM)
            res = []
            for hh in (h0, h0 + 1):
                seg = acum[:, hh:hh + 1] - acum_t[hh:hh + 1, :]
                lmat = jnp.exp(jnp.where(causal, seg, -jnp.inf))
                res.append(_dot((cb * lmat).astype(BF16), xdt_b[:, psl]))
            y_parts.append(jnp.where(lane < SSD_HEAD_DIM, res[0], res[1]) + y_off[:, j * LANES:(j + 1) * LANES])
    y = jnp.concatenate(y_parts, axis=-1)
    st_new = st_scr[...] * eacum_x[cl - 1:cl, :] + jnp.concatenate(upd_parts, axis=-1)
    st_scr[...] = st_new

    @pl.when(c == nc - 1)
    def _():
        ssm_ref[0] = st_new.T

    z = proj_ref[:, OFF_Z:OFF_Z + SSD_WIDTH].astype(F32)
    y_ref[:, 0:SSD_WIDTH] = _ssd_out(y, xs, z, dskip_ref[...], sgain_ref[...]).astype(y_ref.dtype)

    cos = cos_ref[...]
    sin = sin_ref[...]
    q_heads = _rotary(proj_ref[:, OFF_Q:OFF_Q + RET_QK_WIDTH].astype(F32), cos, sin)
    k_heads = _rotary(proj_ref[:, OFF_K:OFF_K + RET_QK_WIDTH].astype(F32), cos, sin)
    o_heads = []
    for h in range(RET_HEADS):
        qh = q_heads[h]
        kh = k_heads[h] * (RET_QK_DIM ** -0.5)
        vh = proj_ref[:, OFF_V + h * RET_V_DIM:OFF_V + (h + 1) * RET_V_DIM]
        hs = slice(h * RET_QK_DIM, (h + 1) * RET_QK_DIM)
        scores = _dot_nt(qh.astype(BF16), kh.astype(BF16)) * dmat_ref[h]
        o_intra = _dot(scores.astype(BF16), vh)
        s_in = ret_ref[0, h]
        o_cross = _dot((qh * qdec_ref[:, hs]).astype(BF16), s_in.astype(BF16))
        kv = _dot((kh * kdec_ref[:, hs]).T.astype(BF16), vh)
        ret_ref[0, h] = ret_chunk_decay[h] * s_in + kv
        o_heads.append(o_intra + o_cross)
    gate = proj_ref[:, OFF_G:OFF_G + RET_WIDTH].astype(F32)
    y_ref[:, SSD_WIDTH:SSD_WIDTH + RET_WIDTH] = _ret_out(o_heads, gate, rgain_ref[...]).astype(y_ref.dtype)


def _ret_tables(cl, reps):
    lg = np.log(np.float32(1.0) - np.exp2(np.float32(-5.0) - np.arange(RET_HEADS, dtype=np.float32))).astype(np.float32)
    idx = np.arange(cl, dtype=np.float32)
    rel = idx[:, None] - idx[None, :]
    dmat = np.where(rel[None] >= 0, np.exp((rel[None] * lg[:, None, None]).astype(np.float32)), 0.0).astype(np.float32)
    kdec = np.exp(((cl - 1 - idx)[:, None] * lg[None, :]).astype(np.float32)).astype(np.float32)
    qdec = np.exp(((idx + 1.0)[:, None] * lg[None, :]).astype(np.float32)).astype(np.float32)
    kdec = np.tile(np.repeat(kdec, RET_QK_DIM, axis=1), (reps, 1))
    qdec = np.tile(np.repeat(qdec, RET_QK_DIM, axis=1), (reps, 1))
    chunk_decay = [float(v) for v in np.exp((np.float32(cl) * lg).astype(np.float32)).astype(np.float32)]
    return dmat, kdec, qdec, chunk_decay


def _rope_tables(pos0, length, reps):
    half = RET_QK_DIM // 2
    inv = (np.float32(ROPE_BASE) ** (-np.arange(half, dtype=np.float32) / np.float32(half))).astype(np.float32)
    pos = (pos0 + np.arange(length)).astype(np.float32)
    ang = (pos[:, None] * inv[None, :]).astype(np.float32).astype(np.float64)
    cos = np.cos(ang).astype(np.float32)
    sin = np.sin(ang).astype(np.float32)
    cos2 = np.tile(np.concatenate([cos, cos], axis=1), (reps, 1))
    sin2 = np.tile(np.concatenate([-sin, sin], axis=1), (reps, 1))
    return cos2, sin2


def _expand_matrix():
    e = np.zeros((LANES, SSD_WIDTH), np.float32)
    for h in range(SSD_HEADS):
        e[h, h * SSD_HEAD_DIM:(h + 1) * SSD_HEAD_DIM] = 1.0
    return e


def _mixer_prompt(proj, dt_raw, lw, batch, seq):
    nc = seq // CHUNK
    dmat, kdec, qdec, chunk_decay = _ret_tables(CHUNK, 1)
    cos, sin = _rope_tables(0, seq, 1)
    t = batch * seq
    row_map = lambda b, c: (b * nc + c, 0)
    kern = functools.partial(_mixer_prompt_kernel, ret_chunk_decay=chunk_decay)
    return pl.pallas_call(
        kern,
        grid=(batch, nc),
        in_specs=[
            pl.BlockSpec((CHUNK, PROJ_MAIN), row_map),
            pl.BlockSpec((CHUNK, LANES), row_map),
            _const_spec((CONV_WIDTH, CONV_DIM)),
            _const_spec((1, CONV_DIM)),
            _const_spec((1, LANES)),
            _const_spec((1, LANES)),
            _const_spec((LANES, SSD_WIDTH)),
            _const_spec((1, SSD_WIDTH)),
            _const_spec((1, SSD_WIDTH)),
            _const_spec((1, RET_WIDTH)),
            pl.BlockSpec((CHUNK, LANES), lambda b, c: (c, 0)),
            pl.BlockSpec((CHUNK, LANES), lambda b, c: (c, 0)),
            _const_spec((RET_HEADS, CHUNK, CHUNK)),
            _const_spec((CHUNK, RET_QK_WIDTH)),
            _const_spec((CHUNK, RET_QK_WIDTH)),
        ],
        out_specs=[
            pl.BlockSpec((CHUNK, SSD_WIDTH + RET_WIDTH), row_map),
            pl.BlockSpec((1, CONV_WIDTH - 1, CONV_DIM), lambda b, c: (b, 0, 0)),
            pl.BlockSpec((1, SSD_WIDTH, SSD_STATE), lambda b, c: (b, 0, 0)),
            pl.BlockSpec((1, RET_HEADS, RET_QK_DIM, RET_V_DIM), lambda b, c: (b, 0, 0, 0)),
        ],
        out_shape=[
            jax.ShapeDtypeStruct((t, SSD_WIDTH + RET_WIDTH), BF16),
            jax.ShapeDtypeStruct((batch, CONV_WIDTH - 1, CONV_DIM), F32),
            jax.ShapeDtypeStruct((batch, SSD_WIDTH, SSD_STATE), F32),
            jax.ShapeDtypeStruct((batch, RET_HEADS, RET_QK_DIM, RET_V_DIM), F32),
        ],
        scratch_shapes=[
            pltpu.VMEM((CHUNK + 8, CONV_DIM), F32),
            pltpu.VMEM((SSD_STATE, SSD_WIDTH), F32),
        ],
        compiler_params=_cparams(("parallel", "arbitrary")),
        name="mixer_prompt",
    )(proj, dt_raw, lw["conv_w"], lw["conv_b"], lw["dt_bias"], lw["a_log"], lw["expand"], lw["d_skip"],
      lw["ssd_gain"], lw["ret_gain"], jnp.asarray(cos), jnp.asarray(sin), jnp.asarray(dmat),
      jnp.asarray(kdec), jnp.asarray(qdec))


def _pad_rows(x, rows):
    return jnp.concatenate([x, jnp.zeros((rows - x.shape[0], x.shape[1]), x.dtype)], axis=0)


def _mixer_sample_kernel(proj_ref, dt_ref, convs_ref, ssm_in_ref, ret_in_ref, convw_ref, convb_ref, dtb_ref,
                         alog_ref, exp_ref, dskip_ref, sgain_ref, rgain_ref, cos_ref, sin_ref, dmat_ref,
                         kdec_ref, qdec_ref, tri_ref,
                         y_ref, conv_ref, ssm_ref, ret_ref, xp_scr, *, bb, cl, ret_chunk_decay):
    for i in range(bb):
        xp_scr[i, 8 - (CONV_WIDTH - 1):8, :] = convs_ref[i]
        xp_scr[i, 8:8 + cl, :] = proj_ref[i * cl:(i + 1) * cl, OFF_XBC:OFF_XBC + CONV_DIM]
        conv_ref[i] = xp_scr[i, 8 + cl - (CONV_WIDTH - 1):8 + cl, :]
    acc = None
    for j in range(CONV_WIDTH):
        s = 8 - (CONV_WIDTH - 1) + j
        tap = jnp.concatenate([xp_scr[i, s:s + cl, :] for i in range(bb)], axis=0) * convw_ref[j:j + 1, :]
        acc = convb_ref[...] + tap if acc is None else acc + tap
    xbc = _silu(acc)
    xs = xbc[:, 0:SSD_WIDTH]
    bm = xbc[:, SSD_WIDTH:SSD_WIDTH + SSD_GROUPS * SSD_STATE]
    cm = xbc[:, SSD_WIDTH + SSD_GROUPS * SSD_STATE:CONV_DIM]

    expand = exp_ref[...]
    dt, acum = _dt_terms(dt_ref[...], dtb_ref[...], alog_ref[...], tri_ref[...], expand)
    eacum = jnp.exp(acum)
    alast = jnp.concatenate(
        [jnp.broadcast_to(acum[(i + 1) * cl - 1:(i + 1) * cl, :], (cl, LANES)) for i in range(bb)], axis=0)
    dt_x = _dot_sel(dt, expand)
    eacum_x = _dot_sel(eacum, expand)
    dte_x = _dot_sel(jnp.exp(alast - acum), expand)
    xdt = xs * dt_x
    xdtd = xdt * dte_x
    row = lax.broadcasted_iota(jnp.int32, (cl, LANES), 0)
    col = lax.broadcasted_iota(jnp.int32, (cl, LANES), 1)
    causal = row >= col
    lane = col

    y_rows = []
    for i in range(bb):
        rs = slice(i * cl, (i + 1) * cl)
        acum_i = acum[rs]
        acum_t = _pad_rows(acum_i, LANES).T
        xdt_p = _pad_rows(xdt[rs], LANES).astype(BF16)
        xdtd_p = _pad_rows(xdtd[rs], LANES)
        chunk_decay = eacum[(i + 1) * cl - 1:(i + 1) * cl, :]
        y_parts = []
        for g in range(SSD_GROUPS):
            bg_p = _pad_rows(bm[rs, g * SSD_STATE:(g + 1) * SSD_STATE], LANES)
            cg_b = cm[rs, g * SSD_STATE:(g + 1) * SSD_STATE].astype(BF16)
            cb = _dot_nt(cg_b, bg_p.astype(BF16))
            gsl = slice(g * GROUP_WIDTH, (g + 1) * GROUP_WIDTH)
            st_g = ssm_in_ref[i, gsl, :]
            y_off = _dot_nt(cg_b, st_g.astype(BF16)) * eacum_x[rs, gsl]
            upd = _dot(xdtd_p[:, gsl].T.astype(BF16), bg_p.astype(BF16))
            for hh in range(SSD_HEADS // SSD_GROUPS):
                h = g * (SSD_HEADS // SSD_GROUPS) + hh
                hsl = slice(hh * SSD_HEAD_DIM, (hh + 1) * SSD_HEAD_DIM)
                ssm_ref[i, h * SSD_HEAD_DIM:(h + 1) * SSD_HEAD_DIM, :] = (
                    st_g[hsl, :] * chunk_decay[:, h:h + 1] + upd[hsl, :])
            for j in range(GROUP_WIDTH // LANES):
                h0 = g * (SSD_HEADS // SSD_GROUPS) + 2 * j
                psl = slice(h0 * SSD_HEAD_DIM, (h0 + 2) * SSD_HEAD_DIM)
                res = []
                for hh in (h0, h0 + 1):
                    seg = acum_i[:, hh:hh + 1] - acum_t[hh:hh + 1, :]
                    lmat = jnp.exp(jnp.where(causal, seg, -jnp.inf))
                    res.append(_dot((cb * lmat).astype(BF16), xdt_p[:, psl]))
                y_parts.append(jnp.where(lane < SSD_HEAD_DIM, res[0], res[1]) + y_off[:, j * LANES:(j + 1) * LANES])
        y_rows.append(jnp.concatenate(y_parts, axis=-1))
    y = jnp.concatenate(y_rows, axis=0)
    z = proj_ref[:, OFF_Z:OFF_Z + SSD_WIDTH]
    y_ref[:, 0:SSD_WIDTH] = _ssd_out(y, xs, z, dskip_ref[...], sgain_ref[...]).astype(y_ref.dtype)

    cos = cos_ref[...]
    sin = sin_ref[...]
    q_heads = _rotary(proj_ref[:, OFF_Q:OFF_Q + RET_QK_WIDTH], cos, sin)
    k_heads = _rotary(proj_ref[:, OFF_K:OFF_K + RET_QK_WIDTH], cos, sin)
    o_heads = []
    for h in range(RET_HEADS):
        hs = slice(h * RET_QK_DIM, (h + 1) * RET_QK_DIM)
        kh_all = k_heads[h] * (RET_QK_DIM ** -0.5)
        q_start = q_heads[h] * qdec_ref[:, hs]
        k_end = kh_all * kdec_ref[:, hs]
        o_rows = []
        for i in range(bb):
            rs = slice(i * cl, (i + 1) * cl)
            vh_p = _pad_rows(proj_ref[rs, OFF_V + h * RET_V_DIM:OFF_V + (h + 1) * RET_V_DIM], LANES).astype(BF16)
            kh_p = _pad_rows(kh_all[rs], LANES).astype(BF16)
            scores = _dot_nt(q_heads[h][rs].astype(BF16), kh_p) * dmat_ref[h]
            o_intra = _dot(scores.astype(BF16), vh_p)
            s_in = ret_in_ref[i, h]
            o_cross = _dot(q_start[rs].astype(BF16), s_in.astype(BF16))
            kv = _dot(_pad_rows(k_end[rs], LANES).T.astype(BF16), vh_p)
            ret_ref[i, h] = ret_chunk_decay[h] * s_in + kv
            o_rows.append(o_intra + o_cross)
        o_heads.append(jnp.concatenate(o_rows, axis=0))
    gate = proj_ref[:, OFF_G:OFF_G + RET_WIDTH]
    y_ref[:, SSD_WIDTH:SSD_WIDTH + RET_WIDTH] = _ret_out(o_heads, gate, rgain_ref[...]).astype(y_ref.dtype)


def _mixer_sample(proj, dt_raw, conv_state, ssm_state, ret_state, lw, batch, cl, bb):
    dmat, kdec, qdec, chunk_decay = _ret_tables(cl, bb)
    dmat = np.concatenate([dmat, np.zeros((RET_HEADS, cl, LANES - cl), np.float32)], axis=-1)
    cos, sin = _rope_tables(PAST_LEN, cl, bb)
    m = bb * cl
    tri = np.kron(np.eye(bb, dtype=np.float32), np.tril(np.ones((cl, cl), np.float32)))
    kern = functools.partial(_mixer_sample_kernel, bb=bb, cl=cl, ret_chunk_decay=chunk_decay)
    row_map = lambda i: (i, 0)
    ssm2 = ssm_state.reshape(batch, SSD_WIDTH, SSD_STATE)
    y, conv_new, ssm_new, ret_new = pl.pallas_call(
        kern,
        grid=(batch // bb,),
        in_specs=[
            pl.BlockSpec((m, PROJ_MAIN), row_map),
            pl.BlockSpec((m, LANES), row_map),
            pl.BlockSpec((bb, CONV_WIDTH - 1, CONV_DIM), lambda i: (i, 0, 0)),
            pl.BlockSpec((bb, SSD_WIDTH, SSD_STATE), lambda i: (i, 0, 0)),
            pl.BlockSpec((bb, RET_HEADS, RET_QK_DIM, RET_V_DIM), lambda i: (i, 0, 0, 0)),
            _const_spec((CONV_WIDTH, CONV_DIM)),
            _const_spec((1, CONV_DIM)),
            _const_spec((1, LANES)),
            _const_spec((1, LANES)),
            _const_spec((LANES, SSD_WIDTH)),
            _const_spec((1, SSD_WIDTH)),
            _const_spec((1, SSD_WIDTH)),
            _const_spec((1, RET_WIDTH)),
            _const_spec((m, LANES)),
            _const_spec((m, LANES)),
            _const_spec((RET_HEADS, cl, LANES)),
            _const_spec((m, RET_QK_WIDTH)),
            _const_spec((m, RET_QK_WIDTH)),
            _const_spec((m, m)),
        ],
        out_specs=[
            pl.BlockSpec((m, SSD_WIDTH + RET_WIDTH), row_map),
            pl.BlockSpec((bb, CONV_WIDTH - 1, CONV_DIM), lambda i: (i, 0, 0)),
            pl.BlockSpec((bb, SSD_WIDTH, SSD_STATE), lambda i: (i, 0, 0)),
            pl.BlockSpec((bb, RET_HEADS, RET_QK_DIM, RET_V_DIM), lambda i: (i, 0, 0, 0)),
        ],
        out_shape=[
            jax.ShapeDtypeStruct((batch * cl, SSD_WIDTH + RET_WIDTH), BF16),
            jax.ShapeDtypeStruct((batch, CONV_WIDTH - 1, CONV_DIM), F32),
            jax.ShapeDtypeStruct((batch, SSD_WIDTH, SSD_STATE), F32),
            jax.ShapeDtypeStruct((batch, RET_HEADS, RET_QK_DIM, RET_V_DIM), F32),
        ],
        scratch_shapes=[pltpu.VMEM((bb, 8 + cl, CONV_DIM), F32)],
        compiler_params=_cparams(("parallel",)),
        name="mixer_sample",
    )(proj, dt_raw, conv_state, ssm2, ret_state, lw["conv_w"], lw["conv_b"], lw["dt_bias"], lw["a_log"],
      lw["expand"], lw["d_skip"], lw["ssd_gain"], lw["ret_gain"], jnp.asarray(cos), jnp.asarray(sin),
      jnp.asarray(dmat), jnp.asarray(kdec), jnp.asarray(qdec), jnp.asarray(tri, dtype=BF16))
    return y, conv_new, ssm_new.reshape(batch, SSD_HEADS, SSD_HEAD_DIM, SSD_STATE), ret_new


def _outproj_kernel(x_ref, y_ref, w_ref, o_ref):
    o_ref[...] = x_ref[...] + _dot(y_ref[...], w_ref[...])


def _outproj(x, y, w, tm):
    t = x.shape[0]
    k = y.shape[1]
    return pl.pallas_call(
        _outproj_kernel,
        grid=(t // tm,),
        in_specs=[
            pl.BlockSpec((tm, D_MODEL), lambda i: (i, 0)),
            pl.BlockSpec((tm, k), lambda i: (i, 0)),
            _const_spec((k, D_MODEL)),
        ],
        out_specs=pl.BlockSpec((tm, D_MODEL), lambda i: (i, 0)),
        out_shape=jax.ShapeDtypeStruct((t, D_MODEL), F32),
        compiler_params=_cparams(("parallel",)),
        name="outproj",
    )(x, y, w)


def _memproj_kernel(m_ref, w_ref, k_ref, v_ref):
    r = _dot(m_ref[...].astype(BF16), w_ref[...])
    k_ref[...] = r[:, 0:D_MODEL]
    v_ref[...] = r[:, D_MODEL:2 * D_MODEL]


def _memproj(mem, w_kv, tm):
    t = mem.shape[0]
    return pl.pallas_call(
        _memproj_kernel,
        grid=(t // tm,),
        in_specs=[pl.BlockSpec((tm, D_MODEL), lambda i: (i, 0)), _const_spec((D_MODEL, 2 * D_MODEL))],
        out_specs=[pl.BlockSpec((tm, D_MODEL), lambda i: (i, 0))] * 2,
        out_shape=[jax.ShapeDtypeStruct((t, D_MODEL), F32)] * 2,
        compiler_params=_cparams(("parallel",)),
        name="memproj",
    )(mem, w_kv)


def _xattn_kernel(x_ref, g_ref, wq_ref, wo_ref, k_ref, v_ref, o_ref, *, bb, tm):
    x = x_ref[...].reshape(bb * tm, D_MODEL)
    h = (_rms(x) * g_ref[...]).astype(BF16)
    q = _dot(h, wq_ref[...])
    scale = XA_HEAD_DIM ** -0.5
    rows = []
    for i in range(bb):
        heads = []
        for hd in range(XA_HEADS):
            hs = slice(hd * XA_HEAD_DIM, (hd + 1) * XA_HEAD_DIM)
            qh = q[i * tm:(i + 1) * tm, hs].astype(BF16)
            s = _dot_nt(qh, k_ref[i, :, hs].astype(BF16)) * scale
            e = jnp.exp(s - jnp.max(s, axis=-1, keepdims=True))
            p = e / jnp.sum(e, axis=-1, keepdims=True)
            heads.append(_dot(p.astype(BF16), v_ref[i, :, hs].astype(BF16)))
        rows.append(jnp.concatenate(heads, axis=-1))
    o = jnp.concatenate(rows, axis=0).astype(BF16)
    o_ref[...] = (x + _dot(o, wo_ref[...])).reshape(bb, tm, D_MODEL)


def _xattn(x, gain, wq, wo, mem_k, mem_v, bb, tm):
    b, l, _ = x.shape
    kern = functools.partial(_xattn_kernel, bb=bb, tm=tm)
    return pl.pallas_call(
        kern,
        grid=(b // bb, l // tm),
        in_specs=[
            pl.BlockSpec((bb, tm, D_MODEL), lambda i, j: (i, j, 0)),
            _const_spec((1, D_MODEL)),
            _const_spec((D_MODEL, D_MODEL)),
            _const_spec((D_MODEL, D_MODEL)),
            pl.BlockSpec((bb, N_MEM, D_MODEL), lambda i, j: (i, 0, 0)),
            pl.BlockSpec((bb, N_MEM, D_MODEL), lambda i, j: (i, 0, 0)),
        ],
        out_specs=pl.BlockSpec((bb, tm, D_MODEL), lambda i, j: (i, j, 0)),
        out_shape=jax.ShapeDtypeStruct((b, l, D_MODEL), F32),
        compiler_params=_cparams(("parallel", "parallel")),
        name="xattn",
    )(x, gain, wq, wo, mem_k, mem_v)


def _router_gates(logits):
    m = logits.shape[0]
    lane_i = lax.broadcasted_iota(jnp.int32, (m, LANES), 1)
    lane = lane_i.astype(F32)
    big = float(LANES)
    is_g = lane_i < N_EGROUPS
    gl = jnp.where(is_g, logits, -jnp.inf)
    gmax = jnp.max(gl, axis=-1, keepdims=True)
    g_idx = jnp.min(jnp.where(is_g & (gl == gmax), lane, big), axis=-1, keepdims=True)
    g_prob = 1.0 / jnp.sum(jnp.exp(gl - gmax), axis=-1, keepdims=True)
    e_lane = lane_i - ROUTER_OFF
    e_group = lax.shift_right_arithmetic(e_lane, int(math.log2(EXPERTS_PER_GROUP))).astype(F32)
    sel = (e_lane >= 0) & (e_lane < N_EXPERTS) & (e_group == g_idx)
    el = jnp.where(sel, logits, -jnp.inf)
    emax = jnp.max(el, axis=-1, keepdims=True)
    ee = jnp.exp(el - emax)
    e_prob = ee / jnp.sum(ee, axis=-1, keepdims=True)
    p1 = jnp.max(jnp.where(sel, e_prob, -1.0), axis=-1, keepdims=True)
    i1 = jnp.min(jnp.where(sel & (e_prob == p1), lane, big), axis=-1, keepdims=True)
    sel2 = sel & (lane != i1)
    p2 = jnp.max(jnp.where(sel2, e_prob, -1.0), axis=-1, keepdims=True)
    i2 = jnp.min(jnp.where(sel2 & (e_prob == p2), lane, big), axis=-1, keepdims=True)
    denom = p1 + p2
    w1 = g_prob * p1 / denom
    w2 = g_prob * p2 / denom
    return jnp.where(lane == i1, w1, 0.0) + jnp.where(lane == i2, w2, 0.0)


def _moe_kernel(x_ref, g_ref, wr_hi_ref, wr_lo_ref, br_ref, wg_ref, wu_ref, wd_ref, gf_ref, o_ref,
                h_scr, gate_scr, *, final_norm):
    e = pl.program_id(1)

    @pl.when(e == 0)
    def _():
        h = _rms(x_ref[...]) * g_ref[...]
        h_hi = h.astype(BF16)
        h_lo = (h - h_hi.astype(F32)).astype(BF16)
        logits = (_dot(h_hi, wr_hi_ref[...]) + _dot(h_hi, wr_lo_ref[...]) + _dot(h_lo, wr_hi_ref[...])
                  + br_ref[...])
        h_scr[...] = h_hi
        gate_scr[...] = _router_gates(logits)
        o_ref[...] = jnp.zeros_like(o_ref)

    hb = h_scr[...]
    lane = lax.broadcasted_iota(jnp.int32, gate_scr.shape, 1)
    gate_e = jnp.sum(jnp.where(lane == e + ROUTER_OFF, gate_scr[...], 0.0), axis=-1, keepdims=True)
    a = _silu(_dot(hb, wg_ref[0])) * _dot(hb, wu_ref[0]) * gate_e
    o_ref[...] += _dot(a.astype(BF16), wd_ref[0])

    @pl.when(e == pl.num_programs(1) - 1)
    def _():
        y = x_ref[...] + o_ref[...]
        if final_norm:
            y = _rms(y) * gf_ref[...]
        o_ref[...] = y


def _moe(x, gain, wr_hi, wr_lo, br, wg, wu, wd, gain_final, final_norm, tm):
    t = x.shape[0]
    kern = functools.partial(_moe_kernel, final_norm=final_norm)
    return pl.pallas_call(
        kern,
        grid=(t // tm, N_EXPERTS),
        in_specs=[
            pl.BlockSpec((tm, D_MODEL), lambda i, e: (i, 0)),
            _const_spec((1, D_MODEL)),
            _const_spec((D_MODEL, LANES)),
            _const_spec((D_MODEL, LANES)),
            _const_spec((1, LANES)),
            pl.BlockSpec((1, D_MODEL, EXPERT_FF), lambda i, e: (e, 0, 0)),
            pl.BlockSpec((1, D_MODEL, EXPERT_FF), lambda i, e: (e, 0, 0)),
            pl.BlockSpec((1, EXPERT_FF, D_MODEL), lambda i, e: (e, 0, 0)),
            _const_spec((1, D_MODEL)),
        ],
        out_specs=pl.BlockSpec((tm, D_MODEL), lambda i, e: (i, 0)),
        out_shape=jax.ShapeDtypeStruct((t, D_MODEL), F32),
        scratch_shapes=[pltpu.VMEM((tm, D_MODEL), BF16), pltpu.VMEM((tm, LANES), F32)],
        compiler_params=_cparams(("parallel", "arbitrary")),
        name="moe",
    )(x, gain, wr_hi, wr_lo, br, wg, wu, wd, gain_final)


def _pad_lanes(v):
    v = v.reshape(1, -1)
    return jnp.pad(v, ((0, 0), (0, LANES - v.shape[1])))


def _layer_weights(i, norm_mix, w_in, conv_w, conv_b, dt_bias, a_log, d_skip, ssd_gain, ret_gain, w_out,
                   norm_mem, w_mq, w_mk, w_mv, w_mo, norm_ffn, w_rg, b_rg, w_re, b_re, w_gate, w_up, w_down):
    wi = w_in[i]
    dt_off = SSD_WIDTH + CONV_DIM
    w_main = jnp.concatenate([wi[:, :dt_off], wi[:, dt_off + SSD_HEADS:]], axis=1).astype(BF16)
    w_dt = jnp.pad(wi[:, dt_off:dt_off + SSD_HEADS], ((0, 0), (0, LANES - SSD_HEADS))).astype(BF16)
    w_router = jnp.pad(jnp.concatenate([w_rg[i], w_re[i]], axis=1),
                       ((0, 0), (0, LANES - N_EGROUPS - N_EXPERTS)))
    wr_hi = w_router.astype(BF16)
    wr_lo = (w_router - wr_hi.astype(F32)).astype(BF16)
    return dict(
        norm_mix=norm_mix[i].reshape(1, -1), w_main=w_main, w_dt=w_dt,
        conv_w=conv_w[i], conv_b=conv_b[i].reshape(1, -1),
        dt_bias=_pad_lanes(dt_bias[i]), a_log=_pad_lanes(a_log[i]),
        expand=jnp.asarray(_expand_matrix(), dtype=BF16),
        d_skip=jnp.repeat(d_skip[i], SSD_HEAD_DIM).reshape(1, -1),
        ssd_gain=ssd_gain[i].reshape(1, -1), ret_gain=ret_gain[i].reshape(1, -1),
        w_out=w_out[i].astype(BF16),
        norm_mem=norm_mem[i].reshape(1, -1), w_mq=w_mq[i].astype(BF16), w_mo=w_mo[i].astype(BF16),
        w_kv=jnp.concatenate([w_mk[i], w_mv[i]], axis=1).astype(BF16),
        norm_ffn=norm_ffn[i].reshape(1, -1), wr_hi=wr_hi, wr_lo=wr_lo,
        b_router=_pad_lanes(jnp.concatenate([b_rg[i], b_re[i]])),
        w_gate=w_gate[i].astype(BF16), w_up=w_up[i].astype(BF16), w_down=w_down[i].astype(BF16),
    )


def _token_tile(t, cap):
    tm = min(t, cap)
    assert t % tm == 0
    return tm


def _trunk(x, mixer_fn, mem_kv, layers, norm_final, xattn_tiles):
    b, l, _ = x.shape
    t = b * l
    xf = x.reshape(t, D_MODEL)
    convs, ssms, rets = [], [], []
    for i, lw in enumerate(layers):
        proj, dt_raw = _inproj(xf, lw["norm_mix"], lw["w_main"], lw["w_dt"], lw["proj_dtype"],
                               _token_tile(t, 512))
        y, c_new, s_new, r_new = mixer_fn(i, proj, dt_raw, lw)
        xf = _outproj(xf, y, lw["w_out"], _token_tile(t, 512))
        mk, mv = mem_kv[i]
        xf = _xattn(xf.reshape(b, l, D_MODEL), lw["norm_mem"], lw["w_mq"], lw["w_mo"], mk, mv,
                    *xattn_tiles).reshape(t, D_MODEL)
        last = i == len(layers) - 1
        xf = _moe(xf, lw["norm_ffn"], lw["wr_hi"], lw["wr_lo"], lw["b_router"], lw["w_gate"], lw["w_up"],
                  lw["w_down"], norm_final.reshape(1, -1), last, _token_tile(t, 1024))
        convs.append(c_new)
        ssms.append(s_new)
        rets.append(r_new)
    return xf.reshape(b, l, D_MODEL), jnp.stack(ssms), jnp.stack(convs), jnp.stack(rets)


def kernel(x_prompt, x_sample, mem_prompt, state_ssm, state_conv, state_ret, cache_mem_k, cache_mem_v,
           norm_mix, w_in, conv_w, conv_b, dt_bias, a_log, d_skip, ssd_gain, ret_gain, w_out,
           norm_mem, w_mq, w_mk, w_mv, w_mo, norm_ffn, w_rg, b_rg, w_re, b_re, w_gate, w_up, w_down,
           norm_final):
    layers = [_layer_weights(i, norm_mix, w_in, conv_w, conv_b, dt_bias, a_log, d_skip, ssd_gain, ret_gain,
                             w_out, norm_mem, w_mq, w_mk, w_mv, w_mo, norm_ffn, w_rg, b_rg, w_re, b_re,
                             w_gate, w_up, w_down) for i in range(DEPTH)]
    bp, lp, _ = x_prompt.shape
    bs, ls, _ = x_sample.shape
    n_mem = mem_prompt.shape[1]

    mem_flat = mem_prompt.reshape(bp * n_mem, D_MODEL)
    mem_kv_p = [_memproj(mem_flat, lw["w_kv"], _token_tile(bp * n_mem, 512)) for lw in layers]
    mem_kv_p = [(k.reshape(bp, n_mem, D_MODEL), v.reshape(bp, n_mem, D_MODEL)) for k, v in mem_kv_p]

    def mixer_p(i, proj, dt_raw, lw):
        y, c_new, s_new, r_new = _mixer_prompt(proj, dt_raw, lw, bp, lp)
        return y, c_new, s_new.reshape(bp, SSD_HEADS, SSD_HEAD_DIM, SSD_STATE), r_new

    layers_p = [dict(lw, proj_dtype=BF16) for lw in layers]
    y_prompt, ssm_p, conv_p, ret_p = _trunk(x_prompt, mixer_p, mem_kv_p, layers_p, norm_final,
                                            (1, _token_tile(lp, 512)))

    sample_bb = 4
    mem_kv_s = [(cache_mem_k[i].reshape(bs, n_mem, D_MODEL), cache_mem_v[i].reshape(bs, n_mem, D_MODEL))
                for i in range(DEPTH)]

    def mixer_s(i, proj, dt_raw, lw):
        return _mixer_sample(proj, dt_raw, state_conv[i], state_ssm[i], state_ret[i], lw, bs, ls, sample_bb)

    layers_s = [dict(lw, proj_dtype=F32) for lw in layers]
    y_sample, ssm_s, conv_s, ret_s = _trunk(x_sample, mixer_s, mem_kv_s, layers_s, norm_final, (sample_bb, ls))

    shp = (DEPTH, bp, n_mem, XA_HEADS, XA_HEAD_DIM)
    mem_k_prompt = jnp.stack([k for k, _ in mem_kv_p]).reshape(shp)
    mem_v_prompt = jnp.stack([v for _, v in mem_kv_p]).reshape(shp)
    return (y_prompt, y_sample, ssm_p, conv_p, ret_p, mem_k_prompt, mem_v_prompt, ssm_s, conv_s, ret_s)
```

```python
import functools
import math

import numpy as np
import jax
import jax.numpy as jnp
from jax import lax
from jax.experimental import pallas as pl
from jax.experimental.pallas import tpu as pltpu

F32 = jnp.float32
BF16 = jnp.bfloat16

D_MODEL = 1024
DEPTH = 2
PAST_LEN = 16384
SSD_HEAD_DIM = 64
SSD_HEADS = 16
SSD_GROUPS = 2
SSD_STATE = 128
SSD_WIDTH = 1024
GROUP_WIDTH = SSD_WIDTH // SSD_GROUPS
CONV_WIDTH = 4
CONV_DIM = SSD_WIDTH + 2 * SSD_GROUPS * SSD_STATE
RET_HEADS = 4
RET_V_DIM = 256
RET_QK_DIM = 128
RET_WIDTH = 1024
RET_QK_WIDTH = RET_HEADS * RET_QK_DIM
ROPE_BASE = 10000.0
N_MEM = 256
XA_HEADS = 4
XA_HEAD_DIM = 256
N_EGROUPS = 4
EXPERTS_PER_GROUP = 4
N_EXPERTS = 16
EXPERT_FF = 512
RMS_EPS = 1e-6

LANES = 128
CHUNK = 128
OFF_Z = 0
OFF_XBC = OFF_Z + SSD_WIDTH
OFF_Q = OFF_XBC + CONV_DIM
OFF_K = OFF_Q + RET_QK_WIDTH
OFF_V = OFF_K + RET_QK_WIDTH
OFF_G = OFF_V + RET_WIDTH
PROJ_MAIN = OFF_G + RET_WIDTH
INPROJ_TN = 512
ROUTER_OFF = N_EGROUPS

VMEM_LIMIT = 56 * 1024 * 1024


def _cparams(sem):
    return pltpu.CompilerParams(dimension_semantics=sem, vmem_limit_bytes=VMEM_LIMIT)


def _const_spec(shape):
    nd = len(shape)
    return pl.BlockSpec(shape, lambda *_: (0,) * nd, pipeline_mode=pl.Buffered(1))


def _layer_spec(shape, layer):
    nd = len(shape)
    return pl.BlockSpec((None,) + tuple(shape), lambda *_: (layer,) + (0,) * nd, pipeline_mode=pl.Buffered(1))


def _alias_spec():
    return pl.BlockSpec(memory_space=pl.ANY)


def _rms(x):
    return x * lax.rsqrt(jnp.mean(x * x, axis=-1, keepdims=True) + RMS_EPS)


def _silu(x):
    return x * jax.nn.sigmoid(x)


def _softplus(x):
    return jnp.maximum(x, 0.0) + jnp.log1p(jnp.exp(-jnp.abs(x)))


def _split3(x):
    hi = x.astype(BF16)
    r = x - hi.astype(F32)
    mid = r.astype(BF16)
    lo = (r - mid.astype(F32)).astype(BF16)
    return hi, mid, lo


def _dot(a, b):
    return jnp.dot(a, b, preferred_element_type=F32)


def _dot_nt(a, b):
    return lax.dot_general(a, b, (((1,), (1,)), ((), ())), preferred_element_type=F32)


def _dot_sel(x, sel):
    hi, mid, lo = _split3(x)
    return _dot(hi, sel) + _dot(mid, sel) + _dot(lo, sel)


def _sel_dot(sel, x):
    hi, mid, lo = _split3(x)
    return _dot(sel, hi) + _dot(sel, mid) + _dot(sel, lo)


def _inproj_kernel(x_ref, g_ref, w_ref, wdt_ref, o_ref, odt_ref):
    h = (_rms(x_ref[...]) * g_ref[...]).astype(BF16)
    for j in range(PROJ_MAIN // INPROJ_TN):
        sl = slice(j * INPROJ_TN, (j + 1) * INPROJ_TN)
        o_ref[:, sl] = _dot(h, w_ref[:, sl]).astype(o_ref.dtype)
    odt_ref[...] = _dot(h, wdt_ref[...])


def _inproj(x, layer, pw, out_dtype, tm):
    t = x.shape[0]
    return pl.pallas_call(
        _inproj_kernel,
        grid=(t // tm,),
        in_specs=[
            pl.BlockSpec((tm, D_MODEL), lambda i: (i, 0)),
            _layer_spec((1, D_MODEL), layer),
            _layer_spec((D_MODEL, PROJ_MAIN), layer),
            _layer_spec((D_MODEL, LANES), layer),
        ],
        out_specs=[
            pl.BlockSpec((tm, PROJ_MAIN), lambda i: (i, 0)),
            pl.BlockSpec((tm, LANES), lambda i: (i, 0)),
        ],
        out_shape=[
            jax.ShapeDtypeStruct((t, PROJ_MAIN), out_dtype),
            jax.ShapeDtypeStruct((t, LANES), F32),
        ],
        compiler_params=_cparams(("parallel",)),
        name="inproj",
    )(x, pw["norm_mix"], pw["w_main"], pw["w_dt"])


def _dt_terms(dt_raw, dtb, alog, tri, expand):
    dt = _softplus(dt_raw + dtb)
    a = dt * (-jnp.exp(alog))
    acum = _sel_dot(tri, a)
    return dt, acum


def _ssd_out(y, xs, z, dskip, gain):
    y = (y + dskip * xs) * _silu(z)
    parts = []
    for g in range(SSD_GROUPS):
        parts.append(_rms(y[:, g * GROUP_WIDTH:(g + 1) * GROUP_WIDTH]))
    return jnp.concatenate(parts, axis=-1) * gain


def _rotary(x, cos, sin_signed):
    parts = []
    for h in range(RET_HEADS):
        xh = x[:, h * RET_QK_DIM:(h + 1) * RET_QK_DIM]
        parts.append(xh * cos + pltpu.roll(xh, RET_QK_DIM // 2, axis=1) * sin_signed)
    return parts


def _ret_out(o_heads, g, gain):
    o = jnp.concatenate([_rms(o) for o in o_heads], axis=-1)
    return o * gain * _silu(g)


def _mixer_prompt_kernel(proj_ref, dt_ref, convw_ref, convb_ref, dtb_ref, alog_ref, exp_ref, dskip_ref,
                         sgain_ref, rgain_ref, cos_ref, sin_ref, dmat_ref, kdec_ref, qdec_ref,
                         *rest, ret_chunk_decay):
    y_ref, conv_ref, ssm_ref, ret_ref, xp_scr, st_scr = rest[-6:]
    cl = CHUNK
    c = pl.program_id(1)
    nc = pl.num_programs(1)

    @pl.when(c == 0)
    def _():
        xp_scr[0:8, :] = jnp.zeros((8, CONV_DIM), F32)
        st_scr[...] = jnp.zeros_like(st_scr)
        ret_ref[...] = jnp.zeros_like(ret_ref)

    xp_scr[8:8 + cl, :] = proj_ref[:, OFF_XBC:OFF_XBC + CONV_DIM].astype(F32)
    acc = convb_ref[...] + xp_scr[8 - (CONV_WIDTH - 1):8 - (CONV_WIDTH - 1) + cl, :] * convw_ref[0:1, :]
    for j in range(1, CONV_WIDTH):
        s = 8 - (CONV_WIDTH - 1) + j
        acc = acc + xp_scr[s:s + cl, :] * convw_ref[j:j + 1, :]
    conv_ref[0] = xp_scr[8 + cl - (CONV_WIDTH - 1):8 + cl, :]
    xp_scr[0:8, :] = xp_scr[cl:cl + 8, :]
    xbc = _silu(acc)
    xs = xbc[:, 0:SSD_WIDTH]
    bm = xbc[:, SSD_WIDTH:SSD_WIDTH + SSD_GROUPS * SSD_STATE]
    cm = xbc[:, SSD_WIDTH + SSD_GROUPS * SSD_STATE:CONV_DIM]

    row = lax.broadcasted_iota(jnp.int32, (cl, cl), 0)
    col = lax.broadcasted_iota(jnp.int32, (cl, cl), 1)
    causal = row >= col
    tri = jnp.where(causal, 1.0, 0.0).astype(BF16)
    expand = exp_ref[...]
    dt, acum = _dt_terms(dt_ref[...], dtb_ref[...], alog_ref[...], tri, expand)
    acum_t = acum.T
    eacum = jnp.exp(acum)
    dt_x = _dot_sel(dt, expand)
    eacum_x = _dot_sel(eacum, expand)
    dte_x = _dot_sel(jnp.exp(acum[cl - 1:cl, :] - acum), expand)
    xdt = xs * dt_x
    xdt_b = xdt.astype(BF16)
    xdtd_b = (xdt * dte_x).astype(BF16)
    lane = lax.broadcasted_iota(jnp.int32, (cl, LANES), 1)

    y_parts = []
    upd_parts = []
    for g in range(SSD_GROUPS):
        bg = bm[:, g * SSD_STATE:(g + 1) * SSD_STATE]
        cg_b = cm[:, g * SSD_STATE:(g + 1) * SSD_STATE].astype(BF16)
        bg_b = bg.astype(BF16)
        cb = _dot_nt(cg_b, bg_b)
        gsl = slice(g * GROUP_WIDTH, (g + 1) * GROUP_WIDTH)
        y_off = _dot(cg_b, st_scr[:, gsl].astype(BF16)) * eacum_x[:, gsl]
        upd_parts.append(_dot(bg.T.astype(BF16), xdtd_b[:, gsl]))
        for j in range(GROUP_WIDTH // LANES):
            h0 = g * (SSD_HEADS // SSD_GROUPS) + 2 * j
            psl = slice(h0 * SSD_HEAD_DIM, (h0 + 2) * SSD_HEAD_DIM)
            res = []
            for hh in (h0, h0 + 1):
                seg = acum[:, hh:hh + 1] - acum_t[hh:hh + 1, :]
                lmat = jnp.exp(jnp.where(causal, seg, -jnp.inf))
                res.append(_dot((cb * lmat).astype(BF16), xdt_b[:, psl]))
            y_parts.append(jnp.where(lane < SSD_HEAD_DIM, res[0], res[1]) + y_off[:, j * LANES:(j + 1) * LANES])
    y = jnp.concatenate(y_parts, axis=-1)
    st_new = st_scr[...] * eacum_x[cl - 1:cl, :] + jnp.concatenate(upd_parts, axis=-1)
    st_scr[...] = st_new

    @pl.when(c == nc - 1)
    def _():
        ssm_ref[0] = st_new.T

    z = proj_ref[:, OFF_Z:OFF_Z + SSD_WIDTH].astype(F32)
    y_ref[:, 0:SSD_WIDTH] = _ssd_out(y, xs, z, dskip_ref[...], sgain_ref[...]).astype(y_ref.dtype)

    cos = cos_ref[...]
    sin = sin_ref[...]
    q_heads = _rotary(proj_ref[:, OFF_Q:OFF_Q + RET_QK_WIDTH].astype(F32), cos, sin)
    k_heads = _rotary(proj_ref[:, OFF_K:OFF_K + RET_QK_WIDTH].astype(F32), cos, sin)
    o_heads = []
    for h in range(RET_HEADS):
        qh = q_heads[h]
        kh = k_heads[h] * (RET_QK_DIM ** -0.5)
        vh = proj_ref[:, OFF_V + h * RET_V_DIM:OFF_V + (h + 1) * RET_V_DIM]
        hs = slice(h * RET_QK_DIM, (h + 1) * RET_QK_DIM)
        scores = _dot_nt(qh.astype(BF16), kh.astype(BF16)) * dmat_ref[h]
        o_intra = _dot(scores.astype(BF16), vh)
        s_in = ret_ref[0, h]
        o_cross = _dot((qh * qdec_ref[:, hs]).astype(BF16), s_in.astype(BF16))
        kv = _dot((kh * kdec_ref[:, hs]).T.astype(BF16), vh)
        ret_ref[0, h] = ret_chunk_decay[h] * s_in + kv
        o_heads.append(o_intra + o_cross)
    gate = proj_ref[:, OFF_G:OFF_G + RET_WIDTH].astype(F32)
    y_ref[:, SSD_WIDTH:SSD_WIDTH + RET_WIDTH] = _ret_out(o_heads, gate, rgain_ref[...]).astype(y_ref.dtype)


def _ret_tables(cl, reps):
    lg = np.log(1.0 - np.exp2(-5.0 - np.arange(RET_HEADS, dtype=np.float64)))
    idx = np.arange(cl, dtype=np.float64)
    rel = idx[:, None] - idx[None, :]
    dmat = np.where(rel[None] >= 0, np.exp(rel[None] * lg[:, None, None]), 0.0).astype(np.float32)
    kdec = np.exp((cl - 1 - idx)[:, None] * lg[None, :]).astype(np.float32)
    qdec = np.exp((idx + 1.0)[:, None] * lg[None, :]).astype(np.float32)
    kdec = np.tile(np.repeat(kdec, RET_QK_DIM, axis=1), (reps, 1))
    qdec = np.tile(np.repeat(qdec, RET_QK_DIM, axis=1), (reps, 1))
    chunk_decay = [float(v) for v in np.exp(cl * lg).astype(np.float32)]
    return dmat, kdec, qdec, chunk_decay


def _rope_tables(pos0, length, reps):
    half = RET_QK_DIM // 2
    inv = ROPE_BASE ** (-np.arange(half, dtype=np.float64) / half)
    pos = (pos0 + np.arange(length)).astype(np.float64)
    ang = pos[:, None] * inv[None, :]
    cos = np.cos(ang).astype(np.float32)
    sin = np.sin(ang).astype(np.float32)
    cos2 = np.tile(np.concatenate([cos, cos], axis=1), (reps, 1))
    sin2 = np.tile(np.concatenate([-sin, sin], axis=1), (reps, 1))
    return cos2, sin2


def _expand_matrix():
    e = np.zeros((LANES, SSD_WIDTH), np.float32)
    for h in range(SSD_HEADS):
        e[h, h * SSD_HEAD_DIM:(h + 1) * SSD_HEAD_DIM] = 1.0
    return e


def _mixer_param_specs(layer):
    return [
        _layer_spec((CONV_WIDTH, CONV_DIM), layer),
        _layer_spec((1, CONV_DIM), layer),
        _layer_spec((1, LANES), layer),
        _layer_spec((1, LANES), layer),
        _const_spec((LANES, SSD_WIDTH)),
        _layer_spec((1, SSD_WIDTH), layer),
        _layer_spec((1, SSD_WIDTH), layer),
        _layer_spec((1, RET_WIDTH), layer),
    ]


def _mixer_params(pw):
    return (pw["conv_w"], pw["conv_b"], pw["dt_bias"], pw["a_log"], pw["expand"], pw["d_skip"],
            pw["ssd_gain"], pw["ret_gain"])


def _state_out_shapes(batch):
    return [
        jax.ShapeDtypeStruct((DEPTH, batch, CONV_WIDTH - 1, CONV_DIM), F32),
        jax.ShapeDtypeStruct((DEPTH, batch, SSD_WIDTH, SSD_STATE), F32),
        jax.ShapeDtypeStruct((DEPTH, batch, RET_HEADS, RET_QK_DIM, RET_V_DIM), F32),
    ]


def _mixer_prompt(layer, proj, dt_raw, pw, prev_states, batch, seq):
    nc = seq // CHUNK
    dmat, kdec, qdec, chunk_decay = _ret_tables(CHUNK, 1)
    cos, sin = _rope_tables(0, seq, 1)
    t = batch * seq
    row_map = lambda b, c: (b * nc + c, 0)
    kern = functools.partial(_mixer_prompt_kernel, ret_chunk_decay=chunk_decay)
    n_in = 15
    return pl.pallas_call(
        kern,
        grid=(batch, nc),
        in_specs=[
            pl.BlockSpec((CHUNK, PROJ_MAIN), row_map),
            pl.BlockSpec((CHUNK, LANES), row_map),
            *_mixer_param_specs(layer),
            pl.BlockSpec((CHUNK, LANES), lambda b, c: (c, 0)),
            pl.BlockSpec((CHUNK, LANES), lambda b, c: (c, 0)),
            _const_spec((RET_HEADS, CHUNK, CHUNK)),
            _const_spec((CHUNK, RET_QK_WIDTH)),
            _const_spec((CHUNK, RET_QK_WIDTH)),
            *[_alias_spec() for _ in prev_states],
        ],
        out_specs=[
            pl.BlockSpec((CHUNK, SSD_WIDTH + RET_WIDTH), row_map),
            pl.BlockSpec((None, 1, CONV_WIDTH - 1, CONV_DIM), lambda b, c: (layer, b, 0, 0)),
            pl.BlockSpec((None, 1, SSD_WIDTH, SSD_STATE), lambda b, c: (layer, b, 0, 0)),
            pl.BlockSpec((None, 1, RET_HEADS, RET_QK_DIM, RET_V_DIM), lambda b, c: (layer, b, 0, 0, 0)),
        ],
        out_shape=[jax.ShapeDtypeStruct((t, SSD_WIDTH + RET_WIDTH), BF16), *_state_out_shapes(batch)],
        input_output_aliases={n_in + k: 1 + k for k in range(len(prev_states))},
        scratch_shapes=[
            pltpu.VMEM((CHUNK + 8, CONV_DIM), F32),
            pltpu.VMEM((SSD_STATE, SSD_WIDTH), F32),
        ],
        compiler_params=_cparams(("parallel", "arbitrary")),
        name="mixer_prompt",
    )(proj, dt_raw, *_mixer_params(pw), jnp.asarray(cos), jnp.asarray(sin), jnp.asarray(dmat),
      jnp.asarray(kdec), jnp.asarray(qdec), *prev_states)


def _pad_rows(x, rows):
    return jnp.concatenate([x, jnp.zeros((rows - x.shape[0], x.shape[1]), x.dtype)], axis=0)


def _mixer_sample_kernel(proj_ref, dt_ref, convs_ref, ssm_in_ref, ret_in_ref, convw_ref, convb_ref, dtb_ref,
                         alog_ref, exp_ref, dskip_ref, sgain_ref, rgain_ref, cos_ref, sin_ref, dmat_ref,
                         kdec_ref, qdec_ref, tri_ref, *rest, bb, cl, ret_chunk_decay):
    y_ref, conv_ref, ssm_ref, ret_ref, xp_scr = rest[-5:]
    for i in range(bb):
        xp_scr[i, 8 - (CONV_WIDTH - 1):8, :] = convs_ref[i]
        xp_scr[i, 8:8 + cl, :] = proj_ref[i * cl:(i + 1) * cl, OFF_XBC:OFF_XBC + CONV_DIM]
        conv_ref[i] = xp_scr[i, 8 + cl - (CONV_WIDTH - 1):8 + cl, :]
    acc = None
    for j in range(CONV_WIDTH):
        s = 8 - (CONV_WIDTH - 1) + j
        tap = jnp.concatenate([xp_scr[i, s:s + cl, :] for i in range(bb)], axis=0) * convw_ref[j:j + 1, :]
        acc = convb_ref[...] + tap if acc is None else acc + tap
    xbc = _silu(acc)
    xs = xbc[:, 0:SSD_WIDTH]
    bm = xbc[:, SSD_WIDTH:SSD_WIDTH + SSD_GROUPS * SSD_STATE]
    cm = xbc[:, SSD_WIDTH + SSD_GROUPS * SSD_STATE:CONV_DIM]

    expand = exp_ref[...]
    dt, acum = _dt_terms(dt_ref[...], dtb_ref[...], alog_ref[...], tri_ref[...], expand)
    eacum = jnp.exp(acum)
    alast = jnp.concatenate(
        [jnp.broadcast_to(acum[(i + 1) * cl - 1:(i + 1) * cl, :], (cl, LANES)) for i in range(bb)], axis=0)
    dt_x = _dot_sel(dt, expand)
    eacum_x = _dot_sel(eacum, expand)
    dte_x = _dot_sel(jnp.exp(alast - acum), expand)
    xdt = xs * dt_x
    xdtd = xdt * dte_x
    row = lax.broadcasted_iota(jnp.int32, (cl, LANES), 0)
    col = lax.broadcasted_iota(jnp.int32, (cl, LANES), 1)
    causal = row >= col
    lane = col

    y_rows = []
    for i in range(bb):
        rs = slice(i * cl, (i + 1) * cl)
        acum_i = acum[rs]
        acum_t = _pad_rows(acum_i, LANES).T
        xdt_p = _pad_rows(xdt[rs], LANES).astype(BF16)
        xdtd_p = _pad_rows(xdtd[rs], LANES)
        chunk_decay = eacum[(i + 1) * cl - 1:(i + 1) * cl, :]
        y_parts = []
        for g in range(SSD_GROUPS):
            bg_p = _pad_rows(bm[rs, g * SSD_STATE:(g + 1) * SSD_STATE], LANES)
            cg_b = cm[rs, g * SSD_STATE:(g + 1) * SSD_STATE].astype(BF16)
            cb = _dot_nt(cg_b, bg_p.astype(BF16))
            gsl = slice(g * GROUP_WIDTH, (g + 1) * GROUP_WIDTH)
            st_g = ssm_in_ref[i, gsl, :]
            y_off = _dot_nt(cg_b, st_g.astype(BF16)) * eacum_x[rs, gsl]
            upd = _dot(xdtd_p[:, gsl].T.astype(BF16), bg_p.astype(BF16))
            for hh in range(SSD_HEADS // SSD_GROUPS):
                h = g * (SSD_HEADS // SSD_GROUPS) + hh
                hsl = slice(hh * SSD_HEAD_DIM, (hh + 1) * SSD_HEAD_DIM)
                ssm_ref[i, h * SSD_HEAD_DIM:(h + 1) * SSD_HEAD_DIM, :] = (
                    st_g[hsl, :] * chunk_decay[:, h:h + 1] + upd[hsl, :])
            for j in range(GROUP_WIDTH // LANES):
                h0 = g * (SSD_HEADS // SSD_GROUPS) + 2 * j
                psl = slice(h0 * SSD_HEAD_DIM, (h0 + 2) * SSD_HEAD_DIM)
                res = []
                for hh in (h0, h0 + 1):
                    seg = acum_i[:, hh:hh + 1] - acum_t[hh:hh + 1, :]
                    lmat = jnp.exp(jnp.where(causal, seg, -jnp.inf))
                    res.append(_dot((cb * lmat).astype(BF16), xdt_p[:, psl]))
                y_parts.append(jnp.where(lane < SSD_HEAD_DIM, res[0], res[1]) + y_off[:, j * LANES:(j + 1) * LANES])
        y_rows.append(jnp.concatenate(y_parts, axis=-1))
    y = jnp.concatenate(y_rows, axis=0)
    z = proj_ref[:, OFF_Z:OFF_Z + SSD_WIDTH]
    y_ref[:, 0:SSD_WIDTH] = _ssd_out(y, xs, z, dskip_ref[...], sgain_ref[...]).astype(y_ref.dtype)

    cos = cos_ref[...]
    sin = sin_ref[...]
    q_heads = _rotary(proj_ref[:, OFF_Q:OFF_Q + RET_QK_WIDTH], cos, sin)
    k_heads = _rotary(proj_ref[:, OFF_K:OFF_K + RET_QK_WIDTH], cos, sin)
    o_heads = []
    for h in range(RET_HEADS):
        hs = slice(h * RET_QK_DIM, (h + 1) * RET_QK_DIM)
        kh_all = k_heads[h] * (RET_QK_DIM ** -0.5)
        q_start = q_heads[h] * qdec_ref[:, hs]
        k_end = kh_all * kdec_ref[:, hs]
        o_rows = []
        for i in range(bb):
            rs = slice(i * cl, (i + 1) * cl)
            vh_p = _pad_rows(proj_ref[rs, OFF_V + h * RET_V_DIM:OFF_V + (h + 1) * RET_V_DIM], LANES).astype(BF16)
            kh_p = _pad_rows(kh_all[rs], LANES).astype(BF16)
            scores = _dot_nt(q_heads[h][rs].astype(BF16), kh_p) * dmat_ref[h]
            o_intra = _dot(scores.astype(BF16), vh_p)
            s_in = ret_in_ref[i, h]
            o_cross = _dot(q_start[rs].astype(BF16), s_in.astype(BF16))
            kv = _dot(_pad_rows(k_end[rs], LANES).T.astype(BF16), vh_p)
            ret_ref[i, h] = ret_chunk_decay[h] * s_in + kv
            o_rows.append(o_intra + o_cross)
        o_heads.append(jnp.concatenate(o_rows, axis=0))
    gate = proj_ref[:, OFF_G:OFF_G + RET_WIDTH]
    y_ref[:, SSD_WIDTH:SSD_WIDTH + RET_WIDTH] = _ret_out(o_heads, gate, rgain_ref[...]).astype(y_ref.dtype)


def _mixer_sample(layer, proj, dt_raw, conv_state, ssm_state, ret_state, pw, prev_states, batch, cl, bb):
    dmat, kdec, qdec, chunk_decay = _ret_tables(cl, bb)
    dmat = np.concatenate([dmat, np.zeros((RET_HEADS, cl, LANES - cl), np.float32)], axis=-1)
    cos, sin = _rope_tables(PAST_LEN, cl, bb)
    m = bb * cl
    tri = np.kron(np.eye(bb, dtype=np.float32), np.tril(np.ones((cl, cl), np.float32)))
    kern = functools.partial(_mixer_sample_kernel, bb=bb, cl=cl, ret_chunk_decay=chunk_decay)
    row_map = lambda i: (i, 0)
    state_specs = [
        pl.BlockSpec((None, bb, CONV_WIDTH - 1, CONV_DIM), lambda i: (layer, i, 0, 0)),
        pl.BlockSpec((None, bb, SSD_WIDTH, SSD_STATE), lambda i: (layer, i, 0, 0)),
        pl.BlockSpec((None, bb, RET_HEADS, RET_QK_DIM, RET_V_DIM), lambda i: (layer, i, 0, 0, 0)),
    ]
    n_in = 19
    return pl.pallas_call(
        kern,
        grid=(batch // bb,),
        in_specs=[
            pl.BlockSpec((m, PROJ_MAIN), row_map),
            pl.BlockSpec((m, LANES), row_map),
            *state_specs,
            *_mixer_param_specs(layer),
            _const_spec((m, LANES)),
            _const_spec((m, LANES)),
            _const_spec((RET_HEADS, cl, LANES)),
            _const_spec((m, RET_QK_WIDTH)),
            _const_spec((m, RET_QK_WIDTH)),
            _const_spec((m, m)),
            *[_alias_spec() for _ in prev_states],
        ],
        out_specs=[pl.BlockSpec((m, SSD_WIDTH + RET_WIDTH), row_map), *state_specs],
        out_shape=[jax.ShapeDtypeStruct((batch * cl, SSD_WIDTH + RET_WIDTH), BF16), *_state_out_shapes(batch)],
        input_output_aliases={n_in + k: 1 + k for k in range(len(prev_states))},
        scratch_shapes=[pltpu.VMEM((bb, 8 + cl, CONV_DIM), F32)],
        compiler_params=_cparams(("parallel",)),
        name="mixer_sample",
    )(proj, dt_raw, conv_state, ssm_state, ret_state, *_mixer_params(pw), jnp.asarray(cos), jnp.asarray(sin),
      jnp.asarray(dmat), jnp.asarray(kdec), jnp.asarray(qdec), jnp.asarray(tri, dtype=BF16), *prev_states)


def _outproj_kernel(x_ref, y_ref, w_ref, o_ref):
    o_ref[...] = x_ref[...] + _dot(y_ref[...], w_ref[...])


def _outproj(x, y, layer, pw, tm):
    t = x.shape[0]
    k = y.shape[1]
    return pl.pallas_call(
        _outproj_kernel,
        grid=(t // tm,),
        in_specs=[
            pl.BlockSpec((tm, D_MODEL), lambda i: (i, 0)),
            pl.BlockSpec((tm, k), lambda i: (i, 0)),
            _layer_spec((k, D_MODEL), layer),
        ],
        out_specs=pl.BlockSpec((tm, D_MODEL), lambda i: (i, 0)),
        out_shape=jax.ShapeDtypeStruct((t, D_MODEL), F32),
        compiler_params=_cparams(("parallel",)),
        name="outproj",
    )(x, y, pw["w_out"])


def _memproj_kernel(m_ref, w_ref, k_ref, v_ref):
    r = _dot(m_ref[...].astype(BF16), w_ref[...])
    k_ref[...] = r[:, 0:D_MODEL]
    v_ref[...] = r[:, D_MODEL:2 * D_MODEL]


def _memproj(mem, pw, tm):
    t = mem.shape[0]
    out_spec = pl.BlockSpec((None, tm, D_MODEL), lambda l, i: (l, i, 0))
    return pl.pallas_call(
        _memproj_kernel,
        grid=(DEPTH, t // tm),
        in_specs=[
            pl.BlockSpec((tm, D_MODEL), lambda l, i: (i, 0)),
            pl.BlockSpec((None, D_MODEL, 2 * D_MODEL), lambda l, i: (l, 0, 0)),
        ],
        out_specs=[out_spec, out_spec],
        out_shape=[jax.ShapeDtypeStruct((DEPTH, t, D_MODEL), F32)] * 2,
        compiler_params=_cparams(("parallel", "parallel")),
        name="memproj",
    )(mem, pw["w_kv"])


def _xattn_kernel(x_ref, g_ref, wq_ref, wo_ref, k_ref, v_ref, o_ref, *, bb, tm):
    x = x_ref[...].reshape(bb * tm, D_MODEL)
    h = (_rms(x) * g_ref[...]).astype(BF16)
    q = _dot(h, wq_ref[...])
    scale = XA_HEAD_DIM ** -0.5
    rows = []
    for i in range(bb):
        heads = []
        for hd in range(XA_HEADS):
            hs = slice(hd * XA_HEAD_DIM, (hd + 1) * XA_HEAD_DIM)
            qh = q[i * tm:(i + 1) * tm, hs].astype(BF16)
            s = _dot_nt(qh, k_ref[i, :, hs].astype(BF16)) * scale
            e = jnp.exp(s - jnp.max(s, axis=-1, keepdims=True))
            p = e / jnp.sum(e, axis=-1, keepdims=True)
            heads.append(_dot(p.astype(BF16), v_ref[i, :, hs].astype(BF16)))
        rows.append(jnp.concatenate(heads, axis=-1))
    o = jnp.concatenate(rows, axis=0).astype(BF16)
    o_ref[...] = (x + _dot(o, wo_ref[...])).reshape(bb, tm, D_MODEL)


def _xattn(x, layer, pw, mem_k, mem_v, bb, tm):
    b, l, _ = x.shape
    kern = functools.partial(_xattn_kernel, bb=bb, tm=tm)
    mem_spec = pl.BlockSpec((None, bb, N_MEM, D_MODEL), lambda i, j: (layer, i, 0, 0))
    return pl.pallas_call(
        kern,
        grid=(b // bb, l // tm),
        in_specs=[
            pl.BlockSpec((bb, tm, D_MODEL), lambda i, j: (i, j, 0)),
            _layer_spec((1, D_MODEL), layer),
            _layer_spec((D_MODEL, D_MODEL), layer),
            _layer_spec((D_MODEL, D_MODEL), layer),
            mem_spec,
            mem_spec,
        ],
        out_specs=pl.BlockSpec((bb, tm, D_MODEL), lambda i, j: (i, j, 0)),
        out_shape=jax.ShapeDtypeStruct((b, l, D_MODEL), F32),
        compiler_params=_cparams(("parallel", "parallel")),
        name="xattn",
    )(x, pw["norm_mem"], pw["w_mq"], pw["w_mo"], mem_k, mem_v)


def _router_gates(logits):
    m = logits.shape[0]
    lane_i = lax.broadcasted_iota(jnp.int32, (m, LANES), 1)
    lane = lane_i.astype(F32)
    big = float(LANES)
    is_g = lane_i < N_EGROUPS
    gl = jnp.where(is_g, logits, -jnp.inf)
    gmax = jnp.max(gl, axis=-1, keepdims=True)
    g_idx = jnp.min(jnp.where(is_g & (gl == gmax), lane, big), axis=-1, keepdims=True)
    g_prob = 1.0 / jnp.sum(jnp.exp(gl - gmax), axis=-1, keepdims=True)
    e_lane = lane_i - ROUTER_OFF
    e_group = lax.shift_right_arithmetic(e_lane, int(math.log2(EXPERTS_PER_GROUP))).astype(F32)
    sel = (e_lane >= 0) & (e_lane < N_EXPERTS) & (e_group == g_idx)
    el = jnp.where(sel, logits, -jnp.inf)
    emax = jnp.max(el, axis=-1, keepdims=True)
    ee = jnp.exp(el - emax)
    e_prob = ee / jnp.sum(ee, axis=-1, keepdims=True)
    p1 = jnp.max(jnp.where(sel, e_prob, -1.0), axis=-1, keepdims=True)
    i1 = jnp.min(jnp.where(sel & (e_prob == p1), lane, big), axis=-1, keepdims=True)
    sel2 = sel & (lane != i1)
    p2 = jnp.max(jnp.where(sel2, e_prob, -1.0), axis=-1, keepdims=True)
    i2 = jnp.min(jnp.where(sel2 & (e_prob == p2), lane, big), axis=-1, keepdims=True)
    denom = p1 + p2
    w1 = g_prob * p1 / denom
    w2 = g_prob * p2 / denom
    return jnp.where(lane == i1, w1, 0.0) + jnp.where(lane == i2, w2, 0.0)


def _moe_kernel(x_ref, g_ref, wr_hi_ref, wr_lo_ref, br_ref, wg_ref, wu_ref, wd_ref, gf_ref, o_ref,
                h_scr, gate_scr, *, final_norm):
    e = pl.program_id(1)

    @pl.when(e == 0)
    def _():
        h = _rms(x_ref[...]) * g_ref[...]
        h_hi = h.astype(BF16)
        h_lo = (h - h_hi.astype(F32)).astype(BF16)
        logits = (_dot(h_hi, wr_hi_ref[...]) + _dot(h_hi, wr_lo_ref[...]) + _dot(h_lo, wr_hi_ref[...])
                  + br_ref[...])
        h_scr[...] = h_hi
        gate_scr[...] = _router_gates(logits)
        o_ref[...] = jnp.zeros_like(o_ref)

    hb = h_scr[...]
    lane = lax.broadcasted_iota(jnp.int32, gate_scr.shape, 1)
    gate_e = jnp.sum(jnp.where(lane == e + ROUTER_OFF, gate_scr[...], 0.0), axis=-1, keepdims=True)
    a = _silu(_dot(hb, wg_ref[...])) * _dot(hb, wu_ref[...]) * gate_e
    o_ref[...] += _dot(a.astype(BF16), wd_ref[...])

    @pl.when(e == pl.num_programs(1) - 1)
    def _():
        y = x_ref[...] + o_ref[...]
        if final_norm:
            y = _rms(y) * gf_ref[...]
        o_ref[...] = y


def _moe(x, layer, pw, final_norm, tm):
    t = x.shape[0]
    kern = functools.partial(_moe_kernel, final_norm=final_norm)
    return pl.pallas_call(
        kern,
        grid=(t // tm, N_EXPERTS),
        in_specs=[
            pl.BlockSpec((tm, D_MODEL), lambda i, e: (i, 0)),
            _layer_spec((1, D_MODEL), layer),
            _layer_spec((D_MODEL, LANES), layer),
            _layer_spec((D_MODEL, LANES), layer),
            _layer_spec((1, LANES), layer),
            pl.BlockSpec((None, None, D_MODEL, EXPERT_FF), lambda i, e: (layer, e, 0, 0)),
            pl.BlockSpec((None, None, D_MODEL, EXPERT_FF), lambda i, e: (layer, e, 0, 0)),
            pl.BlockSpec((None, None, EXPERT_FF, D_MODEL), lambda i, e: (layer, e, 0, 0)),
            _const_spec((1, D_MODEL)),
        ],
        out_specs=pl.BlockSpec((tm, D_MODEL), lambda i, e: (i, 0)),
        out_shape=jax.ShapeDtypeStruct((t, D_MODEL), F32),
        scratch_shapes=[pltpu.VMEM((tm, D_MODEL), BF16), pltpu.VMEM((tm, LANES), F32)],
        compiler_params=_cparams(("parallel", "arbitrary")),
        name="moe",
    )(x, pw["norm_ffn"], pw["wr_hi"], pw["wr_lo"], pw["b_router"], pw["w_gate"], pw["w_up"], pw["w_down"],
      pw["norm_final"])


def _row(v):
    return v.reshape(v.shape[0], 1, v.shape[1])


def _pad_lanes(v):
    return jnp.pad(v, ((0, 0),) * (v.ndim - 1) + ((0, LANES - v.shape[-1]),))


def _prep_weights(norm_mix, w_in, conv_w, conv_b, dt_bias, a_log, d_skip, ssd_gain, ret_gain, w_out,
                  norm_mem, w_mq, w_mk, w_mv, w_mo, norm_ffn, w_rg, b_rg, w_re, b_re, w_gate, w_up, w_down,
                  norm_final):
    dt_off = SSD_WIDTH + CONV_DIM
    w_main = jnp.concatenate([w_in[:, :, :dt_off], w_in[:, :, dt_off + SSD_HEADS:]], axis=2).astype(BF16)
    w_dt = _pad_lanes(w_in[:, :, dt_off:dt_off + SSD_HEADS]).astype(BF16)
    w_router = _pad_lanes(jnp.concatenate([w_rg, w_re], axis=2))
    wr_hi = w_router.astype(BF16)
    wr_lo = (w_router - wr_hi.astype(F32)).astype(BF16)
    return dict(
        norm_mix=_row(norm_mix), w_main=w_main, w_dt=w_dt,
        conv_w=conv_w, conv_b=_row(conv_b),
        dt_bias=_row(_pad_lanes(dt_bias)), a_log=_row(_pad_lanes(a_log)),
        expand=jnp.asarray(_expand_matrix(), dtype=BF16),
        d_skip=_row(jnp.repeat(d_skip, SSD_HEAD_DIM, axis=1)),
        ssd_gain=_row(ssd_gain), ret_gain=_row(ret_gain),
        w_out=w_out.astype(BF16),
        norm_mem=_row(norm_mem), w_mq=w_mq.astype(BF16), w_mo=w_mo.astype(BF16),
        w_kv=jnp.concatenate([w_mk, w_mv], axis=2).astype(BF16),
        norm_ffn=_row(norm_ffn), wr_hi=wr_hi, wr_lo=wr_lo,
        b_router=_row(_pad_lanes(jnp.concatenate([b_rg, b_re], axis=1))),
        w_gate=w_gate.astype(BF16), w_up=w_up.astype(BF16), w_down=w_down.astype(BF16),
        norm_final=norm_final.reshape(1, -1),
    )


def _token_tile(t, cap):
    tm = min(t, cap)
    assert t % tm == 0
    return tm


def _trunk(x, mixer_fn, mem_k, mem_v, pw, proj_dtype, xattn_tiles):
    b, l, _ = x.shape
    t = b * l
    xf = x.reshape(t, D_MODEL)
    states = ()
    for layer in range(DEPTH):
        proj, dt_raw = _inproj(xf, layer, pw, proj_dtype, _token_tile(t, 512))
        y, *states = mixer_fn(layer, proj, dt_raw, tuple(states))
        xf = _outproj(xf, y, layer, pw, _token_tile(t, 512))
        xf = _xattn(xf.reshape(b, l, D_MODEL), layer, pw, mem_k, mem_v, *xattn_tiles).reshape(t, D_MODEL)
        xf = _moe(xf, layer, pw, layer == DEPTH - 1, _token_tile(t, 1024))
    conv, ssm, ret = states
    return (xf.reshape(b, l, D_MODEL), ssm.reshape(DEPTH, b, SSD_HEADS, SSD_HEAD_DIM, SSD_STATE), conv, ret)


def kernel(x_prompt, x_sample, mem_prompt, state_ssm, state_conv, state_ret, cache_mem_k, cache_mem_v,
           norm_mix, w_in, conv_w, conv_b, dt_bias, a_log, d_skip, ssd_gain, ret_gain, w_out,
           norm_mem, w_mq, w_mk, w_mv, w_mo, norm_ffn, w_rg, b_rg, w_re, b_re, w_gate, w_up, w_down,
           norm_final):
    pw = _prep_weights(norm_mix, w_in, conv_w, conv_b, dt_bias, a_log, d_skip, ssd_gain, ret_gain, w_out,
                       norm_mem, w_mq, w_mk, w_mv, w_mo, norm_ffn, w_rg, b_rg, w_re, b_re, w_gate, w_up, w_down,
                       norm_final)
    bp, lp, _ = x_prompt.shape
    bs, ls, _ = x_sample.shape
    n_mem = mem_prompt.shape[1]

    mem_k_p, mem_v_p = _memproj(mem_prompt.reshape(bp * n_mem, D_MODEL), pw, _token_tile(bp * n_mem, 512))
    mem_k_p = mem_k_p.reshape(DEPTH, bp, n_mem, D_MODEL)
    mem_v_p = mem_v_p.reshape(DEPTH, bp, n_mem, D_MODEL)

    def mixer_p(layer, proj, dt_raw, prev_states):
        return _mixer_prompt(layer, proj, dt_raw, pw, prev_states, bp, lp)

    y_prompt, ssm_p, conv_p, ret_p = _trunk(x_prompt, mixer_p, mem_k_p, mem_v_p, pw, BF16,
                                            (1, _token_tile(lp, 512)))

    sample_bb = 4
    ssm_in = state_ssm.reshape(DEPTH, bs, SSD_WIDTH, SSD_STATE)

    def mixer_s(layer, proj, dt_raw, prev_states):
        return _mixer_sample(layer, proj, dt_raw, state_conv, ssm_in, state_ret, pw, prev_states, bs, ls,
                             sample_bb)

    y_sample, ssm_s, conv_s, ret_s = _trunk(x_sample, mixer_s, cache_mem_k.reshape(DEPTH, bs, n_mem, D_MODEL),
                                            cache_mem_v.reshape(DEPTH, bs, n_mem, D_MODEL), pw, F32,
                                            (sample_bb, ls))

    shp = (DEPTH, bp, n_mem, XA_HEADS, XA_HEAD_DIM)
    return (y_prompt, y_sample, ssm_p, conv_p, ret_p, mem_k_p.reshape(shp), mem_v_p.reshape(shp),
            ssm_s, conv_s, ret_s)
```

```python
import functools
import math

import numpy as np
import jax
import jax.numpy as jnp
from jax import lax
from jax.experimental import pallas as pl
from jax.experimental.pallas import tpu as pltpu

F32 = jnp.float32
BF16 = jnp.bfloat16

D_MODEL = 1024
DEPTH = 2
PAST_LEN = 16384
SSD_HEAD_DIM = 64
SSD_HEADS = 16
SSD_GROUPS = 2
SSD_STATE = 128
SSD_WIDTH = 1024
GROUP_WIDTH = SSD_WIDTH // SSD_GROUPS
CONV_WIDTH = 4
CONV_DIM = SSD_WIDTH + 2 * SSD_GROUPS * SSD_STATE
RET_HEADS = 4
RET_V_DIM = 256
RET_QK_DIM = 128
RET_WIDTH = 1024
RET_QK_WIDTH = RET_HEADS * RET_QK_DIM
ROPE_BASE = 10000.0
N_MEM = 256
XA_HEADS = 4
XA_HEAD_DIM = 256
N_EGROUPS = 4
EXPERTS_PER_GROUP = 4
N_EXPERTS = 16
EXPERT_FF = 512
RMS_EPS = 1e-6

LANES = 128
CHUNK = 128
OFF_Z = 0
OFF_XBC = OFF_Z + SSD_WIDTH
OFF_Q = OFF_XBC + CONV_DIM
OFF_K = OFF_Q + RET_QK_WIDTH
OFF_V = OFF_K + RET_QK_WIDTH
OFF_G = OFF_V + RET_WIDTH
PROJ_MAIN = OFF_G + RET_WIDTH
INPROJ_TN = 512
ROUTER_OFF = N_EGROUPS

VMEM_LIMIT = 56 * 1024 * 1024


def _cparams(sem):
    return pltpu.CompilerParams(dimension_semantics=sem, vmem_limit_bytes=VMEM_LIMIT)


def _const_spec(shape):
    nd = len(shape)
    return pl.BlockSpec(shape, lambda *_: (0,) * nd, pipeline_mode=pl.Buffered(1))


def _layer_spec(shape, layer):
    nd = len(shape)
    return pl.BlockSpec((None,) + tuple(shape), lambda *_: (layer,) + (0,) * nd, pipeline_mode=pl.Buffered(1))


def _alias_spec():
    return pl.BlockSpec(memory_space=pl.ANY)


def _rms(x):
    return x * lax.rsqrt(jnp.mean(x * x, axis=-1, keepdims=True) + RMS_EPS)


def _silu(x):
    return x * jax.nn.sigmoid(x)


def _softplus(x):
    return jnp.maximum(x, 0.0) + jnp.log1p(jnp.exp(-jnp.abs(x)))


def _split3(x):
    hi = x.astype(BF16)
    r = x - hi.astype(F32)
    mid = r.astype(BF16)
    lo = (r - mid.astype(F32)).astype(BF16)
    return hi, mid, lo


def _dot(a, b):
    return jnp.dot(a, b, preferred_element_type=F32)


def _dot_nt(a, b):
    return lax.dot_general(a, b, (((1,), (1,)), ((), ())), preferred_element_type=F32)


def _dot_sel(x, sel):
    hi, mid, lo = _split3(x)
    return _dot(hi, sel) + _dot(mid, sel) + _dot(lo, sel)


def _sel_dot(sel, x):
    hi, mid, lo = _split3(x)
    return _dot(sel, hi) + _dot(sel, mid) + _dot(sel, lo)


def _inproj_kernel(x_ref, g_ref, w_ref, wdt_ref, o_ref, odt_ref):
    h = (_rms(x_ref[...]) * g_ref[...]).astype(BF16)
    for j in range(PROJ_MAIN // INPROJ_TN):
        sl = slice(j * INPROJ_TN, (j + 1) * INPROJ_TN)
        o_ref[:, sl] = _dot(h, w_ref[:, sl]).astype(o_ref.dtype)
    odt_ref[...] = _dot(h, wdt_ref[...])


def _inproj(x, layer, pw, out_dtype, tm):
    t = x.shape[0]
    return pl.pallas_call(
        _inproj_kernel,
        grid=(t // tm,),
        in_specs=[
            pl.BlockSpec((tm, D_MODEL), lambda i: (i, 0)),
            _layer_spec((1, D_MODEL), layer),
            _layer_spec((D_MODEL, PROJ_MAIN), layer),
            _layer_spec((D_MODEL, LANES), layer),
        ],
        out_specs=[
            pl.BlockSpec((tm, PROJ_MAIN), lambda i: (i, 0)),
            pl.BlockSpec((tm, LANES), lambda i: (i, 0)),
        ],
        out_shape=[
            jax.ShapeDtypeStruct((t, PROJ_MAIN), out_dtype),
            jax.ShapeDtypeStruct((t, LANES), F32),
        ],
        compiler_params=_cparams(("parallel",)),
        name="inproj",
    )(x, pw["norm_mix"], pw["w_main"], pw["w_dt"])


def _dt_terms(dt_raw, dtb, alog, tri, expand):
    dt = _softplus(dt_raw + dtb)
    a = dt * (-jnp.exp(alog))
    acum = _sel_dot(tri, a)
    return dt, acum


def _ssd_out(y, xs, z, dskip, gain):
    y = (y + dskip * xs) * _silu(z)
    parts = []
    for g in range(SSD_GROUPS):
        parts.append(_rms(y[:, g * GROUP_WIDTH:(g + 1) * GROUP_WIDTH]))
    return jnp.concatenate(parts, axis=-1) * gain


def _rotary(x, cos, sin_signed):
    parts = []
    for h in range(RET_HEADS):
        xh = x[:, h * RET_QK_DIM:(h + 1) * RET_QK_DIM]
        parts.append(xh * cos + pltpu.roll(xh, RET_QK_DIM // 2, axis=1) * sin_signed)
    return parts


def _ret_out(o_heads, g, gain):
    o = jnp.concatenate([_rms(o) for o in o_heads], axis=-1)
    return o * gain * _silu(g)


def _mixer_prompt_kernel(proj_ref, dt_ref, convw_ref, convb_ref, dtb_ref, alog_ref, exp_ref, dskip_ref,
                         sgain_ref, rgain_ref, cos_ref, sin_ref, dmat_ref, kdec_ref, qdec_ref,
                         *rest, ret_chunk_decay):
    y_ref, conv_ref, ssm_ref, ret_ref, xp_scr, st_scr = rest[-6:]
    cl = CHUNK
    c = pl.program_id(1)
    nc = pl.num_programs(1)

    @pl.when(c == 0)
    def _():
        xp_scr[0:8, :] = jnp.zeros((8, CONV_DIM), F32)
        st_scr[...] = jnp.zeros_like(st_scr)
        ret_ref[...] = jnp.zeros_like(ret_ref)

    xp_scr[8:8 + cl, :] = proj_ref[:, OFF_XBC:OFF_XBC + CONV_DIM].astype(F32)
    acc = convb_ref[...] + xp_scr[8 - (CONV_WIDTH - 1):8 - (CONV_WIDTH - 1) + cl, :] * convw_ref[0:1, :]
    for j in range(1, CONV_WIDTH):
        s = 8 - (CONV_WIDTH - 1) + j
        acc = acc + xp_scr[s:s + cl, :] * convw_ref[j:j + 1, :]
    conv_ref[0] = xp_scr[8 + cl - (CONV_WIDTH - 1):8 + cl, :]
    xp_scr[0:8, :] = xp_scr[cl:cl + 8, :]
    xbc = _silu(acc)
    xs = xbc[:, 0:SSD_WIDTH]
    bm = xbc[:, SSD_WIDTH:SSD_WIDTH + SSD_GROUPS * SSD_STATE]
    cm = xbc[:, SSD_WIDTH + SSD_GROUPS * SSD_STATE:CONV_DIM]

    row = lax.broadcasted_iota(jnp.int32, (cl, cl), 0)
    col = lax.broadcasted_iota(jnp.int32, (cl, cl), 1)
    causal = row >= col
    tri = jnp.where(causal, 1.0, 0.0).astype(BF16)
    expand = exp_ref[...]
    dt, acum = _dt_terms(dt_ref[...], dtb_ref[...], alog_ref[...], tri, expand)
    acum_t = acum.T
    eacum = jnp.exp(acum)
    dt_x = _dot_sel(dt, expand)
    eacum_x = _dot_sel(eacum, expand)
    dte_x = _dot_sel(jnp.exp(acum[cl - 1:cl, :] - acum), expand)
    xdt = xs * dt_x
    xdt_b = xdt.astype(BF16)
    xdtd_b = (xdt * dte_x).astype(BF16)
    lane = lax.broadcasted_iota(jnp.int32, (cl, LANES), 1)

    y_parts = []
    upd_parts = []
    for g in range(SSD_GROUPS):
        bg = bm[:, g * SSD_STATE:(g + 1) * SSD_STATE]
        cg_b = cm[:, g * SSD_STATE:(g + 1) * SSD_STATE].astype(BF16)
        bg_b = bg.astype(BF16)
        cb = _dot_nt(cg_b, bg_b)
        gsl = slice(g * GROUP_WIDTH, (g + 1) * GROUP_WIDTH)
        y_off = _dot(cg_b, st_scr[:, gsl].astype(BF16)) * eacum_x[:, gsl]
        upd_parts.append(_dot(bg.T.astype(BF16), xdtd_b[:, gsl]))
        for j in range(GROUP_WIDTH // LANES):
            h0 = g * (SSD_HEADS // SSD_GROUPS) + 2 * j
            psl = slice(h0 * SSD_HEAD_DIM, (h0 + 2) * SSD_HEAD_DIM)
            res = []
            for hh in (h0, h0 + 1):
                seg = acum[:, hh:hh + 1] - acum_t[hh:hh + 1, :]
                lmat = jnp.exp(jnp.where(causal, seg, -jnp.inf))
                res.append(_dot((cb * lmat).astype(BF16), xdt_b[:, psl]))
            y_parts.append(jnp.where(lane < SSD_HEAD_DIM, res[0], res[1]) + y_off[:, j * LANES:(j + 1) * LANES])
    y = jnp.concatenate(y_parts, axis=-1)
    st_new = st_scr[...] * eacum_x[cl - 1:cl, :] + jnp.concatenate(upd_parts, axis=-1)
    st_scr[...] = st_new

    @pl.when(c == nc - 1)
    def _():
        ssm_ref[0] = st_new.T

    z = proj_ref[:, OFF_Z:OFF_Z + SSD_WIDTH].astype(F32)
    y_ref[:, 0:SSD_WIDTH] = _ssd_out(y, xs, z, dskip_ref[...], sgain_ref[...]).astype(y_ref.dtype)

    cos = cos_ref[...]
    sin = sin_ref[...]
    q_heads = _rotary(proj_ref[:, OFF_Q:OFF_Q + RET_QK_WIDTH].astype(F32), cos, sin)
    k_heads = _rotary(proj_ref[:, OFF_K:OFF_K + RET_QK_WIDTH].astype(F32), cos, sin)
    o_heads = []
    for h in range(RET_HEADS):
        qh = q_heads[h]
        kh = k_heads[h] * (RET_QK_DIM ** -0.5)
        vh = proj_ref[:, OFF_V + h * RET_V_DIM:OFF_V + (h + 1) * RET_V_DIM]
        hs = slice(h * RET_QK_DIM, (h + 1) * RET_QK_DIM)
        scores = _dot_nt(qh.astype(BF16), kh.astype(BF16)) * dmat_ref[h]
        o_intra = _dot(scores.astype(BF16), vh)
        s_in = ret_ref[0, h]
        o_cross = _dot((qh * qdec_ref[:, hs]).astype(BF16), s_in.astype(BF16))
        kv = _dot((kh * kdec_ref[:, hs]).T.astype(BF16), vh)
        ret_ref[0, h] = ret_chunk_decay[h] * s_in + kv
        o_heads.append(o_intra + o_cross)
    gate = proj_ref[:, OFF_G:OFF_G + RET_WIDTH].astype(F32)
    y_ref[:, SSD_WIDTH:SSD_WIDTH + RET_WIDTH] = _ret_out(o_heads, gate, rgain_ref[...]).astype(y_ref.dtype)


def _ret_tables(cl, reps):
    lg = np.log(1.0 - np.exp2(-5.0 - np.arange(RET_HEADS, dtype=np.float64)))
    idx = np.arange(cl, dtype=np.float64)
    rel = idx[:, None] - idx[None, :]
    dmat = np.where(rel[None] >= 0, np.exp(rel[None] * lg[:, None, None]), 0.0).astype(np.float32)
    kdec = np.exp((cl - 1 - idx)[:, None] * lg[None, :]).astype(np.float32)
    qdec = np.exp((idx + 1.0)[:, None] * lg[None, :]).astype(np.float32)
    kdec = np.tile(np.repeat(kdec, RET_QK_DIM, axis=1), (reps, 1))
    qdec = np.tile(np.repeat(qdec, RET_QK_DIM, axis=1), (reps, 1))
    chunk_decay = [float(v) for v in np.exp(cl * lg).astype(np.float32)]
    return dmat, kdec, qdec, chunk_decay


def _rope_tables(pos0, length, reps):
    half = RET_QK_DIM // 2
    inv = ROPE_BASE ** (-np.arange(half, dtype=np.float64) / half)
    pos = (pos0 + np.arange(length)).astype(np.float64)
    ang = pos[:, None] * inv[None, :]
    cos = np.cos(ang).astype(np.float32)
    sin = np.sin(ang).astype(np.float32)
    cos2 = np.tile(np.concatenate([cos, cos], axis=1), (reps, 1))
    sin2 = np.tile(np.concatenate([-sin, sin], axis=1), (reps, 1))
    return cos2, sin2


def _expand_matrix():
    e = np.zeros((LANES, SSD_WIDTH), np.float32)
    for h in range(SSD_HEADS):
        e[h, h * SSD_HEAD_DIM:(h + 1) * SSD_HEAD_DIM] = 1.0
    return e


def _mixer_param_specs(layer):
    return [
        _layer_spec((CONV_WIDTH, CONV_DIM), layer),
        _layer_spec((1, CONV_DIM), layer),
        _layer_spec((1, LANES), layer),
        _layer_spec((1, LANES), layer),
        _const_spec((LANES, SSD_WIDTH)),
        _layer_spec((1, SSD_WIDTH), layer),
        _layer_spec((1, SSD_WIDTH), layer),
        _layer_spec((1, RET_WIDTH), layer),
    ]


def _mixer_params(pw):
    return (pw["conv_w"], pw["conv_b"], pw["dt_bias"], pw["a_log"], pw["expand"], pw["d_skip"],
            pw["ssd_gain"], pw["ret_gain"])


def _state_out_shapes(batch):
    return [
        jax.ShapeDtypeStruct((DEPTH, batch, CONV_WIDTH - 1, CONV_DIM), F32),
        jax.ShapeDtypeStruct((DEPTH, batch, SSD_WIDTH, SSD_STATE), F32),
        jax.ShapeDtypeStruct((DEPTH, batch, RET_HEADS, RET_QK_DIM, RET_V_DIM), F32),
    ]


def _mixer_prompt(layer, proj, dt_raw, pw, prev_states, batch, seq):
    nc = seq // CHUNK
    dmat, kdec, qdec, chunk_decay = _ret_tables(CHUNK, 1)
    cos, sin = _rope_tables(0, seq, 1)
    t = batch * seq
    row_map = lambda b, c: (b * nc + c, 0)
    kern = functools.partial(_mixer_prompt_kernel, ret_chunk_decay=chunk_decay)
    n_in = 15
    return pl.pallas_call(
        kern,
        grid=(batch, nc),
        in_specs=[
            pl.BlockSpec((CHUNK, PROJ_MAIN), row_map),
            pl.BlockSpec((CHUNK, LANES), row_map),
            *_mixer_param_specs(layer),
            pl.BlockSpec((CHUNK, LANES), lambda b, c: (c, 0)),
            pl.BlockSpec((CHUNK, LANES), lambda b, c: (c, 0)),
            _const_spec((RET_HEADS, CHUNK, CHUNK)),
            _const_spec((CHUNK, RET_QK_WIDTH)),
            _const_spec((CHUNK, RET_QK_WIDTH)),
            *[_alias_spec() for _ in prev_states],
        ],
        out_specs=[
            pl.BlockSpec((CHUNK, SSD_WIDTH + RET_WIDTH), row_map),
            pl.BlockSpec((None, 1, CONV_WIDTH - 1, CONV_DIM), lambda b, c: (layer, b, 0, 0)),
            pl.BlockSpec((None, 1, SSD_WIDTH, SSD_STATE), lambda b, c: (layer, b, 0, 0)),
            pl.BlockSpec((None, 1, RET_HEADS, RET_QK_DIM, RET_V_DIM), lambda b, c: (layer, b, 0, 0, 0)),
        ],
        out_shape=[jax.ShapeDtypeStruct((t, SSD_WIDTH + RET_WIDTH), BF16), *_state_out_shapes(batch)],
        input_output_aliases={n_in + k: 1 + k for k in range(len(prev_states))},
        scratch_shapes=[
            pltpu.VMEM((CHUNK + 8, CONV_DIM), F32),
            pltpu.VMEM((SSD_STATE, SSD_WIDTH), F32),
        ],
        compiler_params=_cparams(("parallel", "arbitrary")),
        name="mixer_prompt",
    )(proj, dt_raw, *_mixer_params(pw), jnp.asarray(cos), jnp.asarray(sin), jnp.asarray(dmat),
      jnp.asarray(kdec), jnp.asarray(qdec), *prev_states)


def _pad_rows(x, rows):
    return jnp.concatenate([x, jnp.zeros((rows - x.shape[0], x.shape[1]), x.dtype)], axis=0)


def _mixer_sample_kernel(proj_ref, dt_ref, convs_ref, ssm_in_ref, ret_in_ref, convw_ref, convb_ref, dtb_ref,
                         alog_ref, exp_ref, dskip_ref, sgain_ref, rgain_ref, cos_ref, sin_ref, dmat_ref,
                         kdec_ref, qdec_ref, tri_ref, *rest, bb, cl, ret_chunk_decay):
    y_ref, conv_ref, ssm_ref, ret_ref, xp_scr = rest[-5:]
    for i in range(bb):
        xp_scr[i, 8 - (CONV_WIDTH - 1):8, :] = convs_ref[i]
        xp_scr[i, 8:8 + cl, :] = proj_ref[i * cl:(i + 1) * cl, OFF_XBC:OFF_XBC + CONV_DIM]
        conv_ref[i] = xp_scr[i, 8 + cl - (CONV_WIDTH - 1):8 + cl, :]
    acc = None
    for j in range(CONV_WIDTH):
        s = 8 - (CONV_WIDTH - 1) + j
        tap = jnp.concatenate([xp_scr[i, s:s + cl, :] for i in range(bb)], axis=0) * convw_ref[j:j + 1, :]
        acc = convb_ref[...] + tap if acc is None else acc + tap
    xbc = _silu(acc)
    xs = xbc[:, 0:SSD_WIDTH]
    bm = xbc[:, SSD_WIDTH:SSD_WIDTH + SSD_GROUPS * SSD_STATE]
    cm = xbc[:, SSD_WIDTH + SSD_GROUPS * SSD_STATE:CONV_DIM]

    expand = exp_ref[...]
    dt, acum = _dt_terms(dt_ref[...], dtb_ref[...], alog_ref[...], tri_ref[...], expand)
    eacum = jnp.exp(acum)
    alast = jnp.concatenate(
        [jnp.broadcast_to(acum[(i + 1) * cl - 1:(i + 1) * cl, :], (cl, LANES)) for i in range(bb)], axis=0)
    dt_x = _dot_sel(dt, expand)
    eacum_x = _dot_sel(eacum, expand)
    dte_x = _dot_sel(jnp.exp(alast - acum), expand)
    xdt = xs * dt_x
    xdtd = xdt * dte_x
    row = lax.broadcasted_iota(jnp.int32, (cl, LANES), 0)
    col = lax.broadcasted_iota(jnp.int32, (cl, LANES), 1)
    causal = row >= col
    lane = col

    y_rows = []
    for i in range(bb):
        rs = slice(i * cl, (i + 1) * cl)
        acum_i = acum[rs]
        acum_t = _pad_rows(acum_i, LANES).T
        xdt_p = _pad_rows(xdt[rs], LANES).astype(BF16)
        xdtd_p = _pad_rows(xdtd[rs], LANES)
        chunk_decay = eacum[(i + 1) * cl - 1:(i + 1) * cl, :]
        y_parts = []
        for g in range(SSD_GROUPS):
            bg_p = _pad_rows(bm[rs, g * SSD_STATE:(g + 1) * SSD_STATE], LANES)
            cg_b = cm[rs, g * SSD_STATE:(g + 1) * SSD_STATE].astype(BF16)
            cb = _dot_nt(cg_b, bg_p.astype(BF16))
            gsl = slice(g * GROUP_WIDTH, (g + 1) * GROUP_WIDTH)
            st_g = ssm_in_ref[i, gsl, :]
            y_off = _dot_nt(cg_b, st_g.astype(BF16)) * eacum_x[rs, gsl]
            upd = _dot(xdtd_p[:, gsl].T.astype(BF16), bg_p.astype(BF16))
            for hh in range(SSD_HEADS // SSD_GROUPS):
                h = g * (SSD_HEADS // SSD_GROUPS) + hh
                hsl = slice(hh * SSD_HEAD_DIM, (hh + 1) * SSD_HEAD_DIM)
                ssm_ref[i, h * SSD_HEAD_DIM:(h + 1) * SSD_HEAD_DIM, :] = (
                    st_g[hsl, :] * chunk_decay[:, h:h + 1] + upd[hsl, :])
            for j in range(GROUP_WIDTH // LANES):
                h0 = g * (SSD_HEADS // SSD_GROUPS) + 2 * j
                psl = slice(h0 * SSD_HEAD_DIM, (h0 + 2) * SSD_HEAD_DIM)
                res = []
                for hh in (h0, h0 + 1):
                    seg = acum_i[:, hh:hh + 1] - acum_t[hh:hh + 1, :]
                    lmat = jnp.exp(jnp.where(causal, seg, -jnp.inf))
                    res.append(_dot((cb * lmat).astype(BF16), xdt_p[:, psl]))
                y_parts.append(jnp.where(lane < SSD_HEAD_DIM, res[0], res[1]) + y_off[:, j * LANES:(j + 1) * LANES])
        y_rows.append(jnp.concatenate(y_parts, axis=-1))
    y = jnp.concatenate(y_rows, axis=0)
    z = proj_ref[:, OFF_Z:OFF_Z + SSD_WIDTH]
    y_ref[:, 0:SSD_WIDTH] = _ssd_out(y, xs, z, dskip_ref[...], sgain_ref[...]).astype(y_ref.dtype)

    cos = cos_ref[...]
    sin = sin_ref[...]
    q_heads = _rotary(proj_ref[:, OFF_Q:OFF_Q + RET_QK_WIDTH], cos, sin)
    k_heads = _rotary(proj_ref[:, OFF_K:OFF_K + RET_QK_WIDTH], cos, sin)
    o_heads = []
    for h in range(RET_HEADS):
        hs = slice(h * RET_QK_DIM, (h + 1) * RET_QK_DIM)
        kh_all = k_heads[h] * (RET_QK_DIM ** -0.5)
        q_start = q_heads[h] * qdec_ref[:, hs]
        k_end = kh_all * kdec_ref[:, hs]
        o_rows = []
        for i in range(bb):
            rs = slice(i * cl, (i + 1) * cl)
            vh_p = _pad_rows(proj_ref[rs, OFF_V + h * RET_V_DIM:OFF_V + (h + 1) * RET_V_DIM], LANES).astype(BF16)
            kh_p = _pad_rows(kh_all[rs], LANES).astype(BF16)
            scores = _dot_nt(q_heads[h][rs].astype(BF16), kh_p) * dmat_ref[h]
            o_intra = _dot(scores.astype(BF16), vh_p)
            s_in = ret_in_ref[i, h]
            o_cross = _dot(q_start[rs].astype(BF16), s_in.astype(BF16))
            kv = _dot(_pad_rows(k_end[rs], LANES).T.astype(BF16), vh_p)
            ret_ref[i, h] = ret_chunk_decay[h] * s_in + kv
            o_rows.append(o_intra + o_cross)
        o_heads.append(jnp.concatenate(o_rows, axis=0))
    gate = proj_ref[:, OFF_G:OFF_G + RET_WIDTH]
    y_ref[:, SSD_WIDTH:SSD_WIDTH + RET_WIDTH] = _ret_out(o_heads, gate, rgain_ref[...]).astype(y_ref.dtype)


def _mixer_sample(layer, proj, dt_raw, conv_state, ssm_state, ret_state, pw, prev_states, batch, cl, bb):
    dmat, kdec, qdec, chunk_decay = _ret_tables(cl, bb)
    dmat = np.concatenate([dmat, np.zeros((RET_HEADS, cl, LANES - cl), np.float32)], axis=-1)
    cos, sin = _rope_tables(PAST_LEN, cl, bb)
    m = bb * cl
    tri = np.kron(np.eye(bb, dtype=np.float32), np.tril(np.ones((cl, cl), np.float32)))
    kern = functools.partial(_mixer_sample_kernel, bb=bb, cl=cl, ret_chunk_decay=chunk_decay)
    row_map = lambda i: (i, 0)
    state_specs = [
        pl.BlockSpec((None, bb, CONV_WIDTH - 1, CONV_DIM), lambda i: (layer, i, 0, 0)),
        pl.BlockSpec((None, bb, SSD_WIDTH, SSD_STATE), lambda i: (layer, i, 0, 0)),
        pl.BlockSpec((None, bb, RET_HEADS, RET_QK_DIM, RET_V_DIM), lambda i: (layer, i, 0, 0, 0)),
    ]
    n_in = 19
    return pl.pallas_call(
        kern,
        grid=(batch // bb,),
        in_specs=[
            pl.BlockSpec((m, PROJ_MAIN), row_map),
            pl.BlockSpec((m, LANES), row_map),
            *state_specs,
            *_mixer_param_specs(layer),
            _const_spec((m, LANES)),
            _const_spec((m, LANES)),
            _const_spec((RET_HEADS, cl, LANES)),
            _const_spec((m, RET_QK_WIDTH)),
            _const_spec((m, RET_QK_WIDTH)),
            _const_spec((m, m)),
            *[_alias_spec() for _ in prev_states],
        ],
        out_specs=[pl.BlockSpec((m, SSD_WIDTH + RET_WIDTH), row_map), *state_specs],
        out_shape=[jax.ShapeDtypeStruct((batch * cl, SSD_WIDTH + RET_WIDTH), BF16), *_state_out_shapes(batch)],
        input_output_aliases={n_in + k: 1 + k for k in range(len(prev_states))},
        scratch_shapes=[pltpu.VMEM((bb, 8 + cl, CONV_DIM), F32)],
        compiler_params=_cparams(("parallel",)),
        name="mixer_sample",
    )(proj, dt_raw, conv_state, ssm_state, ret_state, *_mixer_params(pw), jnp.asarray(cos), jnp.asarray(sin),
      jnp.asarray(dmat), jnp.asarray(kdec), jnp.asarray(qdec), jnp.asarray(tri, dtype=BF16), *prev_states)


def _outproj_kernel(x_ref, y_ref, w_ref, o_ref):
    o_ref[...] = x_ref[...] + _dot(y_ref[...], w_ref[...])


def _outproj(x, y, layer, pw, tm):
    t = x.shape[0]
    k = y.shape[1]
    return pl.pallas_call(
        _outproj_kernel,
        grid=(t // tm,),
        in_specs=[
            pl.BlockSpec((tm, D_MODEL), lambda i: (i, 0)),
            pl.BlockSpec((tm, k), lambda i: (i, 0)),
            _layer_spec((k, D_MODEL), layer),
        ],
        out_specs=pl.BlockSpec((tm, D_MODEL), lambda i: (i, 0)),
        out_shape=jax.ShapeDtypeStruct((t, D_MODEL), F32),
        compiler_params=_cparams(("parallel",)),
        name="outproj",
    )(x, y, pw["w_out"])


def _memproj_kernel(m_ref, w_ref, k_ref, v_ref, kb_ref, vb_ref, *, bb):
    r = _dot(m_ref[...].reshape(bb * N_MEM, D_MODEL).astype(BF16), w_ref[...])
    kb_ref[...] = r[:, 0:D_MODEL].astype(BF16).reshape(bb, N_MEM, D_MODEL)
    vb_ref[...] = r[:, D_MODEL:2 * D_MODEL].astype(BF16).reshape(bb, N_MEM, D_MODEL)
    for i in range(bb):
        rows = slice(i * N_MEM, (i + 1) * N_MEM)
        for hd in range(XA_HEADS):
            k_ref[i, :, hd, :] = r[rows, hd * XA_HEAD_DIM:(hd + 1) * XA_HEAD_DIM]
            v_ref[i, :, hd, :] = r[rows, D_MODEL + hd * XA_HEAD_DIM:D_MODEL + (hd + 1) * XA_HEAD_DIM]


def _memproj(mem, pw, bb):
    b = mem.shape[0]
    cache_spec = pl.BlockSpec((None, bb, N_MEM, XA_HEADS, XA_HEAD_DIM), lambda l, i: (l, i, 0, 0, 0))
    cache_shape = jax.ShapeDtypeStruct((DEPTH, b, N_MEM, XA_HEADS, XA_HEAD_DIM), F32)
    flat_spec = pl.BlockSpec((None, bb, N_MEM, D_MODEL), lambda l, i: (l, i, 0, 0))
    flat_shape = jax.ShapeDtypeStruct((DEPTH, b, N_MEM, D_MODEL), BF16)
    return pl.pallas_call(
        functools.partial(_memproj_kernel, bb=bb),
        grid=(DEPTH, b // bb),
        in_specs=[
            pl.BlockSpec((bb, N_MEM, D_MODEL), lambda l, i: (i, 0, 0)),
            pl.BlockSpec((None, D_MODEL, 2 * D_MODEL), lambda l, i: (l, 0, 0)),
        ],
        out_specs=[cache_spec, cache_spec, flat_spec, flat_spec],
        out_shape=[cache_shape, cache_shape, flat_shape, flat_shape],
        compiler_params=_cparams(("parallel", "parallel")),
        name="memproj",
    )(mem, pw["w_kv"])


def _softmax_rows(s):
    e = jnp.exp(s - jnp.max(s, axis=-1, keepdims=True))
    return e / jnp.sum(e, axis=-1, keepdims=True)


def _xattn_prompt_kernel(x_ref, g_ref, wq_ref, wo_ref, k_ref, v_ref, o_ref):
    x = x_ref[0]
    h = (_rms(x) * g_ref[...]).astype(BF16)
    q = _dot(h, wq_ref[...])
    scale = XA_HEAD_DIM ** -0.5
    heads = []
    for hd in range(XA_HEADS):
        hs = slice(hd * XA_HEAD_DIM, (hd + 1) * XA_HEAD_DIM)
        p = _softmax_rows(_dot_nt(q[:, hs].astype(BF16), k_ref[0, :, hs]) * scale)
        heads.append(_dot(p.astype(BF16), v_ref[0, :, hs]))
    o = jnp.concatenate(heads, axis=-1).astype(BF16)
    o_ref[0] = x + _dot(o, wo_ref[...])


def _xattn_cache_kernel(x_ref, g_ref, wq_ref, wo_ref, k_ref, v_ref, o_ref, *, bb, tm):
    x = x_ref[...].reshape(bb * tm, D_MODEL)
    h = (_rms(x) * g_ref[...]).astype(BF16)
    q = _dot(h, wq_ref[...])
    scale = XA_HEAD_DIM ** -0.5
    rows_flat = XA_HEADS * tm
    assert tm & (tm - 1) == 0 and XA_HEADS & (XA_HEADS - 1) == 0
    row_head = lax.shift_right_logical(lax.broadcasted_iota(jnp.int32, (rows_flat, N_MEM * XA_HEADS), 0),
                                       int(math.log2(tm)))
    col_head = lax.broadcasted_iota(jnp.int32, (rows_flat, N_MEM * XA_HEADS), 1) & (XA_HEADS - 1)
    own_head = row_head == col_head
    outs = []
    for i in range(bb):
        qi = q[i * tm:(i + 1) * tm]
        qf = jnp.concatenate([qi[:, hd * XA_HEAD_DIM:(hd + 1) * XA_HEAD_DIM] for hd in range(XA_HEADS)], axis=0)
        k_all = k_ref[i].reshape(N_MEM * XA_HEADS, XA_HEAD_DIM).astype(BF16)
        v_all = v_ref[i].reshape(N_MEM * XA_HEADS, XA_HEAD_DIM).astype(BF16)
        s = jnp.where(own_head, _dot_nt(qf.astype(BF16), k_all) * scale, -jnp.inf)
        of = _dot(_softmax_rows(s).astype(BF16), v_all)
        outs.append(jnp.concatenate([of[hd * tm:(hd + 1) * tm] for hd in range(XA_HEADS)], axis=-1))
    o = jnp.concatenate(outs, axis=0).astype(BF16)
    o_ref[...] = (x + _dot(o, wo_ref[...])).reshape(bb, tm, D_MODEL)


def _xattn_weight_specs(layer):
    return [_layer_spec((1, D_MODEL), layer), _layer_spec((D_MODEL, D_MODEL), layer),
            _layer_spec((D_MODEL, D_MODEL), layer)]


def _xattn_prompt(x, layer, pw, mem_k, mem_v, tm):
    b, l, _ = x.shape
    mem_spec = pl.BlockSpec((None, 1, N_MEM, D_MODEL), lambda i, j: (layer, i, 0, 0))
    return pl.pallas_call(
        _xattn_prompt_kernel,
        grid=(b, l // tm),
        in_specs=[pl.BlockSpec((1, tm, D_MODEL), lambda i, j: (i, j, 0)), *_xattn_weight_specs(layer),
                  mem_spec, mem_spec],
        out_specs=pl.BlockSpec((1, tm, D_MODEL), lambda i, j: (i, j, 0)),
        out_shape=jax.ShapeDtypeStruct((b, l, D_MODEL), F32),
        compiler_params=_cparams(("parallel", "parallel")),
        name="xattn_prompt",
    )(x, pw["norm_mem"], pw["w_mq"], pw["w_mo"], mem_k, mem_v)


def _xattn_cache(x, layer, pw, mem_k, mem_v, bb):
    b, l, _ = x.shape
    kern = functools.partial(_xattn_cache_kernel, bb=bb, tm=l)
    mem_spec = pl.BlockSpec((None, bb, N_MEM, XA_HEADS, XA_HEAD_DIM), lambda i: (layer, i, 0, 0, 0))
    return pl.pallas_call(
        kern,
        grid=(b // bb,),
        in_specs=[pl.BlockSpec((bb, l, D_MODEL), lambda i: (i, 0, 0)), *_xattn_weight_specs(layer),
                  mem_spec, mem_spec],
        out_specs=pl.BlockSpec((bb, l, D_MODEL), lambda i: (i, 0, 0)),
        out_shape=jax.ShapeDtypeStruct((b, l, D_MODEL), F32),
        compiler_params=_cparams(("parallel",)),
        name="xattn_cache",
    )(x, pw["norm_mem"], pw["w_mq"], pw["w_mo"], mem_k, mem_v)


def _router_gates(logits):
    m = logits.shape[0]
    lane_i = lax.broadcasted_iota(jnp.int32, (m, LANES), 1)
    lane = lane_i.astype(F32)
    big = float(LANES)
    is_g = lane_i < N_EGROUPS
    gl = jnp.where(is_g, logits, -jnp.inf)
    gmax = jnp.max(gl, axis=-1, keepdims=True)
    g_idx = jnp.min(jnp.where(is_g & (gl == gmax), lane, big), axis=-1, keepdims=True)
    g_prob = 1.0 / jnp.sum(jnp.exp(gl - gmax), axis=-1, keepdims=True)
    e_lane = lane_i - ROUTER_OFF
    e_group = lax.shift_right_arithmetic(e_lane, int(math.log2(EXPERTS_PER_GROUP))).astype(F32)
    sel = (e_lane >= 0) & (e_lane < N_EXPERTS) & (e_group == g_idx)
    el = jnp.where(sel, logits, -jnp.inf)
    emax = jnp.max(el, axis=-1, keepdims=True)
    ee = jnp.exp(el - emax)
    e_prob = ee / jnp.sum(ee, axis=-1, keepdims=True)
    p1 = jnp.max(jnp.where(sel, e_prob, -1.0), axis=-1, keepdims=True)
    i1 = jnp.min(jnp.where(sel & (e_prob == p1), lane, big), axis=-1, keepdims=True)
    sel2 = sel & (lane != i1)
    p2 = jnp.max(jnp.where(sel2, e_prob, -1.0), axis=-1, keepdims=True)
    i2 = jnp.min(jnp.where(sel2 & (e_prob == p2), lane, big), axis=-1, keepdims=True)
    denom = p1 + p2
    w1 = g_prob * p1 / denom
    w2 = g_prob * p2 / denom
    return jnp.where(lane == i1, w1, 0.0) + jnp.where(lane == i2, w2, 0.0)


def _moe_kernel(x_ref, g_ref, wr_hi_ref, wr_lo_ref, br_ref, wg_ref, wu_ref, wd_ref, gf_ref, o_ref,
                h_scr, gate_scr, *, final_norm):
    e = pl.program_id(1)

    @pl.when(e == 0)
    def _():
        h = _rms(x_ref[...]) * g_ref[...]
        h_hi = h.astype(BF16)
        h_lo = (h - h_hi.astype(F32)).astype(BF16)
        logits = (_dot(h_hi, wr_hi_ref[...]) + _dot(h_hi, wr_lo_ref[...]) + _dot(h_lo, wr_hi_ref[...])
                  + br_ref[...])
        h_scr[...] = h_hi
        gate_scr[...] = _router_gates(logits)
        o_ref[...] = jnp.zeros_like(o_ref)

    hb = h_scr[...]
    lane = lax.broadcasted_iota(jnp.int32, gate_scr.shape, 1)
    gate_e = jnp.sum(jnp.where(lane == e + ROUTER_OFF, gate_scr[...], 0.0), axis=-1, keepdims=True)
    a = _silu(_dot(hb, wg_ref[...])) * _dot(hb, wu_ref[...]) * gate_e
    o_ref[...] += _dot(a.astype(BF16), wd_ref[...])

    @pl.when(e == pl.num_programs(1) - 1)
    def _():
        y = x_ref[...] + o_ref[...]
        if final_norm:
            y = _rms(y) * gf_ref[...]
        o_ref[...] = y


def _moe(x, layer, pw, final_norm, tm):
    t = x.shape[0]
    kern = functools.partial(_moe_kernel, final_norm=final_norm)
    return pl.pallas_call(
        kern,
        grid=(t // tm, N_EXPERTS),
        in_specs=[
            pl.BlockSpec((tm, D_MODEL), lambda i, e: (i, 0)),
            _layer_spec((1, D_MODEL), layer),
            _layer_spec((D_MODEL, LANES), layer),
            _layer_spec((D_MODEL, LANES), layer),
            _layer_spec((1, LANES), layer),
            pl.BlockSpec((None, None, D_MODEL, EXPERT_FF), lambda i, e: (layer, e, 0, 0)),
            pl.BlockSpec((None, None, D_MODEL, EXPERT_FF), lambda i, e: (layer, e, 0, 0)),
            pl.BlockSpec((None, None, EXPERT_FF, D_MODEL), lambda i, e: (layer, e, 0, 0)),
            _const_spec((1, D_MODEL)),
        ],
        out_specs=pl.BlockSpec((tm, D_MODEL), lambda i, e: (i, 0)),
        out_shape=jax.ShapeDtypeStruct((t, D_MODEL), F32),
        scratch_shapes=[pltpu.VMEM((tm, D_MODEL), BF16), pltpu.VMEM((tm, LANES), F32)],
        compiler_params=_cparams(("parallel", "arbitrary")),
        name="moe",
    )(x, pw["norm_ffn"], pw["wr_hi"], pw["wr_lo"], pw["b_router"], pw["w_gate"], pw["w_up"], pw["w_down"],
      pw["norm_final"])


def _row(v):
    return v.reshape(v.shape[0], 1, v.shape[1])


def _pad_lanes(v):
    return jnp.pad(v, ((0, 0),) * (v.ndim - 1) + ((0, LANES - v.shape[-1]),))


def _prep_weights(norm_mix, w_in, conv_w, conv_b, dt_bias, a_log, d_skip, ssd_gain, ret_gain, w_out,
                  norm_mem, w_mq, w_mk, w_mv, w_mo, norm_ffn, w_rg, b_rg, w_re, b_re, w_gate, w_up, w_down,
                  norm_final):
    dt_off = SSD_WIDTH + CONV_DIM
    w_main = jnp.concatenate([w_in[:, :, :dt_off], w_in[:, :, dt_off + SSD_HEADS:]], axis=2).astype(BF16)
    w_dt = _pad_lanes(w_in[:, :, dt_off:dt_off + SSD_HEADS]).astype(BF16)
    w_router = _pad_lanes(jnp.concatenate([w_rg, w_re], axis=2))
    wr_hi = w_router.astype(BF16)
    wr_lo = (w_router - wr_hi.astype(F32)).astype(BF16)
    return dict(
        norm_mix=_row(norm_mix), w_main=w_main, w_dt=w_dt,
        conv_w=conv_w, conv_b=_row(conv_b),
        dt_bias=_row(_pad_lanes(dt_bias)), a_log=_row(_pad_lanes(a_log)),
        expand=jnp.asarray(_expand_matrix(), dtype=BF16),
        d_skip=_row(jnp.repeat(d_skip, SSD_HEAD_DIM, axis=1)),
        ssd_gain=_row(ssd_gain), ret_gain=_row(ret_gain),
        w_out=w_out.astype(BF16),
        norm_mem=_row(norm_mem), w_mq=w_mq.astype(BF16), w_mo=w_mo.astype(BF16),
        w_kv=jnp.concatenate([w_mk, w_mv], axis=2).astype(BF16),
        norm_ffn=_row(norm_ffn), wr_hi=wr_hi, wr_lo=wr_lo,
        b_router=_row(_pad_lanes(jnp.concatenate([b_rg, b_re], axis=1))),
        w_gate=w_gate.astype(BF16), w_up=w_up.astype(BF16), w_down=w_down.astype(BF16),
        norm_final=norm_final.reshape(1, -1),
    )


def _token_tile(t, cap):
    tm = min(t, cap)
    assert t % tm == 0
    return tm


def _trunk(x, mixer_fn, xattn_fn, pw, proj_dtype):
    b, l, _ = x.shape
    t = b * l
    xf = x.reshape(t, D_MODEL)
    states = ()
    for layer in range(DEPTH):
        proj, dt_raw = _inproj(xf, layer, pw, proj_dtype, _token_tile(t, 512))
        y, *states = mixer_fn(layer, proj, dt_raw, tuple(states))
        xf = _outproj(xf, y, layer, pw, _token_tile(t, 512))
        xf = xattn_fn(layer, xf.reshape(b, l, D_MODEL)).reshape(t, D_MODEL)
        xf = _moe(xf, layer, pw, layer == DEPTH - 1, _token_tile(t, 1024))
    conv, ssm, ret = states
    return (xf.reshape(b, l, D_MODEL), ssm.reshape(DEPTH, b, SSD_HEADS, SSD_HEAD_DIM, SSD_STATE), conv, ret)


def kernel(x_prompt, x_sample, mem_prompt, state_ssm, state_conv, state_ret, cache_mem_k, cache_mem_v,
           norm_mix, w_in, conv_w, conv_b, dt_bias, a_log, d_skip, ssd_gain, ret_gain, w_out,
           norm_mem, w_mq, w_mk, w_mv, w_mo, norm_ffn, w_rg, b_rg, w_re, b_re, w_gate, w_up, w_down,
           norm_final):
    pw = _prep_weights(norm_mix, w_in, conv_w, conv_b, dt_bias, a_log, d_skip, ssd_gain, ret_gain, w_out,
                       norm_mem, w_mq, w_mk, w_mv, w_mo, norm_ffn, w_rg, b_rg, w_re, b_re, w_gate, w_up, w_down,
                       norm_final)
    bp, lp, _ = x_prompt.shape
    bs, ls, _ = x_sample.shape
    n_mem = mem_prompt.shape[1]

    assert n_mem == N_MEM
    mem_k_p, mem_v_p, mem_k_rows, mem_v_rows = _memproj(mem_prompt, pw, _token_tile(bp, 2))

    def mixer_p(layer, proj, dt_raw, prev_states):
        return _mixer_prompt(layer, proj, dt_raw, pw, prev_states, bp, lp)

    def xattn_p(layer, x):
        return _xattn_prompt(x, layer, pw, mem_k_rows, mem_v_rows, _token_tile(lp, 512))

    y_prompt, ssm_p, conv_p, ret_p = _trunk(x_prompt, mixer_p, xattn_p, pw, BF16)

    sample_bb = 4
    ssm_in = state_ssm.reshape(DEPTH, bs, SSD_WIDTH, SSD_STATE)

    def mixer_s(layer, proj, dt_raw, prev_states):
        return _mixer_sample(layer, proj, dt_raw, state_conv, ssm_in, state_ret, pw, prev_states, bs, ls,
                             sample_bb)

    def xattn_s(layer, x):
        return _xattn_cache(x, layer, pw, cache_mem_k, cache_mem_v, sample_bb)

    y_sample, ssm_s, conv_s, ret_s = _trunk(x_sample, mixer_s, xattn_s, pw, F32)
    return (y_prompt, y_sample, ssm_p, conv_p, ret_p, mem_k_p, mem_v_p, ssm_s, conv_s, ret_s)
```

```python
import functools
import math

import numpy as np
import jax
import jax.numpy as jnp
from jax import lax
from jax.experimental import pallas as pl
from jax.experimental.pallas import tpu as pltpu

F32 = jnp.float32
BF16 = jnp.bfloat16

D_MODEL = 1024
DEPTH = 2
PAST_LEN = 16384
SSD_HEAD_DIM = 64
SSD_HEADS = 16
SSD_GROUPS = 2
SSD_STATE = 128
SSD_WIDTH = 1024
GROUP_WIDTH = SSD_WIDTH // SSD_GROUPS
CONV_WIDTH = 4
CONV_DIM = SSD_WIDTH + 2 * SSD_GROUPS * SSD_STATE
RET_HEADS = 4
RET_V_DIM = 256
RET_QK_DIM = 128
RET_WIDTH = 1024
RET_QK_WIDTH = RET_HEADS * RET_QK_DIM
ROPE_BASE = 10000.0
N_MEM = 256
XA_HEADS = 4
XA_HEAD_DIM = 256
N_EGROUPS = 4
EXPERTS_PER_GROUP = 4
N_EXPERTS = 16
EXPERT_FF = 512
RMS_EPS = 1e-6

LANES = 128
CHUNK = 128
OFF_Z = 0
OFF_XBC = OFF_Z + SSD_WIDTH
OFF_Q = OFF_XBC + CONV_DIM
OFF_K = OFF_Q + RET_QK_WIDTH
OFF_V = OFF_K + RET_QK_WIDTH
OFF_G = OFF_V + RET_WIDTH
PROJ_MAIN = OFF_G + RET_WIDTH
INPROJ_TN = 512
ROUTER_OFF = N_EGROUPS

VMEM_LIMIT = 56 * 1024 * 1024


def _cparams(sem):
    return pltpu.CompilerParams(dimension_semantics=sem, vmem_limit_bytes=VMEM_LIMIT)


def _const_spec(shape):
    nd = len(shape)
    return pl.BlockSpec(shape, lambda *_: (0,) * nd, pipeline_mode=pl.Buffered(1))


def _layer_spec(shape, layer):
    nd = len(shape)
    return pl.BlockSpec((None,) + tuple(shape), lambda *_: (layer,) + (0,) * nd, pipeline_mode=pl.Buffered(1))


def _alias_spec():
    return pl.BlockSpec(memory_space=pl.ANY)


def _rms(x):
    return x * lax.rsqrt(jnp.mean(x * x, axis=-1, keepdims=True) + RMS_EPS)


def _silu(x):
    return x * jax.nn.sigmoid(x)


def _softplus(x):
    return jnp.maximum(x, 0.0) + jnp.log1p(jnp.exp(-jnp.abs(x)))


def _split3(x):
    hi = x.astype(BF16)
    r = x - hi.astype(F32)
    mid = r.astype(BF16)
    lo = (r - mid.astype(F32)).astype(BF16)
    return hi, mid, lo


def _dot(a, b):
    return jnp.dot(a, b, preferred_element_type=F32)


def _dot_nt(a, b):
    return lax.dot_general(a, b, (((1,), (1,)), ((), ())), preferred_element_type=F32)


def _dot_sel(x, sel):
    hi, mid, lo = _split3(x)
    return _dot(hi, sel) + _dot(mid, sel) + _dot(lo, sel)


def _sel_dot(sel, x):
    hi, mid, lo = _split3(x)
    return _dot(sel, hi) + _dot(sel, mid) + _dot(sel, lo)


def _inproj_kernel(x_ref, g_ref, w_ref, wdt_ref, o_ref, odt_ref):
    h = (_rms(x_ref[...]) * g_ref[...]).astype(BF16)
    for j in range(PROJ_MAIN // INPROJ_TN):
        sl = slice(j * INPROJ_TN, (j + 1) * INPROJ_TN)
        o_ref[:, sl] = _dot(h, w_ref[:, sl]).astype(o_ref.dtype)
    odt_ref[...] = _dot(h, wdt_ref[...])


def _inproj(x, layer, pw, out_dtype, tm):
    t = x.shape[0]
    return pl.pallas_call(
        _inproj_kernel,
        grid=(t // tm,),
        in_specs=[
            pl.BlockSpec((tm, D_MODEL), lambda i: (i, 0)),
            _layer_spec((1, D_MODEL), layer),
            _layer_spec((D_MODEL, PROJ_MAIN), layer),
            _layer_spec((D_MODEL, LANES), layer),
        ],
        out_specs=[
            pl.BlockSpec((tm, PROJ_MAIN), lambda i: (i, 0)),
            pl.BlockSpec((tm, LANES), lambda i: (i, 0)),
        ],
        out_shape=[
            jax.ShapeDtypeStruct((t, PROJ_MAIN), out_dtype),
            jax.ShapeDtypeStruct((t, LANES), F32),
        ],
        compiler_params=_cparams(("parallel",)),
        name="inproj",
    )(x, pw["norm_mix"], pw["w_main"], pw["w_dt"])


def _dt_terms(dt_raw, dtb, alog, tri, expand):
    dt = _softplus(dt_raw + dtb)
    a = dt * (-jnp.exp(alog))
    acum = _sel_dot(tri, a)
    return dt, acum


def _ssd_out(y, xs, z, dskip, gain):
    y = (y + dskip * xs) * _silu(z)
    parts = []
    for g in range(SSD_GROUPS):
        parts.append(_rms(y[:, g * GROUP_WIDTH:(g + 1) * GROUP_WIDTH]))
    return jnp.concatenate(parts, axis=-1) * gain


def _rotary(x, cos, sin_signed):
    parts = []
    for h in range(RET_HEADS):
        xh = x[:, h * RET_QK_DIM:(h + 1) * RET_QK_DIM]
        parts.append(xh * cos + pltpu.roll(xh, RET_QK_DIM // 2, axis=1) * sin_signed)
    return parts


def _ret_out(o_heads, g, gain):
    o = jnp.concatenate([_rms(o) for o in o_heads], axis=-1)
    return o * gain * _silu(g)


def _mixer_prompt_kernel(proj_ref, dt_ref, convw_ref, convb_ref, dtb_ref, alog_ref, exp_ref, dskip_ref,
                         sgain_ref, rgain_ref, cos_ref, sin_ref, dmat_ref, kdec_ref, qdec_ref,
                         *rest, ret_chunk_decay):
    y_ref, conv_ref, ssm_ref, ret_ref, xp_scr, st_scr = rest[-6:]
    cl = CHUNK
    c = pl.program_id(1)
    nc = pl.num_programs(1)

    @pl.when(c == 0)
    def _():
        xp_scr[0:8, :] = jnp.zeros((8, CONV_DIM), F32)
        st_scr[...] = jnp.zeros_like(st_scr)
        ret_ref[...] = jnp.zeros_like(ret_ref)

    xp_scr[8:8 + cl, :] = proj_ref[:, OFF_XBC:OFF_XBC + CONV_DIM].astype(F32)
    acc = convb_ref[...] + xp_scr[8 - (CONV_WIDTH - 1):8 - (CONV_WIDTH - 1) + cl, :] * convw_ref[0:1, :]
    for j in range(1, CONV_WIDTH):
        s = 8 - (CONV_WIDTH - 1) + j
        acc = acc + xp_scr[s:s + cl, :] * convw_ref[j:j + 1, :]
    conv_ref[0] = xp_scr[8 + cl - (CONV_WIDTH - 1):8 + cl, :]
    xp_scr[0:8, :] = xp_scr[cl:cl + 8, :]
    xbc = _silu(acc)
    xs = xbc[:, 0:SSD_WIDTH]
    bm = xbc[:, SSD_WIDTH:SSD_WIDTH + SSD_GROUPS * SSD_STATE]
    cm = xbc[:, SSD_WIDTH + SSD_GROUPS * SSD_STATE:CONV_DIM]

    row = lax.broadcasted_iota(jnp.int32, (cl, cl), 0)
    col = lax.broadcasted_iota(jnp.int32, (cl, cl), 1)
    causal = row >= col
    tri = jnp.where(causal, 1.0, 0.0).astype(BF16)
    expand = exp_ref[...]
    dt, acum = _dt_terms(dt_ref[...], dtb_ref[...], alog_ref[...], tri, expand)
    acum_t = acum.T
    eacum = jnp.exp(acum)
    dt_x = _dot_sel(dt, expand)
    eacum_x = _dot_sel(eacum, expand)
    dte_x = _dot_sel(jnp.exp(acum[cl - 1:cl, :] - acum), expand)
    xdt = xs * dt_x
    xdt_b = xdt.astype(BF16)
    xdtd_b = (xdt * dte_x).astype(BF16)
    lane = lax.broadcasted_iota(jnp.int32, (cl, LANES), 1)

    y_parts = []
    upd_parts = []
    for g in range(SSD_GROUPS):
        bg = bm[:, g * SSD_STATE:(g + 1) * SSD_STATE]
        cg_b = cm[:, g * SSD_STATE:(g + 1) * SSD_STATE].astype(BF16)
        bg_b = bg.astype(BF16)
        cb = _dot_nt(cg_b, bg_b)
        gsl = slice(g * GROUP_WIDTH, (g + 1) * GROUP_WIDTH)
        y_off = _dot(cg_b, st_scr[:, gsl].astype(BF16)) * eacum_x[:, gsl]
        upd_parts.append(_dot(bg.T.astype(BF16), xdtd_b[:, gsl]))
        for j in range(GROUP_WIDTH // LANES):
            h0 = g * (SSD_HEADS // SSD_GROUPS) + 2 * j
            psl = slice(h0 * SSD_HEAD_DIM, (h0 + 2) * SSD_HEAD_DIM)
            res = []
            for hh in (h0, h0 + 1):
                seg = acum[:, hh:hh + 1] - acum_t[hh:hh + 1, :]
                lmat = jnp.exp(jnp.where(causal, seg, -jnp.inf))
                res.append(_dot((cb * lmat).astype(BF16), xdt_b[:, psl]))
            y_parts.append(jnp.where(lane < SSD_HEAD_DIM, res[0], res[1]) + y_off[:, j * LANES:(j + 1) * LANES])
    y = jnp.concatenate(y_parts, axis=-1)
    st_new = st_scr[...] * eacum_x[cl - 1:cl, :] + jnp.concatenate(upd_parts, axis=-1)
    st_scr[...] = st_new

    @pl.when(c == nc - 1)
    def _():
        ssm_ref[0] = st_new.T

    z = proj_ref[:, OFF_Z:OFF_Z + SSD_WIDTH].astype(F32)
    y_ref[:, 0:SSD_WIDTH] = _ssd_out(y, xs, z, dskip_ref[...], sgain_ref[...]).astype(y_ref.dtype)

    cos = cos_ref[...]
    sin = sin_ref[...]
    q_heads = _rotary(proj_ref[:, OFF_Q:OFF_Q + RET_QK_WIDTH].astype(F32), cos, sin)
    k_heads = _rotary(proj_ref[:, OFF_K:OFF_K + RET_QK_WIDTH].astype(F32), cos, sin)
    o_heads = []
    for h in range(RET_HEADS):
        qh = q_heads[h]
        kh = k_heads[h] * (RET_QK_DIM ** -0.5)
        vh = proj_ref[:, OFF_V + h * RET_V_DIM:OFF_V + (h + 1) * RET_V_DIM]
        hs = slice(h * RET_QK_DIM, (h + 1) * RET_QK_DIM)
        scores = _dot_nt(qh.astype(BF16), kh.astype(BF16)) * dmat_ref[h]
        o_intra = _dot(scores.astype(BF16), vh)
        s_in = ret_ref[0, h]
        o_cross = _dot((qh * qdec_ref[:, hs]).astype(BF16), s_in.astype(BF16))
        kv = _dot((kh * kdec_ref[:, hs]).T.astype(BF16), vh)
        ret_ref[0, h] = ret_chunk_decay[h] * s_in + kv
        o_heads.append(o_intra + o_cross)
    gate = proj_ref[:, OFF_G:OFF_G + RET_WIDTH].astype(F32)
    y_ref[:, SSD_WIDTH:SSD_WIDTH + RET_WIDTH] = _ret_out(o_heads, gate, rgain_ref[...]).astype(y_ref.dtype)


def _ret_tables(cl, reps):
    lg = np.log(1.0 - np.exp2(-5.0 - np.arange(RET_HEADS, dtype=np.float64)))
    idx = np.arange(cl, dtype=np.float64)
    rel = idx[:, None] - idx[None, :]
    dmat = np.where(rel[None] >= 0, np.exp(rel[None] * lg[:, None, None]), 0.0).astype(np.float32)
    kdec = np.exp((cl - 1 - idx)[:, None] * lg[None, :]).astype(np.float32)
    qdec = np.exp((idx + 1.0)[:, None] * lg[None, :]).astype(np.float32)
    kdec = np.tile(np.repeat(kdec, RET_QK_DIM, axis=1), (reps, 1))
    qdec = np.tile(np.repeat(qdec, RET_QK_DIM, axis=1), (reps, 1))
    chunk_decay = [float(v) for v in np.exp(cl * lg).astype(np.float32)]
    return dmat, kdec, qdec, chunk_decay


def _rope_tables(pos0, length, reps):
    half = RET_QK_DIM // 2
    inv = ROPE_BASE ** (-np.arange(half, dtype=np.float64) / half)
    pos = (pos0 + np.arange(length)).astype(np.float64)
    ang = pos[:, None] * inv[None, :]
    cos = np.cos(ang).astype(np.float32)
    sin = np.sin(ang).astype(np.float32)
    cos2 = np.tile(np.concatenate([cos, cos], axis=1), (reps, 1))
    sin2 = np.tile(np.concatenate([-sin, sin], axis=1), (reps, 1))
    return cos2, sin2


def _expand_matrix():
    e = np.zeros((LANES, SSD_WIDTH), np.float32)
    for h in range(SSD_HEADS):
        e[h, h * SSD_HEAD_DIM:(h + 1) * SSD_HEAD_DIM] = 1.0
    return e


def _mixer_param_specs(layer):
    return [
        _layer_spec((CONV_WIDTH, CONV_DIM), layer),
        _layer_spec((1, CONV_DIM), layer),
        _layer_spec((1, LANES), layer),
        _layer_spec((1, LANES), layer),
        _const_spec((LANES, SSD_WIDTH)),
        _layer_spec((1, SSD_WIDTH), layer),
        _layer_spec((1, SSD_WIDTH), layer),
        _layer_spec((1, RET_WIDTH), layer),
    ]


def _mixer_params(pw):
    return (pw["conv_w"], pw["conv_b"], pw["dt_bias"], pw["a_log"], pw["expand"], pw["d_skip"],
            pw["ssd_gain"], pw["ret_gain"])


def _state_out_shapes(batch):
    return [
        jax.ShapeDtypeStruct((DEPTH, batch, CONV_WIDTH - 1, CONV_DIM), F32),
        jax.ShapeDtypeStruct((DEPTH, batch, SSD_WIDTH, SSD_STATE), F32),
        jax.ShapeDtypeStruct((DEPTH, batch, RET_HEADS, RET_QK_DIM, RET_V_DIM), F32),
    ]


def _mixer_prompt(layer, proj, dt_raw, pw, prev_states, batch, seq):
    nc = seq // CHUNK
    dmat, kdec, qdec, chunk_decay = _ret_tables(CHUNK, 1)
    cos, sin = _rope_tables(0, seq, 1)
    t = batch * seq
    row_map = lambda b, c: (b * nc + c, 0)
    kern = functools.partial(_mixer_prompt_kernel, ret_chunk_decay=chunk_decay)
    n_in = 15
    return pl.pallas_call(
        kern,
        grid=(batch, nc),
        in_specs=[
            pl.BlockSpec((CHUNK, PROJ_MAIN), row_map),
            pl.BlockSpec((CHUNK, LANES), row_map),
            *_mixer_param_specs(layer),
            pl.BlockSpec((CHUNK, LANES), lambda b, c: (c, 0)),
            pl.BlockSpec((CHUNK, LANES), lambda b, c: (c, 0)),
            _const_spec((RET_HEADS, CHUNK, CHUNK)),
            _const_spec((CHUNK, RET_QK_WIDTH)),
            _const_spec((CHUNK, RET_QK_WIDTH)),
            *[_alias_spec() for _ in prev_states],
        ],
        out_specs=[
            pl.BlockSpec((CHUNK, SSD_WIDTH + RET_WIDTH), row_map),
            pl.BlockSpec((None, 1, CONV_WIDTH - 1, CONV_DIM), lambda b, c: (layer, b, 0, 0)),
            pl.BlockSpec((None, 1, SSD_WIDTH, SSD_STATE), lambda b, c: (layer, b, 0, 0)),
            pl.BlockSpec((None, 1, RET_HEADS, RET_QK_DIM, RET_V_DIM), lambda b, c: (layer, b, 0, 0, 0)),
        ],
        out_shape=[jax.ShapeDtypeStruct((t, SSD_WIDTH + RET_WIDTH), BF16), *_state_out_shapes(batch)],
        input_output_aliases={n_in + k: 1 + k for k in range(len(prev_states))},
        scratch_shapes=[
            pltpu.VMEM((CHUNK + 8, CONV_DIM), F32),
            pltpu.VMEM((SSD_STATE, SSD_WIDTH), F32),
        ],
        compiler_params=_cparams(("parallel", "arbitrary")),
        name="mixer_prompt",
    )(proj, dt_raw, *_mixer_params(pw), jnp.asarray(cos), jnp.asarray(sin), jnp.asarray(dmat),
      jnp.asarray(kdec), jnp.asarray(qdec), *prev_states)


def _pad_rows(x, rows):
    return jnp.concatenate([x, jnp.zeros((rows - x.shape[0], x.shape[1]), x.dtype)], axis=0)


def _mixer_sample_kernel(proj_ref, dt_ref, convs_ref, ssm_in_ref, ret_in_ref, convw_ref, convb_ref, dtb_ref,
                         alog_ref, exp_ref, dskip_ref, sgain_ref, rgain_ref, cos_ref, sin_ref, dmat_ref,
                         kdec_ref, qdec_ref, tri_ref, *rest, bb, cl, ret_chunk_decay):
    y_ref, conv_ref, ssm_ref, ret_ref, xp_scr = rest[-5:]
    for i in range(bb):
        xp_scr[i, 8 - (CONV_WIDTH - 1):8, :] = convs_ref[i]
        xp_scr[i, 8:8 + cl, :] = proj_ref[i * cl:(i + 1) * cl, OFF_XBC:OFF_XBC + CONV_DIM]
        conv_ref[i] = xp_scr[i, 8 + cl - (CONV_WIDTH - 1):8 + cl, :]
    acc = None
    for j in range(CONV_WIDTH):
        s = 8 - (CONV_WIDTH - 1) + j
        tap = jnp.concatenate([xp_scr[i, s:s + cl, :] for i in range(bb)], axis=0) * convw_ref[j:j + 1, :]
        acc = convb_ref[...] + tap if acc is None else acc + tap
    xbc = _silu(acc)
    xs = xbc[:, 0:SSD_WIDTH]
    bm = xbc[:, SSD_WIDTH:SSD_WIDTH + SSD_GROUPS * SSD_STATE]
    cm = xbc[:, SSD_WIDTH + SSD_GROUPS * SSD_STATE:CONV_DIM]

    expand = exp_ref[...]
    dt, acum = _dt_terms(dt_ref[...], dtb_ref[...], alog_ref[...], tri_ref[...], expand)
    eacum = jnp.exp(acum)
    alast = jnp.concatenate(
        [jnp.broadcast_to(acum[(i + 1) * cl - 1:(i + 1) * cl, :], (cl, LANES)) for i in range(bb)], axis=0)
    dt_x = _dot_sel(dt, expand)
    eacum_x = _dot_sel(eacum, expand)
    dte_x = _dot_sel(jnp.exp(alast - acum), expand)
    xdt = xs * dt_x
    xdtd = xdt * dte_x
    row = lax.broadcasted_iota(jnp.int32, (cl, LANES), 0)
    col = lax.broadcasted_iota(jnp.int32, (cl, LANES), 1)
    causal = row >= col
    lane = col

    y_rows = []
    for i in range(bb):
        rs = slice(i * cl, (i + 1) * cl)
        acum_i = acum[rs]
        acum_t = _pad_rows(acum_i, LANES).T
        xdt_p = _pad_rows(xdt[rs], LANES).astype(BF16)
        xdtd_p = _pad_rows(xdtd[rs], LANES)
        chunk_decay = eacum[(i + 1) * cl - 1:(i + 1) * cl, :]
        y_parts = []
        for g in range(SSD_GROUPS):
            bg_p = _pad_rows(bm[rs, g * SSD_STATE:(g + 1) * SSD_STATE], LANES)
            cg_b = cm[rs, g * SSD_STATE:(g + 1) * SSD_STATE].astype(BF16)
            cb = _dot_nt(cg_b, bg_p.astype(BF16))
            gsl = slice(g * GROUP_WIDTH, (g + 1) * GROUP_WIDTH)
            st_g = ssm_in_ref[i, gsl, :]
            y_off = _dot_nt(cg_b, st_g.astype(BF16)) * eacum_x[rs, gsl]
            upd = _dot(xdtd_p[:, gsl].T.astype(BF16), bg_p.astype(BF16))
            for hh in range(SSD_HEADS // SSD_GROUPS):
                h = g * (SSD_HEADS // SSD_GROUPS) + hh
                hsl = slice(hh * SSD_HEAD_DIM, (hh + 1) * SSD_HEAD_DIM)
                ssm_ref[i, h * SSD_HEAD_DIM:(h + 1) * SSD_HEAD_DIM, :] = (
                    st_g[hsl, :] * chunk_decay[:, h:h + 1] + upd[hsl, :])
            for j in range(GROUP_WIDTH // LANES):
                h0 = g * (SSD_HEADS // SSD_GROUPS) + 2 * j
                psl = slice(h0 * SSD_HEAD_DIM, (h0 + 2) * SSD_HEAD_DIM)
                res = []
                for hh in (h0, h0 + 1):
                    seg = acum_i[:, hh:hh + 1] - acum_t[hh:hh + 1, :]
                    lmat = jnp.exp(jnp.where(causal, seg, -jnp.inf))
                    res.append(_dot((cb * lmat).astype(BF16), xdt_p[:, psl]))
                y_parts.append(jnp.where(lane < SSD_HEAD_DIM, res[0], res[1]) + y_off[:, j * LANES:(j + 1) * LANES])
        y_rows.append(jnp.concatenate(y_parts, axis=-1))
    y = jnp.concatenate(y_rows, axis=0)
    z = proj_ref[:, OFF_Z:OFF_Z + SSD_WIDTH]
    y_ref[:, 0:SSD_WIDTH] = _ssd_out(y, xs, z, dskip_ref[...], sgain_ref[...]).astype(y_ref.dtype)

    cos = cos_ref[...]
    sin = sin_ref[...]
    q_heads = _rotary(proj_ref[:, OFF_Q:OFF_Q + RET_QK_WIDTH], cos, sin)
    k_heads = _rotary(proj_ref[:, OFF_K:OFF_K + RET_QK_WIDTH], cos, sin)
    o_heads = []
    for h in range(RET_HEADS):
        hs = slice(h * RET_QK_DIM, (h + 1) * RET_QK_DIM)
        kh_all = k_heads[h] * (RET_QK_DIM ** -0.5)
        q_start = q_heads[h] * qdec_ref[:, hs]
        k_end = kh_all * kdec_ref[:, hs]
        o_rows = []
        for i in range(bb):
            rs = slice(i * cl, (i + 1) * cl)
            vh_p = _pad_rows(proj_ref[rs, OFF_V + h * RET_V_DIM:OFF_V + (h + 1) * RET_V_DIM], LANES).astype(BF16)
            kh_p = _pad_rows(kh_all[rs], LANES).astype(BF16)
            scores = _dot_nt(q_heads[h][rs].astype(BF16), kh_p) * dmat_ref[h]
            o_intra = _dot(scores.astype(BF16), vh_p)
            s_in = ret_in_ref[i, h]
            o_cross = _dot(q_start[rs].astype(BF16), s_in.astype(BF16))
            kv = _dot(_pad_rows(k_end[rs], LANES).T.astype(BF16), vh_p)
            ret_ref[i, h] = ret_chunk_decay[h] * s_in + kv
            o_rows.append(o_intra + o_cross)
        o_heads.append(jnp.concatenate(o_rows, axis=0))
    gate = proj_ref[:, OFF_G:OFF_G + RET_WIDTH]
    y_ref[:, SSD_WIDTH:SSD_WIDTH + RET_WIDTH] = _ret_out(o_heads, gate, rgain_ref[...]).astype(y_ref.dtype)


def _mixer_sample(layer, proj, dt_raw, conv_state, ssm_state, ret_state, pw, prev_states, batch, cl, bb):
    dmat, kdec, qdec, chunk_decay = _ret_tables(cl, bb)
    dmat = np.concatenate([dmat, np.zeros((RET_HEADS, cl, LANES - cl), np.float32)], axis=-1)
    cos, sin = _rope_tables(PAST_LEN, cl, bb)
    m = bb * cl
    tri = np.kron(np.eye(bb, dtype=np.float32), np.tril(np.ones((cl, cl), np.float32)))
    kern = functools.partial(_mixer_sample_kernel, bb=bb, cl=cl, ret_chunk_decay=chunk_decay)
    row_map = lambda i: (i, 0)
    state_specs = [
        pl.BlockSpec((None, bb, CONV_WIDTH - 1, CONV_DIM), lambda i: (layer, i, 0, 0)),
        pl.BlockSpec((None, bb, SSD_WIDTH, SSD_STATE), lambda i: (layer, i, 0, 0)),
        pl.BlockSpec((None, bb, RET_HEADS, RET_QK_DIM, RET_V_DIM), lambda i: (layer, i, 0, 0, 0)),
    ]
    n_in = 19
    return pl.pallas_call(
        kern,
        grid=(batch // bb,),
        in_specs=[
            pl.BlockSpec((m, PROJ_MAIN), row_map),
            pl.BlockSpec((m, LANES), row_map),
            *state_specs,
            *_mixer_param_specs(layer),
            _const_spec((m, LANES)),
            _const_spec((m, LANES)),
            _const_spec((RET_HEADS, cl, LANES)),
            _const_spec((m, RET_QK_WIDTH)),
            _const_spec((m, RET_QK_WIDTH)),
            _const_spec((m, m)),
            *[_alias_spec() for _ in prev_states],
        ],
        out_specs=[pl.BlockSpec((m, SSD_WIDTH + RET_WIDTH), row_map), *state_specs],
        out_shape=[jax.ShapeDtypeStruct((batch * cl, SSD_WIDTH + RET_WIDTH), BF16), *_state_out_shapes(batch)],
        input_output_aliases={n_in + k: 1 + k for k in range(len(prev_states))},
        scratch_shapes=[pltpu.VMEM((bb, 8 + cl, CONV_DIM), F32)],
        compiler_params=_cparams(("parallel",)),
        name="mixer_sample",
    )(proj, dt_raw, conv_state, ssm_state, ret_state, *_mixer_params(pw), jnp.asarray(cos), jnp.asarray(sin),
      jnp.asarray(dmat), jnp.asarray(kdec), jnp.asarray(qdec), jnp.asarray(tri, dtype=BF16), *prev_states)


def _outproj_kernel(x_ref, y_ref, w_ref, o_ref):
    o_ref[...] = x_ref[...] + _dot(y_ref[...], w_ref[...])


def _outproj(x, y, layer, pw, tm):
    t = x.shape[0]
    k = y.shape[1]
    return pl.pallas_call(
        _outproj_kernel,
        grid=(t // tm,),
        in_specs=[
            pl.BlockSpec((tm, D_MODEL), lambda i: (i, 0)),
            pl.BlockSpec((tm, k), lambda i: (i, 0)),
            _layer_spec((k, D_MODEL), layer),
        ],
        out_specs=pl.BlockSpec((tm, D_MODEL), lambda i: (i, 0)),
        out_shape=jax.ShapeDtypeStruct((t, D_MODEL), F32),
        compiler_params=_cparams(("parallel",)),
        name="outproj",
    )(x, y, pw["w_out"])


def _memproj_kernel(m_ref, w_ref, k_ref, v_ref, kb_ref, vb_ref, *, bb):
    r = _dot(m_ref[...].reshape(bb * N_MEM, D_MODEL).astype(BF16), w_ref[...])
    kb_ref[...] = r[:, 0:D_MODEL].astype(BF16).reshape(bb, N_MEM, D_MODEL)
    vb_ref[...] = r[:, D_MODEL:2 * D_MODEL].astype(BF16).reshape(bb, N_MEM, D_MODEL)
    for i in range(bb):
        rows = slice(i * N_MEM, (i + 1) * N_MEM)
        for hd in range(XA_HEADS):
            k_ref[i, :, hd, :] = r[rows, hd * XA_HEAD_DIM:(hd + 1) * XA_HEAD_DIM]
            v_ref[i, :, hd, :] = r[rows, D_MODEL + hd * XA_HEAD_DIM:D_MODEL + (hd + 1) * XA_HEAD_DIM]


def _memproj(mem, pw, bb):
    b = mem.shape[0]
    cache_spec = pl.BlockSpec((None, bb, N_MEM, XA_HEADS, XA_HEAD_DIM), lambda l, i: (l, i, 0, 0, 0))
    cache_shape = jax.ShapeDtypeStruct((DEPTH, b, N_MEM, XA_HEADS, XA_HEAD_DIM), F32)
    flat_spec = pl.BlockSpec((None, bb, N_MEM, D_MODEL), lambda l, i: (l, i, 0, 0))
    flat_shape = jax.ShapeDtypeStruct((DEPTH, b, N_MEM, D_MODEL), BF16)
    return pl.pallas_call(
        functools.partial(_memproj_kernel, bb=bb),
        grid=(DEPTH, b // bb),
        in_specs=[
            pl.BlockSpec((bb, N_MEM, D_MODEL), lambda l, i: (i, 0, 0)),
            pl.BlockSpec((None, D_MODEL, 2 * D_MODEL), lambda l, i: (l, 0, 0)),
        ],
        out_specs=[cache_spec, cache_spec, flat_spec, flat_spec],
        out_shape=[cache_shape, cache_shape, flat_shape, flat_shape],
        compiler_params=_cparams(("parallel", "parallel")),
        name="memproj",
    )(mem, pw["w_kv"])


def _softmax_rows(s):
    e = jnp.exp(s - jnp.max(s, axis=-1, keepdims=True))
    return e / jnp.sum(e, axis=-1, keepdims=True)


def _xattn_prompt_kernel(x_ref, g_ref, wq_ref, wo_ref, k_ref, v_ref, o_ref):
    x = x_ref[0]
    h = (_rms(x) * g_ref[...]).astype(BF16)
    q = _dot(h, wq_ref[...])
    scale = XA_HEAD_DIM ** -0.5
    heads = []
    for hd in range(XA_HEADS):
        hs = slice(hd * XA_HEAD_DIM, (hd + 1) * XA_HEAD_DIM)
        p = _softmax_rows(_dot_nt(q[:, hs].astype(BF16), k_ref[0, :, hs]) * scale)
        heads.append(_dot(p.astype(BF16), v_ref[0, :, hs]))
    o = jnp.concatenate(heads, axis=-1).astype(BF16)
    o_ref[0] = x + _dot(o, wo_ref[...])


def _xattn_cache_kernel(x_ref, g_ref, wq_ref, wo_ref, k_ref, v_ref, o_ref, *, bb, tm):
    x = x_ref[...].reshape(bb * tm, D_MODEL)
    h = (_rms(x) * g_ref[...]).astype(BF16)
    q = _dot(h, wq_ref[...])
    scale = XA_HEAD_DIM ** -0.5
    rows_flat = XA_HEADS * tm
    assert tm & (tm - 1) == 0 and XA_HEADS & (XA_HEADS - 1) == 0
    row_head = lax.shift_right_logical(lax.broadcasted_iota(jnp.int32, (rows_flat, N_MEM * XA_HEADS), 0),
                                       int(math.log2(tm)))
    col_head = lax.broadcasted_iota(jnp.int32, (rows_flat, N_MEM * XA_HEADS), 1) & (XA_HEADS - 1)
    own_head = row_head == col_head
    outs = []
    for i in range(bb):
        qi = q[i * tm:(i + 1) * tm]
        qf = jnp.concatenate([qi[:, hd * XA_HEAD_DIM:(hd + 1) * XA_HEAD_DIM] for hd in range(XA_HEADS)], axis=0)
        k_all = k_ref[i].reshape(N_MEM * XA_HEADS, XA_HEAD_DIM).astype(BF16)
        v_all = v_ref[i].reshape(N_MEM * XA_HEADS, XA_HEAD_DIM).astype(BF16)
        s = jnp.where(own_head, _dot_nt(qf.astype(BF16), k_all) * scale, -jnp.inf)
        of = _dot(_softmax_rows(s).astype(BF16), v_all)
        outs.append(jnp.concatenate([of[hd * tm:(hd + 1) * tm] for hd in range(XA_HEADS)], axis=-1))
    o = jnp.concatenate(outs, axis=0).astype(BF16)
    o_ref[...] = (x + _dot(o, wo_ref[...])).reshape(bb, tm, D_MODEL)


def _xattn_weight_specs(layer):
    return [_layer_spec((1, D_MODEL), layer), _layer_spec((D_MODEL, D_MODEL), layer),
            _layer_spec((D_MODEL, D_MODEL), layer)]


def _xattn_prompt(x, layer, pw, mem_k, mem_v, tm):
    b, l, _ = x.shape
    mem_spec = pl.BlockSpec((None, 1, N_MEM, D_MODEL), lambda i, j: (layer, i, 0, 0))
    return pl.pallas_call(
        _xattn_prompt_kernel,
        grid=(b, l // tm),
        in_specs=[pl.BlockSpec((1, tm, D_MODEL), lambda i, j: (i, j, 0)), *_xattn_weight_specs(layer),
                  mem_spec, mem_spec],
        out_specs=pl.BlockSpec((1, tm, D_MODEL), lambda i, j: (i, j, 0)),
        out_shape=jax.ShapeDtypeStruct((b, l, D_MODEL), F32),
        compiler_params=_cparams(("parallel", "parallel")),
        name="xattn_prompt",
    )(x, pw["norm_mem"], pw["w_mq"], pw["w_mo"], mem_k, mem_v)


def _xattn_cache(x, layer, pw, mem_k, mem_v, bb):
    b, l, _ = x.shape
    kern = functools.partial(_xattn_cache_kernel, bb=bb, tm=l)
    mem_spec = pl.BlockSpec((None, bb, N_MEM, XA_HEADS, XA_HEAD_DIM), lambda i: (layer, i, 0, 0, 0))
    return pl.pallas_call(
        kern,
        grid=(b // bb,),
        in_specs=[pl.BlockSpec((bb, l, D_MODEL), lambda i: (i, 0, 0)), *_xattn_weight_specs(layer),
                  mem_spec, mem_spec],
        out_specs=pl.BlockSpec((bb, l, D_MODEL), lambda i: (i, 0, 0)),
        out_shape=jax.ShapeDtypeStruct((b, l, D_MODEL), F32),
        compiler_params=_cparams(("parallel",)),
        name="xattn_cache",
    )(x, pw["norm_mem"], pw["w_mq"], pw["w_mo"], mem_k, mem_v)


def _router_gates(logits):
    m = logits.shape[0]
    lane_i = lax.broadcasted_iota(jnp.int32, (m, LANES), 1)
    lane = lane_i.astype(F32)
    big = float(LANES)
    is_g = lane_i < N_EGROUPS
    gl = jnp.where(is_g, logits, -jnp.inf)
    gmax = jnp.max(gl, axis=-1, keepdims=True)
    g_idx = jnp.min(jnp.where(is_g & (gl == gmax), lane, big), axis=-1, keepdims=True)
    g_prob = 1.0 / jnp.sum(jnp.exp(gl - gmax), axis=-1, keepdims=True)
    e_lane = lane_i - ROUTER_OFF
    e_group = lax.shift_right_arithmetic(e_lane, int(math.log2(EXPERTS_PER_GROUP))).astype(F32)
    sel = (e_lane >= 0) & (e_lane < N_EXPERTS) & (e_group == g_idx)
    el = jnp.where(sel, logits, -jnp.inf)
    emax = jnp.max(el, axis=-1, keepdims=True)
    ee = jnp.exp(el - emax)
    e_prob = ee / jnp.sum(ee, axis=-1, keepdims=True)
    p1 = jnp.max(jnp.where(sel, e_prob, -1.0), axis=-1, keepdims=True)
    i1 = jnp.min(jnp.where(sel & (e_prob == p1), lane, big), axis=-1, keepdims=True)
    sel2 = sel & (lane != i1)
    p2 = jnp.max(jnp.where(sel2, e_prob, -1.0), axis=-1, keepdims=True)
    i2 = jnp.min(jnp.where(sel2 & (e_prob == p2), lane, big), axis=-1, keepdims=True)
    denom = p1 + p2
    w1 = g_prob * p1 / denom
    w2 = g_prob * p2 / denom
    return jnp.where(lane == i1, w1, 0.0) + jnp.where(lane == i2, w2, 0.0), g_idx


def _moe_kernel(x_ref, g_ref, wr_hi_ref, wr_lo_ref, br_ref, wg_ref, wu_ref, wd_ref, gf_ref, o_ref,
                h_scr, gate_scr, *, final_norm):
    e = pl.program_id(1)

    @pl.when(e == 0)
    def _():
        h = _rms(x_ref[...]) * g_ref[...]
        h_hi = h.astype(BF16)
        h_lo = (h - h_hi.astype(F32)).astype(BF16)
        logits = (_dot(h_hi, wr_hi_ref[...]) + _dot(h_hi, wr_lo_ref[...]) + _dot(h_lo, wr_hi_ref[...])
                  + br_ref[...])
        h_scr[...] = h_hi
        gate_scr[...] = _router_gates(logits)[0]
        o_ref[...] = jnp.zeros_like(o_ref)

    hb = h_scr[...]
    lane = lax.broadcasted_iota(jnp.int32, gate_scr.shape, 1)
    gate_e = jnp.sum(jnp.where(lane == e + ROUTER_OFF, gate_scr[...], 0.0), axis=-1, keepdims=True)
    a = _silu(_dot(hb, wg_ref[...])) * _dot(hb, wu_ref[...]) * gate_e
    o_ref[...] += _dot(a.astype(BF16), wd_ref[...])

    @pl.when(e == pl.num_programs(1) - 1)
    def _():
        y = x_ref[...] + o_ref[...]
        if final_norm:
            y = _rms(y) * gf_ref[...]
        o_ref[...] = y


def _moe(x, layer, pw, final_norm, tm):
    t = x.shape[0]
    kern = functools.partial(_moe_kernel, final_norm=final_norm)
    return pl.pallas_call(
        kern,
        grid=(t // tm, N_EXPERTS),
        in_specs=[
            pl.BlockSpec((tm, D_MODEL), lambda i, e: (i, 0)),
            _layer_spec((1, D_MODEL), layer),
            _layer_spec((D_MODEL, LANES), layer),
            _layer_spec((D_MODEL, LANES), layer),
            _layer_spec((1, LANES), layer),
            pl.BlockSpec((None, None, D_MODEL, EXPERT_FF), lambda i, e: (layer, e, 0, 0)),
            pl.BlockSpec((None, None, D_MODEL, EXPERT_FF), lambda i, e: (layer, e, 0, 0)),
            pl.BlockSpec((None, None, EXPERT_FF, D_MODEL), lambda i, e: (layer, e, 0, 0)),
            _const_spec((1, D_MODEL)),
        ],
        out_specs=pl.BlockSpec((tm, D_MODEL), lambda i, e: (i, 0)),
        out_shape=jax.ShapeDtypeStruct((t, D_MODEL), F32),
        scratch_shapes=[pltpu.VMEM((tm, D_MODEL), BF16), pltpu.VMEM((tm, LANES), F32)],
        compiler_params=_cparams(("parallel", "arbitrary")),
        name="moe",
    )(x, pw["norm_ffn"], pw["wr_hi"], pw["wr_lo"], pw["b_router"], pw["w_gate"], pw["w_up"], pw["w_down"],
      pw["norm_final"])


MOE_TS = 2048
MOE_M = 256
MOE_NB = MOE_TS // MOE_M + N_EGROUPS - 1
MOE_R = MOE_NB * MOE_M
SUBLANES = 8
TOKEN_TILE = (SUBLANES, D_MODEL // SUBLANES)
PLAN_BLK = 512


def _moe_route_kernel(x_ref, g_ref, wr_hi_ref, wr_lo_ref, br_ref, h_ref, info_ref):
    h = _rms(x_ref[...]) * g_ref[...]
    h_hi = h.astype(BF16)
    h_lo = (h - h_hi.astype(F32)).astype(BF16)
    logits = (_dot(h_hi, wr_hi_ref[...]) + _dot(h_hi, wr_lo_ref[...]) + _dot(h_lo, wr_hi_ref[...])
              + br_ref[...])
    gates, g_idx = _router_gates(logits)
    lane = lax.broadcasted_iota(jnp.int32, gates.shape, 1)
    info_ref[...] = jnp.where(lane == 0, g_idx, gates)
    h_ref[...] = h.reshape(h.shape[0], *TOKEN_TILE)


def _moe_route(x, layer, pw, tm):
    t = x.shape[0]
    return pl.pallas_call(
        _moe_route_kernel,
        grid=(t // tm,),
        in_specs=[
            pl.BlockSpec((tm, D_MODEL), lambda i: (i, 0)),
            _layer_spec((1, D_MODEL), layer),
            _layer_spec((D_MODEL, LANES), layer),
            _layer_spec((D_MODEL, LANES), layer),
            _layer_spec((1, LANES), layer),
        ],
        out_specs=[pl.BlockSpec((tm, *TOKEN_TILE), lambda i: (i, 0, 0)),
                   pl.BlockSpec((tm, LANES), lambda i: (i, 0))],
        out_shape=[jax.ShapeDtypeStruct((t, *TOKEN_TILE), F32), jax.ShapeDtypeStruct((t, LANES), F32)],
        compiler_params=_cparams(("parallel",)),
        name="moe_route",
    )(x, pw["norm_ffn"], pw["wr_hi"], pw["wr_lo"], pw["b_router"])


def _lane_pick(v, idx):
    lane = lax.broadcasted_iota(jnp.int32, v.shape, 1)
    return jnp.sum(jnp.where(lane == idx, v, 0.0), axis=-1, keepdims=True)


def _moe_plan_kernel(info_ref, dest_ref, table_ref, *, n_tiles):
    row = lax.broadcasted_iota(jnp.int32, (PLAN_BLK, PLAN_BLK), 0)
    col = lax.broadcasted_iota(jnp.int32, (PLAN_BLK, PLAN_BLK), 1)
    tri = jnp.where(row >= col, 1.0, 0.0).astype(BF16)
    row = lax.broadcasted_iota(jnp.int32, (LANES, LANES), 0)
    col = lax.broadcasted_iota(jnp.int32, (LANES, LANES), 1)
    before = jnp.where(row < col, 1.0, 0.0).astype(BF16)
    lane = lax.broadcasted_iota(jnp.int32, (PLAN_BLK, LANES), 1)
    lane_row = lax.broadcasted_iota(jnp.int32, (1, LANES), 1)
    for s in range(n_tiles):
        carry = jnp.zeros((1, LANES), F32)
        parts = []
        for blk in range(MOE_TS // PLAN_BLK):
            r0 = s * MOE_TS + blk * PLAN_BLK
            gid = info_ref[r0:r0 + PLAN_BLK, 0:1]
            onehot = jnp.where((lane < N_EGROUPS) & (lane.astype(F32) == gid), 1.0, 0.0)
            cum = _dot(tri, onehot.astype(BF16)) + carry
            carry = cum[PLAN_BLK - 1:PLAN_BLK, :]
            parts.append((onehot, cum))
        counts = carry
        padded = jnp.ceil(counts * (1.0 / MOE_M)) * MOE_M
        seg_start = _dot_sel(jnp.broadcast_to(padded, (8, LANES)), before)[0:1, :]
        rows = []
        for onehot, cum in parts:
            dest_col = jnp.sum(onehot * (seg_start + cum - 1.0), axis=-1, keepdims=True)
            for q in range(PLAN_BLK // LANES):
                piece = jnp.broadcast_to(dest_col[q * LANES:(q + 1) * LANES, :], (LANES, LANES))
                rows.append(piece.T[0:1, :])
        n_rows = MOE_TS // LANES
        dest_ref[s * n_rows:(s + 1) * n_rows, :] = jnp.concatenate(rows, axis=0).astype(jnp.int32)
        blk_start = lane_row.astype(F32) * MOE_M
        group_of_block = jnp.full((1, LANES), -1.0, F32)
        last_group = jnp.zeros((1, 1), F32)
        for g in range(N_EGROUPS):
            start_g = _lane_pick(seg_start, g)
            size_g = _lane_pick(padded, g)
            group_of_block = jnp.where((blk_start >= start_g) & (blk_start < start_g + size_g), float(g),
                                       group_of_block)
            last_group = jnp.where(size_g > 0, float(g), last_group)
        table = jnp.where(group_of_block >= 0, group_of_block, -1.0 - last_group)
        table_ref[s:s + 1, :] = table.astype(jnp.int32)


def _moe_plan(info, n_tiles):
    t = info.shape[0]
    return pl.pallas_call(
        functools.partial(_moe_plan_kernel, n_tiles=n_tiles),
        grid=(1,),
        in_specs=[pl.BlockSpec((t, LANES), lambda i: (0, 0))],
        out_specs=[pl.BlockSpec((t // LANES, LANES), lambda i: (0, 0)),
                   pl.BlockSpec((n_tiles, LANES), lambda i: (0, 0))],
        out_shape=[jax.ShapeDtypeStruct((t // LANES, LANES), jnp.int32),
                   jax.ShapeDtypeStruct((n_tiles, LANES), jnp.int32)],
        compiler_params=_cparams(("arbitrary",)),
        name="moe_plan",
    )(info)


def _dest_row(dest_ref, token):
    return dest_ref[token // LANES, token % LANES]


def _moe_permute_kernel(dest_ref, h_ref, info_ref, hs_ref, infos_ref):
    base = pl.program_id(0) * MOE_TS
    hs_ref[...] = jnp.zeros_like(hs_ref)
    infos_ref[...] = jnp.zeros_like(infos_ref)

    def body(t, carry):
        d = _dest_row(dest_ref, base + t)
        hs_ref[d] = h_ref[t]
        infos_ref[pl.ds(d, 1), :] = info_ref[pl.ds(t, 1), :]
        return carry

    lax.fori_loop(0, MOE_TS, body, 0, unroll=8)


def _moe_permute(dest, h, info, n_tiles):
    return pl.pallas_call(
        _moe_permute_kernel,
        grid_spec=pltpu.PrefetchScalarGridSpec(
            num_scalar_prefetch=1,
            grid=(n_tiles,),
            in_specs=[pl.BlockSpec((MOE_TS, *TOKEN_TILE), lambda s, d: (s, 0, 0)),
                      pl.BlockSpec((MOE_TS, LANES), lambda s, d: (s, 0))],
            out_specs=[pl.BlockSpec((MOE_R, *TOKEN_TILE), lambda s, d: (s, 0, 0)),
                       pl.BlockSpec((MOE_R, LANES), lambda s, d: (s, 0))],
        ),
        out_shape=[jax.ShapeDtypeStruct((n_tiles * MOE_R, *TOKEN_TILE), F32),
                   jax.ShapeDtypeStruct((n_tiles * MOE_R, LANES), F32)],
        compiler_params=_cparams(("arbitrary",)),
        name="moe_permute",
    )(dest, h, info)


def _moe_ffn_kernel(table_ref, x_ref, gates_ref, wg_ref, wu_ref, wd_ref, y_ref):
    group = table_ref[pl.program_id(0), pl.program_id(1)]

    @pl.when(group >= 0)
    def _():
        xb = x_ref[...].reshape(MOE_M, D_MODEL).astype(BF16)
        gates = gates_ref[...]
        lane = lax.broadcasted_iota(jnp.int32, gates.shape, 1)
        y = None
        for e in range(EXPERTS_PER_GROUP):
            gate_e = jnp.sum(jnp.where(lane == ROUTER_OFF + group * EXPERTS_PER_GROUP + e, gates, 0.0),
                             axis=-1, keepdims=True)
            a = _silu(_dot(xb, wg_ref[e])) * _dot(xb, wu_ref[e]) * gate_e
            d = _dot(a.astype(BF16), wd_ref[e])
            y = d if y is None else y + d
        y_ref[...] = y.reshape(MOE_M, *TOKEN_TILE)

    @pl.when(group < 0)
    def _():
        y_ref[...] = jnp.zeros_like(y_ref)


def _moe_ffn(table, hs, infos, layer, pw, n_tiles):
    def w_map(s, j, tbl):
        g = tbl[s, j]
        return (layer, jnp.where(g >= 0, g, -1 - g), 0, 0, 0)

    def grouped(w):
        return w.reshape(DEPTH, N_EGROUPS, EXPERTS_PER_GROUP, *w.shape[2:])

    tile_spec = pl.BlockSpec((MOE_M, *TOKEN_TILE), lambda s, j, tbl: (s * MOE_NB + j, 0, 0))
    return pl.pallas_call(
        _moe_ffn_kernel,
        grid_spec=pltpu.PrefetchScalarGridSpec(
            num_scalar_prefetch=1,
            grid=(n_tiles, MOE_NB),
            in_specs=[
                tile_spec,
                pl.BlockSpec((MOE_M, LANES), lambda s, j, tbl: (s * MOE_NB + j, 0)),
                pl.BlockSpec((None, None, EXPERTS_PER_GROUP, D_MODEL, EXPERT_FF), w_map),
                pl.BlockSpec((None, None, EXPERTS_PER_GROUP, D_MODEL, EXPERT_FF), w_map),
                pl.BlockSpec((None, None, EXPERTS_PER_GROUP, EXPERT_FF, D_MODEL), w_map),
            ],
            out_specs=tile_spec,
        ),
        out_shape=jax.ShapeDtypeStruct((n_tiles * MOE_R, *TOKEN_TILE), F32),
        compiler_params=_cparams(("arbitrary", "arbitrary")),
        name="moe_ffn",
    )(table, hs, infos, grouped(pw["w_gate"]), grouped(pw["w_up"]), grouped(pw["w_down"]))


def _moe_unpermute_kernel(dest_ref, x_ref, ys_ref, gf_ref, o_ref, y_scr, *, tm, final_norm):
    base = pl.program_id(0) * MOE_TS + pl.program_id(1) * tm

    def body(t, carry):
        y_scr[t] = ys_ref[_dest_row(dest_ref, base + t)]
        return carry

    lax.fori_loop(0, tm, body, 0, unroll=8)
    y = x_ref[...] + y_scr[...].reshape(tm, D_MODEL)
    if final_norm:
        y = _rms(y) * gf_ref[...]
    o_ref[...] = y


def _moe_unpermute(dest, x, ys, pw, final_norm, n_tiles, tm):
    per_tile = MOE_TS // tm
    return pl.pallas_call(
        functools.partial(_moe_unpermute_kernel, tm=tm, final_norm=final_norm),
        grid_spec=pltpu.PrefetchScalarGridSpec(
            num_scalar_prefetch=1,
            grid=(n_tiles, per_tile),
            in_specs=[
                pl.BlockSpec((tm, D_MODEL), lambda s, i, d: (s * per_tile + i, 0)),
                pl.BlockSpec((MOE_R, *TOKEN_TILE), lambda s, i, d: (s, 0, 0)),
                pl.BlockSpec((1, D_MODEL), lambda s, i, d: (0, 0), pipeline_mode=pl.Buffered(1)),
            ],
            out_specs=pl.BlockSpec((tm, D_MODEL), lambda s, i, d: (s * per_tile + i, 0)),
            scratch_shapes=[pltpu.VMEM((tm, *TOKEN_TILE), F32)],
        ),
        out_shape=jax.ShapeDtypeStruct(x.shape, F32),
        compiler_params=_cparams(("arbitrary", "arbitrary")),
        name="moe_unpermute",
    )(dest, x, ys, pw["norm_final"])


def _moe_sorted(x, layer, pw, final_norm):
    n_tiles = x.shape[0] // MOE_TS
    h, info = _moe_route(x, layer, pw, 1024)
    dest, table = _moe_plan(info, n_tiles)
    hs, infos = _moe_permute(dest, h, info, n_tiles)
    ys = _moe_ffn(table, hs, infos, layer, pw, n_tiles)
    return _moe_unpermute(dest, x, ys, pw, final_norm, n_tiles, 1024)


def _row(v):
    return v.reshape(v.shape[0], 1, v.shape[1])


def _pad_lanes(v):
    return jnp.pad(v, ((0, 0),) * (v.ndim - 1) + ((0, LANES - v.shape[-1]),))


def _prep_weights(norm_mix, w_in, conv_w, conv_b, dt_bias, a_log, d_skip, ssd_gain, ret_gain, w_out,
                  norm_mem, w_mq, w_mk, w_mv, w_mo, norm_ffn, w_rg, b_rg, w_re, b_re, w_gate, w_up, w_down,
                  norm_final):
    dt_off = SSD_WIDTH + CONV_DIM
    w_main = jnp.concatenate([w_in[:, :, :dt_off], w_in[:, :, dt_off + SSD_HEADS:]], axis=2).astype(BF16)
    w_dt = _pad_lanes(w_in[:, :, dt_off:dt_off + SSD_HEADS]).astype(BF16)
    w_router = _pad_lanes(jnp.concatenate([w_rg, w_re], axis=2))
    wr_hi = w_router.astype(BF16)
    wr_lo = (w_router - wr_hi.astype(F32)).astype(BF16)
    return dict(
        norm_mix=_row(norm_mix), w_main=w_main, w_dt=w_dt,
        conv_w=conv_w, conv_b=_row(conv_b),
        dt_bias=_row(_pad_lanes(dt_bias)), a_log=_row(_pad_lanes(a_log)),
        expand=jnp.asarray(_expand_matrix(), dtype=BF16),
        d_skip=_row(jnp.repeat(d_skip, SSD_HEAD_DIM, axis=1)),
        ssd_gain=_row(ssd_gain), ret_gain=_row(ret_gain),
        w_out=w_out.astype(BF16),
        norm_mem=_row(norm_mem), w_mq=w_mq.astype(BF16), w_mo=w_mo.astype(BF16),
        w_kv=jnp.concatenate([w_mk, w_mv], axis=2).astype(BF16),
        norm_ffn=_row(norm_ffn), wr_hi=wr_hi, wr_lo=wr_lo,
        b_router=_row(_pad_lanes(jnp.concatenate([b_rg, b_re], axis=1))),
        w_gate=w_gate.astype(BF16), w_up=w_up.astype(BF16), w_down=w_down.astype(BF16),
        norm_final=norm_final.reshape(1, -1),
    )


def _token_tile(t, cap):
    tm = min(t, cap)
    assert t % tm == 0
    return tm


def _trunk(x, mixer_fn, xattn_fn, pw, proj_dtype):
    b, l, _ = x.shape
    t = b * l
    xf = x.reshape(t, D_MODEL)
    states = ()
    for layer in range(DEPTH):
        proj, dt_raw = _inproj(xf, layer, pw, proj_dtype, _token_tile(t, 512))
        y, *states = mixer_fn(layer, proj, dt_raw, tuple(states))
        xf = _outproj(xf, y, layer, pw, _token_tile(t, 512))
        xf = xattn_fn(layer, xf.reshape(b, l, D_MODEL)).reshape(t, D_MODEL)
        if t % MOE_TS == 0:
            xf = _moe_sorted(xf, layer, pw, layer == DEPTH - 1)
        else:
            xf = _moe(xf, layer, pw, layer == DEPTH - 1, _token_tile(t, 1024))
    conv, ssm, ret = states
    return (xf.reshape(b, l, D_MODEL), ssm.reshape(DEPTH, b, SSD_HEADS, SSD_HEAD_DIM, SSD_STATE), conv, ret)


def kernel(x_prompt, x_sample, mem_prompt, state_ssm, state_conv, state_ret, cache_mem_k, cache_mem_v,
           norm_mix, w_in, conv_w, conv_b, dt_bias, a_log, d_skip, ssd_gain, ret_gain, w_out,
           norm_mem, w_mq, w_mk, w_mv, w_mo, norm_ffn, w_rg, b_rg, w_re, b_re, w_gate, w_up, w_down,
           norm_final):
    pw = _prep_weights(norm_mix, w_in, conv_w, conv_b, dt_bias, a_log, d_skip, ssd_gain, ret_gain, w_out,
                       norm_mem, w_mq, w_mk, w_mv, w_mo, norm_ffn, w_rg, b_rg, w_re, b_re, w_gate, w_up, w_down,
                       norm_final)
    bp, lp, _ = x_prompt.shape
    bs, ls, _ = x_sample.shape
    n_mem = mem_prompt.shape[1]

    assert n_mem == N_MEM
    mem_k_p, mem_v_p, mem_k_rows, mem_v_rows = _memproj(mem_prompt, pw, _token_tile(bp, 2))

    def mixer_p(layer, proj, dt_raw, prev_states):
        return _mixer_prompt(layer, proj, dt_raw, pw, prev_states, bp, lp)

    def xattn_p(layer, x):
        return _xattn_prompt(x, layer, pw, mem_k_rows, mem_v_rows, _token_tile(lp, 512))

    y_prompt, ssm_p, conv_p, ret_p = _trunk(x_prompt, mixer_p, xattn_p, pw, BF16)

    sample_bb = 4
    ssm_in = state_ssm.reshape(DEPTH, bs, SSD_WIDTH, SSD_STATE)

    def mixer_s(layer, proj, dt_raw, prev_states):
        return _mixer_sample(layer, proj, dt_raw, state_conv, ssm_in, state_ret, pw, prev_states, bs, ls,
                             sample_bb)

    def xattn_s(layer, x):
        return _xattn_cache(x, layer, pw, cache_mem_k, cache_mem_v, sample_bb)

    y_sample, ssm_s, conv_s, ret_s = _trunk(x_sample, mixer_s, xattn_s, pw, F32)
    return (y_prompt, y_sample, ssm_p, conv_p, ret_p, mem_k_p, mem_v_p, ssm_s, conv_s, ret_s)
```

```python
import functools
import math

import numpy as np
import jax
import jax.numpy as jnp
from jax import lax
from jax.experimental import pallas as pl
from jax.experimental.pallas import tpu as pltpu

F32 = jnp.float32
BF16 = jnp.bfloat16

D_MODEL = 1024
DEPTH = 2
PAST_LEN = 16384
SSD_HEAD_DIM = 64
SSD_HEADS = 16
SSD_GROUPS = 2
SSD_STATE = 128
SSD_WIDTH = 1024
GROUP_WIDTH = SSD_WIDTH // SSD_GROUPS
CONV_WIDTH = 4
CONV_DIM = SSD_WIDTH + 2 * SSD_GROUPS * SSD_STATE
RET_HEADS = 4
RET_V_DIM = 256
RET_QK_DIM = 128
RET_WIDTH = 1024
RET_QK_WIDTH = RET_HEADS * RET_QK_DIM
ROPE_BASE = 10000.0
N_MEM = 256
XA_HEADS = 4
XA_HEAD_DIM = 256
N_EGROUPS = 4
EXPERTS_PER_GROUP = 4
N_EXPERTS = 16
EXPERT_FF = 512
RMS_EPS = 1e-6

LANES = 128
CHUNK = 128
OFF_Z = 0
OFF_XBC = OFF_Z + SSD_WIDTH
OFF_Q = OFF_XBC + CONV_DIM
OFF_K = OFF_Q + RET_QK_WIDTH
OFF_V = OFF_K + RET_QK_WIDTH
OFF_G = OFF_V + RET_WIDTH
PROJ_MAIN = OFF_G + RET_WIDTH
INPROJ_TN = 512
CONV_TAIL = 16
CONV_K = 256
ROUTER_OFF = N_EGROUPS

VMEM_LIMIT = 56 * 1024 * 1024


def _cparams(sem):
    return pltpu.CompilerParams(dimension_semantics=sem, vmem_limit_bytes=VMEM_LIMIT)


def _const_spec(shape):
    nd = len(shape)
    return pl.BlockSpec(shape, lambda *_: (0,) * nd, pipeline_mode=pl.Buffered(1))


def _layer_spec(shape, layer):
    nd = len(shape)
    return pl.BlockSpec((None,) + tuple(shape), lambda *_: (layer,) + (0,) * nd, pipeline_mode=pl.Buffered(1))


def _alias_spec():
    return pl.BlockSpec(memory_space=pl.ANY)


def _rms(x):
    return x * lax.rsqrt(jnp.mean(x * x, axis=-1, keepdims=True) + RMS_EPS)


def _silu(x):
    return x * jax.nn.sigmoid(x)


def _softplus(x):
    return jnp.maximum(x, 0.0) + jnp.log1p(jnp.exp(-jnp.abs(x)))


def _split3(x):
    hi = x.astype(BF16)
    r = x - hi.astype(F32)
    mid = r.astype(BF16)
    lo = (r - mid.astype(F32)).astype(BF16)
    return hi, mid, lo


def _dot(a, b):
    return jnp.dot(a, b, preferred_element_type=F32)


def _dot_nt(a, b):
    return lax.dot_general(a, b, (((1,), (1,)), ((), ())), preferred_element_type=F32)


def _dot_sel(x, sel):
    hi, mid, lo = _split3(x)
    return _dot(hi, sel) + _dot(mid, sel) + _dot(lo, sel)


def _sel_dot(sel, x):
    hi, mid, lo = _split3(x)
    return _dot(sel, hi) + _dot(sel, mid) + _dot(sel, lo)


def _inproj_kernel(x_ref, g_ref, w_ref, wdt_ref, o_ref, odt_ref):
    h = (_rms(x_ref[...]) * g_ref[...]).astype(BF16)
    for j in range(PROJ_MAIN // INPROJ_TN):
        sl = slice(j * INPROJ_TN, (j + 1) * INPROJ_TN)
        o_ref[:, sl] = _dot(h, w_ref[:, sl]).astype(o_ref.dtype)
    odt_ref[...] = _dot(h, wdt_ref[...])


def _inproj(x, layer, pw, out_dtype, tm):
    t = x.shape[0]
    return pl.pallas_call(
        _inproj_kernel,
        grid=(t // tm,),
        in_specs=[
            pl.BlockSpec((tm, D_MODEL), lambda i: (i, 0)),
            _layer_spec((1, D_MODEL), layer),
            _layer_spec((D_MODEL, PROJ_MAIN), layer),
            _layer_spec((D_MODEL, LANES), layer),
        ],
        out_specs=[
            pl.BlockSpec((tm, PROJ_MAIN), lambda i: (i, 0)),
            pl.BlockSpec((tm, LANES), lambda i: (i, 0)),
        ],
        out_shape=[
            jax.ShapeDtypeStruct((t, PROJ_MAIN), out_dtype),
            jax.ShapeDtypeStruct((t, LANES), F32),
        ],
        compiler_params=_cparams(("parallel",)),
        name="inproj",
    )(x, pw["norm_mix"], pw["w_main"], pw["w_dt"])


def _dt_terms(dt_raw, dtb, alog, tri, expand):
    dt = _softplus(dt_raw + dtb)
    a = dt * (-jnp.exp(alog))
    acum = _sel_dot(tri, a)
    return dt, acum


def _ssd_out(y, xs, z, dskip, gain):
    y = (y + dskip * xs) * _silu(z)
    parts = []
    for g in range(SSD_GROUPS):
        parts.append(_rms(y[:, g * GROUP_WIDTH:(g + 1) * GROUP_WIDTH]))
    return jnp.concatenate(parts, axis=-1) * gain


def _rotary(x, cos, sin_signed):
    parts = []
    for h in range(RET_HEADS):
        xh = x[:, h * RET_QK_DIM:(h + 1) * RET_QK_DIM]
        parts.append(xh * cos + pltpu.roll(xh, RET_QK_DIM // 2, axis=1) * sin_signed)
    return parts


def _ret_out(o_heads, g, gain):
    o = jnp.concatenate([_rms(o) for o in o_heads], axis=-1)
    return o * gain * _silu(g)


def _mixer_prompt_kernel(proj_ref, dt_ref, convw_ref, convb_ref, dtb_ref, alog_ref, exp_ref, dskip_ref,
                         sgain_ref, rgain_ref, cos_ref, sin_ref, dmat_ref, kdec_ref, qdec_ref, shift_ref,
                         *rest, ret_chunk_decay):
    y_ref, conv_ref, ssm_ref, ret_ref, xp_scr, st_scr = rest[-6:]
    cl = CHUNK
    c = pl.program_id(1)
    nc = pl.num_programs(1)

    @pl.when(c == 0)
    def _():
        xp_scr[...] = jnp.zeros_like(xp_scr)
        st_scr[...] = jnp.zeros_like(st_scr)
        ret_ref[...] = jnp.zeros_like(ret_ref)

    xbc_b = proj_ref[:, OFF_XBC:OFF_XBC + CONV_DIM]
    x_ext = jnp.concatenate(
        [xp_scr[c % 2], xbc_b, jnp.zeros((CONV_K - CONV_TAIL - cl, CONV_DIM), xbc_b.dtype)], axis=0)
    taps = _dot(shift_ref[...], x_ext)
    xbc_raw = xbc_b.astype(F32)
    acc = convb_ref[...] + xbc_raw * convw_ref[CONV_WIDTH - 1:CONV_WIDTH, :]
    for j in range(CONV_WIDTH - 1):
        acc = acc + taps[j * cl:(j + 1) * cl, :] * convw_ref[j:j + 1, :]
    conv_ref[0] = xbc_raw[cl - (CONV_WIDTH - 1):cl, :]
    xp_scr[(c + 1) % 2] = xbc_b[cl - CONV_TAIL:cl, :]
    xbc = _silu(acc)
    xs = xbc[:, 0:SSD_WIDTH]
    bm = xbc[:, SSD_WIDTH:SSD_WIDTH + SSD_GROUPS * SSD_STATE]
    cm = xbc[:, SSD_WIDTH + SSD_GROUPS * SSD_STATE:CONV_DIM]

    row = lax.broadcasted_iota(jnp.int32, (cl, cl), 0)
    col = lax.broadcasted_iota(jnp.int32, (cl, cl), 1)
    causal = row >= col
    tri = jnp.where(causal, 1.0, 0.0).astype(BF16)
    expand = exp_ref[...]
    dt, acum = _dt_terms(dt_ref[...], dtb_ref[...], alog_ref[...], tri, expand)
    acum_t = acum.T
    eacum = jnp.exp(acum)
    dt_x = _dot_sel(dt, expand)
    eacum_x = _dot_sel(eacum, expand)
    dte_x = _dot_sel(jnp.exp(acum[cl - 1:cl, :] - acum), expand)
    xdt = xs * dt_x
    xdt_b = xdt.astype(BF16)
    xdtd_b = (xdt * dte_x).astype(BF16)
    lane = lax.broadcasted_iota(jnp.int32, (cl, LANES), 1)

    y_parts = []
    upd_parts = []
    for g in range(SSD_GROUPS):
        bg = bm[:, g * SSD_STATE:(g + 1) * SSD_STATE]
        cg_b = cm[:, g * SSD_STATE:(g + 1) * SSD_STATE].astype(BF16)
        bg_b = bg.astype(BF16)
        cb = _dot_nt(cg_b, bg_b)
        gsl = slice(g * GROUP_WIDTH, (g + 1) * GROUP_WIDTH)
        y_off = _dot(cg_b, st_scr[:, gsl].astype(BF16)) * eacum_x[:, gsl]
        upd_parts.append(_dot(bg.T.astype(BF16), xdtd_b[:, gsl]))
        for j in range(GROUP_WIDTH // LANES):
            h0 = g * (SSD_HEADS // SSD_GROUPS) + 2 * j
            psl = slice(h0 * SSD_HEAD_DIM, (h0 + 2) * SSD_HEAD_DIM)
            res = []
            for hh in (h0, h0 + 1):
                seg = acum[:, hh:hh + 1] - acum_t[hh:hh + 1, :]
                lmat = jnp.exp(jnp.where(causal, seg, -jnp.inf))
                res.append(_dot((cb * lmat).astype(BF16), xdt_b[:, psl]))
            y_parts.append(jnp.where(lane < SSD_HEAD_DIM, res[0], res[1]) + y_off[:, j * LANES:(j + 1) * LANES])
    y = jnp.concatenate(y_parts, axis=-1)
    st_new = st_scr[...] * eacum_x[cl - 1:cl, :] + jnp.concatenate(upd_parts, axis=-1)
    st_scr[...] = st_new

    @pl.when(c == nc - 1)
    def _():
        ssm_ref[0] = st_new.T

    z = proj_ref[:, OFF_Z:OFF_Z + SSD_WIDTH].astype(F32)
    y_ref[:, 0:SSD_WIDTH] = _ssd_out(y, xs, z, dskip_ref[...], sgain_ref[...]).astype(y_ref.dtype)

    cos = cos_ref[...]
    sin = sin_ref[...]
    q_heads = _rotary(proj_ref[:, OFF_Q:OFF_Q + RET_QK_WIDTH].astype(F32), cos, sin)
    k_heads = _rotary(proj_ref[:, OFF_K:OFF_K + RET_QK_WIDTH].astype(F32), cos, sin)
    o_heads = []
    for h in range(RET_HEADS):
        qh = q_heads[h]
        kh = k_heads[h] * (RET_QK_DIM ** -0.5)
        vh = proj_ref[:, OFF_V + h * RET_V_DIM:OFF_V + (h + 1) * RET_V_DIM]
        hs = slice(h * RET_QK_DIM, (h + 1) * RET_QK_DIM)
        scores = _dot_nt(qh.astype(BF16), kh.astype(BF16)) * dmat_ref[h]
        o_intra = _dot(scores.astype(BF16), vh)
        s_in = ret_ref[0, h]
        o_cross = _dot((qh * qdec_ref[:, hs]).astype(BF16), s_in.astype(BF16))
        kv = _dot((kh * kdec_ref[:, hs]).T.astype(BF16), vh)
        ret_ref[0, h] = ret_chunk_decay[h] * s_in + kv
        o_heads.append(o_intra + o_cross)
    gate = proj_ref[:, OFF_G:OFF_G + RET_WIDTH].astype(F32)
    y_ref[:, SSD_WIDTH:SSD_WIDTH + RET_WIDTH] = _ret_out(o_heads, gate, rgain_ref[...]).astype(y_ref.dtype)


def _ret_tables(cl, reps):
    lg = np.log(1.0 - np.exp2(-5.0 - np.arange(RET_HEADS, dtype=np.float64)))
    idx = np.arange(cl, dtype=np.float64)
    rel = idx[:, None] - idx[None, :]
    dmat = np.where(rel[None] >= 0, np.exp(rel[None] * lg[:, None, None]), 0.0).astype(np.float32)
    kdec = np.exp((cl - 1 - idx)[:, None] * lg[None, :]).astype(np.float32)
    qdec = np.exp((idx + 1.0)[:, None] * lg[None, :]).astype(np.float32)
    kdec = np.tile(np.repeat(kdec, RET_QK_DIM, axis=1), (reps, 1))
    qdec = np.tile(np.repeat(qdec, RET_QK_DIM, axis=1), (reps, 1))
    chunk_decay = [float(v) for v in np.exp(cl * lg).astype(np.float32)]
    return dmat, kdec, qdec, chunk_decay


def _rope_tables(pos0, length, reps):
    half = RET_QK_DIM // 2
    inv = ROPE_BASE ** (-np.arange(half, dtype=np.float64) / half)
    pos = (pos0 + np.arange(length)).astype(np.float64)
    ang = pos[:, None] * inv[None, :]
    cos = np.cos(ang).astype(np.float32)
    sin = np.sin(ang).astype(np.float32)
    cos2 = np.tile(np.concatenate([cos, cos], axis=1), (reps, 1))
    sin2 = np.tile(np.concatenate([-sin, sin], axis=1), (reps, 1))
    return cos2, sin2


def _conv_shift_matrix():
    s = np.zeros(((CONV_WIDTH - 1) * CHUNK, CONV_K), np.float32)
    for j in range(CONV_WIDTH - 1):
        for t in range(CHUNK):
            s[j * CHUNK + t, CONV_TAIL + t - (CONV_WIDTH - 1) + j] = 1.0
    return s


def _expand_matrix():
    e = np.zeros((LANES, SSD_WIDTH), np.float32)
    for h in range(SSD_HEADS):
        e[h, h * SSD_HEAD_DIM:(h + 1) * SSD_HEAD_DIM] = 1.0
    return e


def _mixer_param_specs(layer):
    return [
        _layer_spec((CONV_WIDTH, CONV_DIM), layer),
        _layer_spec((1, CONV_DIM), layer),
        _layer_spec((1, LANES), layer),
        _layer_spec((1, LANES), layer),
        _const_spec((LANES, SSD_WIDTH)),
        _layer_spec((1, SSD_WIDTH), layer),
        _layer_spec((1, SSD_WIDTH), layer),
        _layer_spec((1, RET_WIDTH), layer),
    ]


def _mixer_params(pw):
    return (pw["conv_w"], pw["conv_b"], pw["dt_bias"], pw["a_log"], pw["expand"], pw["d_skip"],
            pw["ssd_gain"], pw["ret_gain"])


def _state_out_shapes(batch):
    return [
        jax.ShapeDtypeStruct((DEPTH, batch, CONV_WIDTH - 1, CONV_DIM), F32),
        jax.ShapeDtypeStruct((DEPTH, batch, SSD_WIDTH, SSD_STATE), F32),
        jax.ShapeDtypeStruct((DEPTH, batch, RET_HEADS, RET_QK_DIM, RET_V_DIM), F32),
    ]


def _mixer_prompt(layer, proj, dt_raw, pw, prev_states, batch, seq):
    nc = seq // CHUNK
    dmat, kdec, qdec, chunk_decay = _ret_tables(CHUNK, 1)
    cos, sin = _rope_tables(0, seq, 1)
    t = batch * seq
    row_map = lambda b, c: (b * nc + c, 0)
    kern = functools.partial(_mixer_prompt_kernel, ret_chunk_decay=chunk_decay)
    n_in = 16
    return pl.pallas_call(
        kern,
        grid=(batch, nc),
        in_specs=[
            pl.BlockSpec((CHUNK, PROJ_MAIN), row_map),
            pl.BlockSpec((CHUNK, LANES), row_map),
            *_mixer_param_specs(layer),
            pl.BlockSpec((CHUNK, LANES), lambda b, c: (c, 0)),
            pl.BlockSpec((CHUNK, LANES), lambda b, c: (c, 0)),
            _const_spec((RET_HEADS, CHUNK, CHUNK)),
            _const_spec((CHUNK, RET_QK_WIDTH)),
            _const_spec((CHUNK, RET_QK_WIDTH)),
            _const_spec(((CONV_WIDTH - 1) * CHUNK, CONV_K)),
            *[_alias_spec() for _ in prev_states],
        ],
        out_specs=[
            pl.BlockSpec((CHUNK, SSD_WIDTH + RET_WIDTH), row_map),
            pl.BlockSpec((None, 1, CONV_WIDTH - 1, CONV_DIM), lambda b, c: (layer, b, 0, 0)),
            pl.BlockSpec((None, 1, SSD_WIDTH, SSD_STATE), lambda b, c: (layer, b, 0, 0)),
            pl.BlockSpec((None, 1, RET_HEADS, RET_QK_DIM, RET_V_DIM), lambda b, c: (layer, b, 0, 0, 0)),
        ],
        out_shape=[jax.ShapeDtypeStruct((t, SSD_WIDTH + RET_WIDTH), BF16), *_state_out_shapes(batch)],
        input_output_aliases={n_in + k: 1 + k for k in range(len(prev_states))},
        scratch_shapes=[
            pltpu.VMEM((2, CONV_TAIL, CONV_DIM), BF16),
            pltpu.VMEM((SSD_STATE, SSD_WIDTH), F32),
        ],
        compiler_params=_cparams(("parallel", "arbitrary")),
        name="mixer_prompt",
    )(proj, dt_raw, *_mixer_params(pw), jnp.asarray(cos), jnp.asarray(sin), jnp.asarray(dmat),
      jnp.asarray(kdec), jnp.asarray(qdec), jnp.asarray(_conv_shift_matrix(), dtype=BF16), *prev_states)


def _pad_rows(x, rows):
    return jnp.concatenate([x, jnp.zeros((rows - x.shape[0], x.shape[1]), x.dtype)], axis=0)


def _mixer_sample_kernel(proj_ref, dt_ref, convs_ref, ssm_in_ref, ret_in_ref, convw_ref, convb_ref, dtb_ref,
                         alog_ref, exp_ref, dskip_ref, sgain_ref, rgain_ref, cos_ref, sin_ref, dmat_ref,
                         kdec_ref, qdec_ref, tri_ref, *rest, bb, cl, ret_chunk_decay):
    y_ref, conv_ref, ssm_ref, ret_ref, xp_scr = rest[-5:]
    for i in range(bb):
        xp_scr[i, 8 - (CONV_WIDTH - 1):8, :] = convs_ref[i]
        xp_scr[i, 8:8 + cl, :] = proj_ref[i * cl:(i + 1) * cl, OFF_XBC:OFF_XBC + CONV_DIM]
        conv_ref[i] = xp_scr[i, 8 + cl - (CONV_WIDTH - 1):8 + cl, :]
    acc = None
    for j in range(CONV_WIDTH):
        s = 8 - (CONV_WIDTH - 1) + j
        tap = jnp.concatenate([xp_scr[i, s:s + cl, :] for i in range(bb)], axis=0) * convw_ref[j:j + 1, :]
        acc = convb_ref[...] + tap if acc is None else acc + tap
    xbc = _silu(acc)
    xs = xbc[:, 0:SSD_WIDTH]
    bm = xbc[:, SSD_WIDTH:SSD_WIDTH + SSD_GROUPS * SSD_STATE]
    cm = xbc[:, SSD_WIDTH + SSD_GROUPS * SSD_STATE:CONV_DIM]

    expand = exp_ref[...]
    dt, acum = _dt_terms(dt_ref[...], dtb_ref[...], alog_ref[...], tri_ref[...], expand)
    eacum = jnp.exp(acum)
    alast = jnp.concatenate(
        [jnp.broadcast_to(acum[(i + 1) * cl - 1:(i + 1) * cl, :], (cl, LANES)) for i in range(bb)], axis=0)
    dt_x = _dot_sel(dt, expand)
    eacum_x = _dot_sel(eacum, expand)
    dte_x = _dot_sel(jnp.exp(alast - acum), expand)
    xdt = xs * dt_x
    xdtd = xdt * dte_x
    row = lax.broadcasted_iota(jnp.int32, (cl, LANES), 0)
    col = lax.broadcasted_iota(jnp.int32, (cl, LANES), 1)
    causal = row >= col
    lane = col

    y_rows = []
    for i in range(bb):
        rs = slice(i * cl, (i + 1) * cl)
        acum_i = acum[rs]
        acum_t = _pad_rows(acum_i, LANES).T
        xdt_p = _pad_rows(xdt[rs], LANES).astype(BF16)
        xdtd_p = _pad_rows(xdtd[rs], LANES)
        chunk_decay = eacum[(i + 1) * cl - 1:(i + 1) * cl, :]
        y_parts = []
        for g in range(SSD_GROUPS):
            bg_p = _pad_rows(bm[rs, g * SSD_STATE:(g + 1) * SSD_STATE], LANES)
            cg_b = cm[rs, g * SSD_STATE:(g + 1) * SSD_STATE].astype(BF16)
            cb = _dot_nt(cg_b, bg_p.astype(BF16))
            gsl = slice(g * GROUP_WIDTH, (g + 1) * GROUP_WIDTH)
            st_g = ssm_in_ref[i, gsl, :]
            y_off = _dot_nt(cg_b, st_g.astype(BF16)) * eacum_x[rs, gsl]
            upd = _dot(xdtd_p[:, gsl].T.astype(BF16), bg_p.astype(BF16))
            for hh in range(SSD_HEADS // SSD_GROUPS):
                h = g * (SSD_HEADS // SSD_GROUPS) + hh
                hsl = slice(hh * SSD_HEAD_DIM, (hh + 1) * SSD_HEAD_DIM)
                ssm_ref[i, h * SSD_HEAD_DIM:(h + 1) * SSD_HEAD_DIM, :] = (
                    st_g[hsl, :] * chunk_decay[:, h:h + 1] + upd[hsl, :])
            for j in range(GROUP_WIDTH // LANES):
                h0 = g * (SSD_HEADS // SSD_GROUPS) + 2 * j
                psl = slice(h0 * SSD_HEAD_DIM, (h0 + 2) * SSD_HEAD_DIM)
                res = []
                for hh in (h0, h0 + 1):
                    seg = acum_i[:, hh:hh + 1] - acum_t[hh:hh + 1, :]
                    lmat = jnp.exp(jnp.where(causal, seg, -jnp.inf))
                    res.append(_dot((cb * lmat).astype(BF16), xdt_p[:, psl]))
                y_parts.append(jnp.where(lane < SSD_HEAD_DIM, res[0], res[1]) + y_off[:, j * LANES:(j + 1) * LANES])
        y_rows.append(jnp.concatenate(y_parts, axis=-1))
    y = jnp.concatenate(y_rows, axis=0)
    z = proj_ref[:, OFF_Z:OFF_Z + SSD_WIDTH]
    y_ref[:, 0:SSD_WIDTH] = _ssd_out(y, xs, z, dskip_ref[...], sgain_ref[...]).astype(y_ref.dtype)

    cos = cos_ref[...]
    sin = sin_ref[...]
    q_heads = _rotary(proj_ref[:, OFF_Q:OFF_Q + RET_QK_WIDTH], cos, sin)
    k_heads = _rotary(proj_ref[:, OFF_K:OFF_K + RET_QK_WIDTH], cos, sin)
    o_heads = []
    for h in range(RET_HEADS):
        hs = slice(h * RET_QK_DIM, (h + 1) * RET_QK_DIM)
        kh_all = k_heads[h] * (RET_QK_DIM ** -0.5)
        q_start = q_heads[h] * qdec_ref[:, hs]
        k_end = kh_all * kdec_ref[:, hs]
        o_rows = []
        for i in range(bb):
            rs = slice(i * cl, (i + 1) * cl)
            vh_p = _pad_rows(proj_ref[rs, OFF_V + h * RET_V_DIM:OFF_V + (h + 1) * RET_V_DIM], LANES).astype(BF16)
            kh_p = _pad_rows(kh_all[rs], LANES).astype(BF16)
            scores = _dot_nt(q_heads[h][rs].astype(BF16), kh_p) * dmat_ref[h]
            o_intra = _dot(scores.astype(BF16), vh_p)
            s_in = ret_in_ref[i, h]
            o_cross = _dot(q_start[rs].astype(BF16), s_in.astype(BF16))
            kv = _dot(_pad_rows(k_end[rs], LANES).T.astype(BF16), vh_p)
            ret_ref[i, h] = ret_chunk_decay[h] * s_in + kv
            o_rows.append(o_intra + o_cross)
        o_heads.append(jnp.concatenate(o_rows, axis=0))
    gate = proj_ref[:, OFF_G:OFF_G + RET_WIDTH]
    y_ref[:, SSD_WIDTH:SSD_WIDTH + RET_WIDTH] = _ret_out(o_heads, gate, rgain_ref[...]).astype(y_ref.dtype)


def _mixer_sample(layer, proj, dt_raw, conv_state, ssm_state, ret_state, pw, prev_states, batch, cl, bb):
    dmat, kdec, qdec, chunk_decay = _ret_tables(cl, bb)
    dmat = np.concatenate([dmat, np.zeros((RET_HEADS, cl, LANES - cl), np.float32)], axis=-1)
    cos, sin = _rope_tables(PAST_LEN, cl, bb)
    m = bb * cl
    tri = np.kron(np.eye(bb, dtype=np.float32), np.tril(np.ones((cl, cl), np.float32)))
    kern = functools.partial(_mixer_sample_kernel, bb=bb, cl=cl, ret_chunk_decay=chunk_decay)
    row_map = lambda i: (i, 0)
    state_specs = [
        pl.BlockSpec((None, bb, CONV_WIDTH - 1, CONV_DIM), lambda i: (layer, i, 0, 0)),
        pl.BlockSpec((None, bb, SSD_WIDTH, SSD_STATE), lambda i: (layer, i, 0, 0)),
        pl.BlockSpec((None, bb, RET_HEADS, RET_QK_DIM, RET_V_DIM), lambda i: (layer, i, 0, 0, 0)),
    ]
    n_in = 19
    return pl.pallas_call(
        kern,
        grid=(batch // bb,),
        in_specs=[
            pl.BlockSpec((m, PROJ_MAIN), row_map),
            pl.BlockSpec((m, LANES), row_map),
            *state_specs,
            *_mixer_param_specs(layer),
            _const_spec((m, LANES)),
            _const_spec((m, LANES)),
            _const_spec((RET_HEADS, cl, LANES)),
            _const_spec((m, RET_QK_WIDTH)),
            _const_spec((m, RET_QK_WIDTH)),
            _const_spec((m, m)),
            *[_alias_spec() for _ in prev_states],
        ],
        out_specs=[pl.BlockSpec((m, SSD_WIDTH + RET_WIDTH), row_map), *state_specs],
        out_shape=[jax.ShapeDtypeStruct((batch * cl, SSD_WIDTH + RET_WIDTH), BF16), *_state_out_shapes(batch)],
        input_output_aliases={n_in + k: 1 + k for k in range(len(prev_states))},
        scratch_shapes=[pltpu.VMEM((bb, 8 + cl, CONV_DIM), F32)],
        compiler_params=_cparams(("parallel",)),
        name="mixer_sample",
    )(proj, dt_raw, conv_state, ssm_state, ret_state, *_mixer_params(pw), jnp.asarray(cos), jnp.asarray(sin),
      jnp.asarray(dmat), jnp.asarray(kdec), jnp.asarray(qdec), jnp.asarray(tri, dtype=BF16), *prev_states)


def _outproj_kernel(x_ref, y_ref, w_ref, o_ref):
    o_ref[...] = x_ref[...] + _dot(y_ref[...], w_ref[...])


def _outproj(x, y, layer, pw, tm):
    t = x.shape[0]
    k = y.shape[1]
    return pl.pallas_call(
        _outproj_kernel,
        grid=(t // tm,),
        in_specs=[
            pl.BlockSpec((tm, D_MODEL), lambda i: (i, 0)),
            pl.BlockSpec((tm, k), lambda i: (i, 0)),
            _layer_spec((k, D_MODEL), layer),
        ],
        out_specs=pl.BlockSpec((tm, D_MODEL), lambda i: (i, 0)),
        out_shape=jax.ShapeDtypeStruct((t, D_MODEL), F32),
        compiler_params=_cparams(("parallel",)),
        name="outproj",
    )(x, y, pw["w_out"])


def _memproj_kernel(m_ref, w_ref, k_ref, v_ref, kb_ref, vb_ref, *, bb):
    r = _dot(m_ref[...].reshape(bb * N_MEM, D_MODEL).astype(BF16), w_ref[...])
    kb_ref[...] = r[:, 0:D_MODEL].astype(BF16).reshape(bb, N_MEM, D_MODEL)
    vb_ref[...] = r[:, D_MODEL:2 * D_MODEL].astype(BF16).reshape(bb, N_MEM, D_MODEL)
    for i in range(bb):
        rows = slice(i * N_MEM, (i + 1) * N_MEM)
        for hd in range(XA_HEADS):
            k_ref[i, :, hd, :] = r[rows, hd * XA_HEAD_DIM:(hd + 1) * XA_HEAD_DIM]
            v_ref[i, :, hd, :] = r[rows, D_MODEL + hd * XA_HEAD_DIM:D_MODEL + (hd + 1) * XA_HEAD_DIM]


def _memproj(mem, pw, bb):
    b = mem.shape[0]
    cache_spec = pl.BlockSpec((None, bb, N_MEM, XA_HEADS, XA_HEAD_DIM), lambda l, i: (l, i, 0, 0, 0))
    cache_shape = jax.ShapeDtypeStruct((DEPTH, b, N_MEM, XA_HEADS, XA_HEAD_DIM), F32)
    flat_spec = pl.BlockSpec((None, bb, N_MEM, D_MODEL), lambda l, i: (l, i, 0, 0))
    flat_shape = jax.ShapeDtypeStruct((DEPTH, b, N_MEM, D_MODEL), BF16)
    return pl.pallas_call(
        functools.partial(_memproj_kernel, bb=bb),
        grid=(DEPTH, b // bb),
        in_specs=[
            pl.BlockSpec((bb, N_MEM, D_MODEL), lambda l, i: (i, 0, 0)),
            pl.BlockSpec((None, D_MODEL, 2 * D_MODEL), lambda l, i: (l, 0, 0)),
        ],
        out_specs=[cache_spec, cache_spec, flat_spec, flat_spec],
        out_shape=[cache_shape, cache_shape, flat_shape, flat_shape],
        compiler_params=_cparams(("parallel", "parallel")),
        name="memproj",
    )(mem, pw["w_kv"])


def _softmax_rows(s):
    e = jnp.exp(s - jnp.max(s, axis=-1, keepdims=True))
    return e / jnp.sum(e, axis=-1, keepdims=True)


def _xattn_prompt_kernel(x_ref, g_ref, wq_ref, wo_ref, k_ref, v_ref, o_ref):
    x = x_ref[0]
    h = (_rms(x) * g_ref[...]).astype(BF16)
    q = _dot(h, wq_ref[...])
    scale = XA_HEAD_DIM ** -0.5
    heads = []
    for hd in range(XA_HEADS):
        hs = slice(hd * XA_HEAD_DIM, (hd + 1) * XA_HEAD_DIM)
        p = _softmax_rows(_dot_nt(q[:, hs].astype(BF16), k_ref[0, :, hs]) * scale)
        heads.append(_dot(p.astype(BF16), v_ref[0, :, hs]))
    o = jnp.concatenate(heads, axis=-1).astype(BF16)
    o_ref[0] = x + _dot(o, wo_ref[...])


def _xattn_cache_kernel(x_ref, g_ref, wq_ref, wo_ref, k_ref, v_ref, o_ref, *, bb, tm):
    x = x_ref[...].reshape(bb * tm, D_MODEL)
    h = (_rms(x) * g_ref[...]).astype(BF16)
    q = _dot(h, wq_ref[...])
    scale = XA_HEAD_DIM ** -0.5
    rows_flat = XA_HEADS * tm
    assert tm & (tm - 1) == 0 and XA_HEADS & (XA_HEADS - 1) == 0
    row_head = lax.shift_right_logical(lax.broadcasted_iota(jnp.int32, (rows_flat, N_MEM * XA_HEADS), 0),
                                       int(math.log2(tm)))
    col_head = lax.broadcasted_iota(jnp.int32, (rows_flat, N_MEM * XA_HEADS), 1) & (XA_HEADS - 1)
    own_head = row_head == col_head
    outs = []
    for i in range(bb):
        qi = q[i * tm:(i + 1) * tm]
        qf = jnp.concatenate([qi[:, hd * XA_HEAD_DIM:(hd + 1) * XA_HEAD_DIM] for hd in range(XA_HEADS)], axis=0)
        k_all = k_ref[i].reshape(N_MEM * XA_HEADS, XA_HEAD_DIM).astype(BF16)
        v_all = v_ref[i].reshape(N_MEM * XA_HEADS, XA_HEAD_DIM).astype(BF16)
        s = jnp.where(own_head, _dot_nt(qf.astype(BF16), k_all) * scale, -jnp.inf)
        of = _dot(_softmax_rows(s).astype(BF16), v_all)
        outs.append(jnp.concatenate([of[hd * tm:(hd + 1) * tm] for hd in range(XA_HEADS)], axis=-1))
    o = jnp.concatenate(outs, axis=0).astype(BF16)
    o_ref[...] = (x + _dot(o, wo_ref[...])).reshape(bb, tm, D_MODEL)


def _xattn_weight_specs(layer):
    return [_layer_spec((1, D_MODEL), layer), _layer_spec((D_MODEL, D_MODEL), layer),
            _layer_spec((D_MODEL, D_MODEL), layer)]


def _xattn_prompt(x, layer, pw, mem_k, mem_v, tm):
    b, l, _ = x.shape
    mem_spec = pl.BlockSpec((None, 1, N_MEM, D_MODEL), lambda i, j: (layer, i, 0, 0))
    return pl.pallas_call(
        _xattn_prompt_kernel,
        grid=(b, l // tm),
        in_specs=[pl.BlockSpec((1, tm, D_MODEL), lambda i, j: (i, j, 0)), *_xattn_weight_specs(layer),
                  mem_spec, mem_spec],
        out_specs=pl.BlockSpec((1, tm, D_MODEL), lambda i, j: (i, j, 0)),
        out_shape=jax.ShapeDtypeStruct((b, l, D_MODEL), F32),
        compiler_params=_cparams(("parallel", "parallel")),
        name="xattn_prompt",
    )(x, pw["norm_mem"], pw["w_mq"], pw["w_mo"], mem_k, mem_v)


def _xattn_cache(x, layer, pw, mem_k, mem_v, bb):
    b, l, _ = x.shape
    kern = functools.partial(_xattn_cache_kernel, bb=bb, tm=l)
    mem_spec = pl.BlockSpec((None, bb, N_MEM, XA_HEADS, XA_HEAD_DIM), lambda i: (layer, i, 0, 0, 0))
    return pl.pallas_call(
        kern,
        grid=(b // bb,),
        in_specs=[pl.BlockSpec((bb, l, D_MODEL), lambda i: (i, 0, 0)), *_xattn_weight_specs(layer),
                  mem_spec, mem_spec],
        out_specs=pl.BlockSpec((bb, l, D_MODEL), lambda i: (i, 0, 0)),
        out_shape=jax.ShapeDtypeStruct((b, l, D_MODEL), F32),
        compiler_params=_cparams(("parallel",)),
        name="xattn_cache",
    )(x, pw["norm_mem"], pw["w_mq"], pw["w_mo"], mem_k, mem_v)


def _router_gates(logits):
    m = logits.shape[0]
    lane_i = lax.broadcasted_iota(jnp.int32, (m, LANES), 1)
    lane = lane_i.astype(F32)
    big = float(LANES)
    is_g = lane_i < N_EGROUPS
    gl = jnp.where(is_g, logits, -jnp.inf)
    gmax = jnp.max(gl, axis=-1, keepdims=True)
    g_idx = jnp.min(jnp.where(is_g & (gl == gmax), lane, big), axis=-1, keepdims=True)
    g_prob = 1.0 / jnp.sum(jnp.exp(gl - gmax), axis=-1, keepdims=True)
    e_lane = lane_i - ROUTER_OFF
    e_group = lax.shift_right_arithmetic(e_lane, int(math.log2(EXPERTS_PER_GROUP))).astype(F32)
    sel = (e_lane >= 0) & (e_lane < N_EXPERTS) & (e_group == g_idx)
    el = jnp.where(sel, logits, -jnp.inf)
    emax = jnp.max(el, axis=-1, keepdims=True)
    ee = jnp.exp(el - emax)
    e_prob = ee / jnp.sum(ee, axis=-1, keepdims=True)
    p1 = jnp.max(jnp.where(sel, e_prob, -1.0), axis=-1, keepdims=True)
    i1 = jnp.min(jnp.where(sel & (e_prob == p1), lane, big), axis=-1, keepdims=True)
    sel2 = sel & (lane != i1)
    p2 = jnp.max(jnp.where(sel2, e_prob, -1.0), axis=-1, keepdims=True)
    i2 = jnp.min(jnp.where(sel2 & (e_prob == p2), lane, big), axis=-1, keepdims=True)
    denom = p1 + p2
    w1 = g_prob * p1 / denom
    w2 = g_prob * p2 / denom
    return jnp.where(lane == i1, w1, 0.0) + jnp.where(lane == i2, w2, 0.0), g_idx


def _moe_kernel(x_ref, g_ref, wr_hi_ref, wr_lo_ref, br_ref, wg_ref, wu_ref, wd_ref, gf_ref, o_ref,
                h_scr, gate_scr, *, final_norm):
    e = pl.program_id(1)

    @pl.when(e == 0)
    def _():
        h = _rms(x_ref[...]) * g_ref[...]
        h_hi = h.astype(BF16)
        h_lo = (h - h_hi.astype(F32)).astype(BF16)
        logits = (_dot(h_hi, wr_hi_ref[...]) + _dot(h_hi, wr_lo_ref[...]) + _dot(h_lo, wr_hi_ref[...])
                  + br_ref[...])
        h_scr[...] = h_hi
        gate_scr[...] = _router_gates(logits)[0]
        o_ref[...] = jnp.zeros_like(o_ref)

    hb = h_scr[...]
    lane = lax.broadcasted_iota(jnp.int32, gate_scr.shape, 1)
    gate_e = jnp.sum(jnp.where(lane == e + ROUTER_OFF, gate_scr[...], 0.0), axis=-1, keepdims=True)
    a = _silu(_dot(hb, wg_ref[...])) * _dot(hb, wu_ref[...]) * gate_e
    o_ref[...] += _dot(a.astype(BF16), wd_ref[...])

    @pl.when(e == pl.num_programs(1) - 1)
    def _():
        y = x_ref[...] + o_ref[...]
        if final_norm:
            y = _rms(y) * gf_ref[...]
        o_ref[...] = y


def _moe(x, layer, pw, final_norm, tm):
    t = x.shape[0]
    kern = functools.partial(_moe_kernel, final_norm=final_norm)
    return pl.pallas_call(
        kern,
        grid=(t // tm, N_EXPERTS),
        in_specs=[
            pl.BlockSpec((tm, D_MODEL), lambda i, e: (i, 0)),
            _layer_spec((1, D_MODEL), layer),
            _layer_spec((D_MODEL, LANES), layer),
            _layer_spec((D_MODEL, LANES), layer),
            _layer_spec((1, LANES), layer),
            pl.BlockSpec((None, None, D_MODEL, EXPERT_FF), lambda i, e: (layer, e, 0, 0)),
            pl.BlockSpec((None, None, D_MODEL, EXPERT_FF), lambda i, e: (layer, e, 0, 0)),
            pl.BlockSpec((None, None, EXPERT_FF, D_MODEL), lambda i, e: (layer, e, 0, 0)),
            _const_spec((1, D_MODEL)),
        ],
        out_specs=pl.BlockSpec((tm, D_MODEL), lambda i, e: (i, 0)),
        out_shape=jax.ShapeDtypeStruct((t, D_MODEL), F32),
        scratch_shapes=[pltpu.VMEM((tm, D_MODEL), BF16), pltpu.VMEM((tm, LANES), F32)],
        compiler_params=_cparams(("parallel", "arbitrary")),
        name="moe",
    )(x, pw["norm_ffn"], pw["wr_hi"], pw["wr_lo"], pw["b_router"], pw["w_gate"], pw["w_up"], pw["w_down"],
      pw["norm_final"])


MOE_TS = 2048
MOE_M = 256
MOE_NB = MOE_TS // MOE_M + N_EGROUPS - 1
MOE_R = MOE_NB * MOE_M
SUBLANES = 8
TOKEN_TILE = (SUBLANES, D_MODEL // SUBLANES)
PLAN_BLK = 512


def _moe_route_kernel(x_ref, g_ref, wr_hi_ref, wr_lo_ref, br_ref, h_ref, info_ref):
    h = _rms(x_ref[...]) * g_ref[...]
    h_hi = h.astype(BF16)
    h_lo = (h - h_hi.astype(F32)).astype(BF16)
    logits = (_dot(h_hi, wr_hi_ref[...]) + _dot(h_hi, wr_lo_ref[...]) + _dot(h_lo, wr_hi_ref[...])
              + br_ref[...])
    gates, g_idx = _router_gates(logits)
    lane = lax.broadcasted_iota(jnp.int32, gates.shape, 1)
    info_ref[...] = jnp.where(lane == 0, g_idx, gates)
    h_ref[...] = h.reshape(h.shape[0], *TOKEN_TILE)


def _moe_route(x, layer, pw, tm):
    t = x.shape[0]
    return pl.pallas_call(
        _moe_route_kernel,
        grid=(t // tm,),
        in_specs=[
            pl.BlockSpec((tm, D_MODEL), lambda i: (i, 0)),
            _layer_spec((1, D_MODEL), layer),
            _layer_spec((D_MODEL, LANES), layer),
            _layer_spec((D_MODEL, LANES), layer),
            _layer_spec((1, LANES), layer),
        ],
        out_specs=[pl.BlockSpec((tm, *TOKEN_TILE), lambda i: (i, 0, 0)),
                   pl.BlockSpec((tm, LANES), lambda i: (i, 0))],
        out_shape=[jax.ShapeDtypeStruct((t, *TOKEN_TILE), F32), jax.ShapeDtypeStruct((t, LANES), F32)],
        compiler_params=_cparams(("parallel",)),
        name="moe_route",
    )(x, pw["norm_ffn"], pw["wr_hi"], pw["wr_lo"], pw["b_router"])


def _lane_pick(v, idx):
    lane = lax.broadcasted_iota(jnp.int32, v.shape, 1)
    return jnp.sum(jnp.where(lane == idx, v, 0.0), axis=-1, keepdims=True)


def _moe_plan_kernel(info_ref, dest_ref, items_ref, *, n_tiles):
    row = lax.broadcasted_iota(jnp.int32, (PLAN_BLK, PLAN_BLK), 0)
    col = lax.broadcasted_iota(jnp.int32, (PLAN_BLK, PLAN_BLK), 1)
    tri = jnp.where(row >= col, 1.0, 0.0).astype(BF16)
    row = lax.broadcasted_iota(jnp.int32, (LANES, LANES), 0)
    col = lax.broadcasted_iota(jnp.int32, (LANES, LANES), 1)
    before = jnp.where(row < col, 1.0, 0.0).astype(BF16)
    lane = lax.broadcasted_iota(jnp.int32, (PLAN_BLK, LANES), 1)
    lane_row = lax.broadcasted_iota(jnp.int32, (1, LANES), 1)
    block_groups = jnp.zeros((1, LANES), F32)
    for s in range(n_tiles):
        carry = jnp.zeros((1, LANES), F32)
        parts = []
        for blk in range(MOE_TS // PLAN_BLK):
            r0 = s * MOE_TS + blk * PLAN_BLK
            gid = info_ref[r0:r0 + PLAN_BLK, 0:1]
            onehot = jnp.where((lane < N_EGROUPS) & (lane.astype(F32) == gid), 1.0, 0.0)
            cum = _dot(tri, onehot.astype(BF16)) + carry
            carry = cum[PLAN_BLK - 1:PLAN_BLK, :]
            parts.append((onehot, cum))
        counts = carry
        padded = jnp.ceil(counts * (1.0 / MOE_M)) * MOE_M
        seg_start = _dot_sel(jnp.broadcast_to(padded, (8, LANES)), before)[0:1, :]
        rows = []
        for onehot, cum in parts:
            dest_col = jnp.sum(onehot * (seg_start + cum - 1.0), axis=-1, keepdims=True)
            for q in range(PLAN_BLK // LANES):
                piece = jnp.broadcast_to(dest_col[q * LANES:(q + 1) * LANES, :], (LANES, LANES))
                rows.append(piece.T[0:1, :])
        n_rows = MOE_TS // LANES
        dest_ref[s * n_rows:(s + 1) * n_rows, :] = jnp.concatenate(rows, axis=0).astype(jnp.int32)
        blk_start = lane_row.astype(F32) * MOE_M
        group_of_block = jnp.full((1, LANES), -1.0, F32)
        for g in range(N_EGROUPS):
            start_g = _lane_pick(seg_start, g)
            size_g = _lane_pick(padded, g)
            group_of_block = jnp.where((blk_start >= start_g) & (blk_start < start_g + size_g), float(g),
                                       group_of_block)
        in_tile = jnp.where(lane_row < MOE_NB, group_of_block + 1.0, 0.0)
        if s:
            in_tile = pltpu.roll(jnp.broadcast_to(in_tile, (SUBLANES, LANES)), s * MOE_NB, axis=1)[0:1]
        block_groups = block_groups + in_tile

    n_items = n_tiles * MOE_NB
    group = block_groups - 1.0
    lane_f = lane_row.astype(F32)
    order_key = jnp.where(group >= 0, group, float(N_EGROUPS)) * LANES + lane_f
    order_key = jnp.where(lane_row < n_items, order_key, float((N_EGROUPS + 1) * LANES) + lane_f)
    key_by_lane = jnp.broadcast_to(order_key, (LANES, LANES))
    key_by_row = key_by_lane.T
    rank = jnp.sum(jnp.where(key_by_lane < key_by_row, 1.0, 0.0), axis=-1, keepdims=True)
    chosen = rank == lane_f
    block_id = lax.broadcasted_iota(jnp.int32, (LANES, LANES), 0).astype(F32)
    group_by_row = jnp.broadcast_to(group, (LANES, LANES)).T
    item_block = jnp.sum(jnp.where(chosen, block_id, 0.0), axis=0, keepdims=True)
    item_group = jnp.sum(jnp.where(chosen, group_by_row, 0.0), axis=0, keepdims=True)
    item_group = jnp.where(item_group >= 0, item_group, -float(N_EGROUPS))
    items_ref[0:1, :] = item_block.astype(jnp.int32)
    items_ref[1:2, :] = item_group.astype(jnp.int32)


def _moe_plan(info, n_tiles):
    t = info.shape[0]
    assert n_tiles * MOE_NB <= LANES
    return pl.pallas_call(
        functools.partial(_moe_plan_kernel, n_tiles=n_tiles),
        grid=(1,),
        in_specs=[pl.BlockSpec((t, LANES), lambda i: (0, 0))],
        out_specs=[pl.BlockSpec((t // LANES, LANES), lambda i: (0, 0)),
                   pl.BlockSpec((2, LANES), lambda i: (0, 0))],
        out_shape=[jax.ShapeDtypeStruct((t // LANES, LANES), jnp.int32),
                   jax.ShapeDtypeStruct((2, LANES), jnp.int32)],
        compiler_params=_cparams(("arbitrary",)),
        name="moe_plan",
    )(info)


def _dest_row(dest_ref, token):
    return dest_ref[token]


def _moe_permute_kernel(dest_ref, h_ref, info_ref, hs_ref, infos_ref):
    base = pl.program_id(0) * MOE_TS
    hs_ref[...] = jnp.zeros_like(hs_ref)
    infos_ref[...] = jnp.zeros_like(infos_ref)

    def body(t, carry):
        d = _dest_row(dest_ref, base + t)
        hs_ref[d] = h_ref[t]
        infos_ref[pl.ds(d, 1), :] = info_ref[pl.ds(t, 1), :]
        return carry

    lax.fori_loop(0, MOE_TS, body, 0, unroll=8)


def _moe_permute(dest, h, info, n_tiles):
    return pl.pallas_call(
        _moe_permute_kernel,
        grid_spec=pltpu.PrefetchScalarGridSpec(
            num_scalar_prefetch=1,
            grid=(n_tiles,),
            in_specs=[pl.BlockSpec((MOE_TS, *TOKEN_TILE), lambda s, d: (s, 0, 0)),
                      pl.BlockSpec((MOE_TS, LANES), lambda s, d: (s, 0))],
            out_specs=[pl.BlockSpec((MOE_R, *TOKEN_TILE), lambda s, d: (s, 0, 0)),
                       pl.BlockSpec((MOE_R, LANES), lambda s, d: (s, 0))],
        ),
        out_shape=[jax.ShapeDtypeStruct((n_tiles * MOE_R, *TOKEN_TILE), F32),
                   jax.ShapeDtypeStruct((n_tiles * MOE_R, LANES), F32)],
        compiler_params=_cparams(("arbitrary",)),
        name="moe_permute",
    )(dest, h, info)


def _moe_ffn_kernel(item_block_ref, item_group_ref, x_ref, gates_ref, wg_ref, wu_ref, wd_ref, y_ref):
    group = item_group_ref[pl.program_id(0)]

    @pl.when(group >= 0)
    def _():
        xb = x_ref[...].reshape(MOE_M, D_MODEL).astype(BF16)
        gates = gates_ref[...]
        lane = lax.broadcasted_iota(jnp.int32, gates.shape, 1)
        y = None
        for e in range(EXPERTS_PER_GROUP):
            gate_e = jnp.sum(jnp.where(lane == ROUTER_OFF + group * EXPERTS_PER_GROUP + e, gates, 0.0),
                             axis=-1, keepdims=True)
            a = _silu(_dot(xb, wg_ref[e])) * _dot(xb, wu_ref[e]) * gate_e
            d = _dot(a.astype(BF16), wd_ref[e])
            y = d if y is None else y + d
        y_ref[...] = y.reshape(MOE_M, *TOKEN_TILE)

    @pl.when(group < 0)
    def _():
        y_ref[...] = jnp.zeros_like(y_ref)


def _moe_ffn(item_block, item_group, hs, infos, layer, pw, n_tiles):
    def w_map(w, blk, grp):
        g = grp[w]
        return (layer, jnp.where(g >= 0, g, -1 - g), 0, 0, 0)

    def grouped(w):
        return w.reshape(DEPTH, N_EGROUPS, EXPERTS_PER_GROUP, *w.shape[2:])

    tile_spec = pl.BlockSpec((MOE_M, *TOKEN_TILE), lambda w, blk, grp: (blk[w], 0, 0))
    return pl.pallas_call(
        _moe_ffn_kernel,
        grid_spec=pltpu.PrefetchScalarGridSpec(
            num_scalar_prefetch=2,
            grid=(n_tiles * MOE_NB,),
            in_specs=[
                tile_spec,
                pl.BlockSpec((MOE_M, LANES), lambda w, blk, grp: (blk[w], 0)),
                pl.BlockSpec((None, None, EXPERTS_PER_GROUP, D_MODEL, EXPERT_FF), w_map),
                pl.BlockSpec((None, None, EXPERTS_PER_GROUP, D_MODEL, EXPERT_FF), w_map),
                pl.BlockSpec((None, None, EXPERTS_PER_GROUP, EXPERT_FF, D_MODEL), w_map),
            ],
            out_specs=tile_spec,
        ),
        out_shape=jax.ShapeDtypeStruct((n_tiles * MOE_R, *TOKEN_TILE), F32),
        compiler_params=_cparams(("arbitrary",)),
        name="moe_ffn",
    )(item_block, item_group, hs, infos, grouped(pw["w_gate"]), grouped(pw["w_up"]), grouped(pw["w_down"]))


def _moe_unpermute_kernel(dest_ref, x_ref, ys_ref, gf_ref, o_ref, y_scr, *, tm, final_norm):
    base = pl.program_id(0) * MOE_TS + pl.program_id(1) * tm

    def body(t, carry):
        y_scr[t] = ys_ref[_dest_row(dest_ref, base + t)]
        return carry

    lax.fori_loop(0, tm, body, 0, unroll=8)
    y = x_ref[...] + y_scr[...].reshape(tm, D_MODEL)
    if final_norm:
        y = _rms(y) * gf_ref[...]
    o_ref[...] = y


def _moe_unpermute(dest, x, ys, pw, final_norm, n_tiles, tm):
    per_tile = MOE_TS // tm
    return pl.pallas_call(
        functools.partial(_moe_unpermute_kernel, tm=tm, final_norm=final_norm),
        grid_spec=pltpu.PrefetchScalarGridSpec(
            num_scalar_prefetch=1,
            grid=(n_tiles, per_tile),
            in_specs=[
                pl.BlockSpec((tm, D_MODEL), lambda s, i, d: (s * per_tile + i, 0)),
                pl.BlockSpec((MOE_R, *TOKEN_TILE), lambda s, i, d: (s, 0, 0)),
                pl.BlockSpec((1, D_MODEL), lambda s, i, d: (0, 0), pipeline_mode=pl.Buffered(1)),
            ],
            out_specs=pl.BlockSpec((tm, D_MODEL), lambda s, i, d: (s * per_tile + i, 0)),
            scratch_shapes=[pltpu.VMEM((tm, *TOKEN_TILE), F32)],
        ),
        out_shape=jax.ShapeDtypeStruct(x.shape, F32),
        compiler_params=_cparams(("arbitrary", "arbitrary")),
        name="moe_unpermute",
    )(dest, x, ys, pw["norm_final"])


def _moe_sorted(x, layer, pw, final_norm):
    n_tiles = x.shape[0] // MOE_TS
    h, info = _moe_route(x, layer, pw, 1024)
    dest, items = _moe_plan(info, n_tiles)
    dest = dest.reshape(-1)
    hs, infos = _moe_permute(dest, h, info, n_tiles)
    ys = _moe_ffn(items[0], items[1], hs, infos, layer, pw, n_tiles)
    return _moe_unpermute(dest, x, ys, pw, final_norm, n_tiles, 1024)


def _row(v):
    return v.reshape(v.shape[0], 1, v.shape[1])


def _pad_lanes(v):
    return jnp.pad(v, ((0, 0),) * (v.ndim - 1) + ((0, LANES - v.shape[-1]),))


def _prep_weights(norm_mix, w_in, conv_w, conv_b, dt_bias, a_log, d_skip, ssd_gain, ret_gain, w_out,
                  norm_mem, w_mq, w_mk, w_mv, w_mo, norm_ffn, w_rg, b_rg, w_re, b_re, w_gate, w_up, w_down,
                  norm_final):
    dt_off = SSD_WIDTH + CONV_DIM
    w_main = jnp.concatenate([w_in[:, :, :dt_off], w_in[:, :, dt_off + SSD_HEADS:]], axis=2).astype(BF16)
    w_dt = _pad_lanes(w_in[:, :, dt_off:dt_off + SSD_HEADS]).astype(BF16)
    w_router = _pad_lanes(jnp.concatenate([w_rg, w_re], axis=2))
    wr_hi = w_router.astype(BF16)
    wr_lo = (w_router - wr_hi.astype(F32)).astype(BF16)
    return dict(
        norm_mix=_row(norm_mix), w_main=w_main, w_dt=w_dt,
        conv_w=conv_w, conv_b=_row(conv_b),
        dt_bias=_row(_pad_lanes(dt_bias)), a_log=_row(_pad_lanes(a_log)),
        expand=jnp.asarray(_expand_matrix(), dtype=BF16),
        d_skip=_row(jnp.repeat(d_skip, SSD_HEAD_DIM, axis=1)),
        ssd_gain=_row(ssd_gain), ret_gain=_row(ret_gain),
        w_out=w_out.astype(BF16),
        norm_mem=_row(norm_mem), w_mq=w_mq.astype(BF16), w_mo=w_mo.astype(BF16),
        w_kv=jnp.concatenate([w_mk, w_mv], axis=2).astype(BF16),
        norm_ffn=_row(norm_ffn), wr_hi=wr_hi, wr_lo=wr_lo,
        b_router=_row(_pad_lanes(jnp.concatenate([b_rg, b_re], axis=1))),
        w_gate=w_gate.astype(BF16), w_up=w_up.astype(BF16), w_down=w_down.astype(BF16),
        norm_final=norm_final.reshape(1, -1),
    )


def _token_tile(t, cap):
    tm = min(t, cap)
    assert t % tm == 0
    return tm


def _trunk(x, mixer_fn, xattn_fn, pw, proj_dtype):
    b, l, _ = x.shape
    t = b * l
    xf = x.reshape(t, D_MODEL)
    states = ()
    for layer in range(DEPTH):
        proj, dt_raw = _inproj(xf, layer, pw, proj_dtype, _token_tile(t, 512))
        y, *states = mixer_fn(layer, proj, dt_raw, tuple(states))
        xf = _outproj(xf, y, layer, pw, _token_tile(t, 512))
        xf = xattn_fn(layer, xf.reshape(b, l, D_MODEL)).reshape(t, D_MODEL)
        if t % MOE_TS == 0:
            xf = _moe_sorted(xf, layer, pw, layer == DEPTH - 1)
        else:
            xf = _moe(xf, layer, pw, layer == DEPTH - 1, _token_tile(t, 1024))
    conv, ssm, ret = states
    return (xf.reshape(b, l, D_MODEL), ssm.reshape(DEPTH, b, SSD_HEADS, SSD_HEAD_DIM, SSD_STATE), conv, ret)


def kernel(x_prompt, x_sample, mem_prompt, state_ssm, state_conv, state_ret, cache_mem_k, cache_mem_v,
           norm_mix, w_in, conv_w, conv_b, dt_bias, a_log, d_skip, ssd_gain, ret_gain, w_out,
           norm_mem, w_mq, w_mk, w_mv, w_mo, norm_ffn, w_rg, b_rg, w_re, b_re, w_gate, w_up, w_down,
           norm_final):
    pw = _prep_weights(norm_mix, w_in, conv_w, conv_b, dt_bias, a_log, d_skip, ssd_gain, ret_gain, w_out,
                       norm_mem, w_mq, w_mk, w_mv, w_mo, norm_ffn, w_rg, b_rg, w_re, b_re, w_gate, w_up, w_down,
                       norm_final)
    bp, lp, _ = x_prompt.shape
    bs, ls, _ = x_sample.shape
    n_mem = mem_prompt.shape[1]

    assert n_mem == N_MEM
    mem_k_p, mem_v_p, mem_k_rows, mem_v_rows = _memproj(mem_prompt, pw, _token_tile(bp, 2))

    def mixer_p(layer, proj, dt_raw, prev_states):
        return _mixer_prompt(layer, proj, dt_raw, pw, prev_states, bp, lp)

    def xattn_p(layer, x):
        return _xattn_prompt(x, layer, pw, mem_k_rows, mem_v_rows, _token_tile(lp, 512))

    y_prompt, ssm_p, conv_p, ret_p = _trunk(x_prompt, mixer_p, xattn_p, pw, BF16)

    sample_bb = 4
    ssm_in = state_ssm.reshape(DEPTH, bs, SSD_WIDTH, SSD_STATE)

    def mixer_s(layer, proj, dt_raw, prev_states):
        return _mixer_sample(layer, proj, dt_raw, state_conv, ssm_in, state_ret, pw, prev_states, bs, ls,
                             sample_bb)

    def xattn_s(layer, x):
        return _xattn_cache(x, layer, pw, cache_mem_k, cache_mem_v, sample_bb)

    y_sample, ssm_s, conv_s, ret_s = _trunk(x_sample, mixer_s, xattn_s, pw, F32)
    return (y_prompt, y_sample, ssm_p, conv_p, ret_p, mem_k_p, mem_v_p, ssm_s, conv_s, ret_s)
```

```python
import functools
import math

import numpy as np
import jax
import jax.numpy as jnp
from jax import lax
from jax.experimental import pallas as pl
from jax.experimental.pallas import tpu as pltpu

F32 = jnp.float32
BF16 = jnp.bfloat16

D_MODEL = 1024
DEPTH = 2
PAST_LEN = 16384
SSD_HEAD_DIM = 64
SSD_HEADS = 16
SSD_GROUPS = 2
SSD_STATE = 128
SSD_WIDTH = 1024
GROUP_WIDTH = SSD_WIDTH // SSD_GROUPS
CONV_WIDTH = 4
CONV_DIM = SSD_WIDTH + 2 * SSD_GROUPS * SSD_STATE
RET_HEADS = 4
RET_V_DIM = 256
RET_QK_DIM = 128
RET_WIDTH = 1024
RET_QK_WIDTH = RET_HEADS * RET_QK_DIM
ROPE_BASE = 10000.0
N_MEM = 256
XA_HEADS = 4
XA_HEAD_DIM = 256
N_EGROUPS = 4
EXPERTS_PER_GROUP = 4
N_EXPERTS = 16
EXPERT_FF = 512
RMS_EPS = 1e-6

LANES = 128
CHUNK = 128
OFF_Z = 0
OFF_XBC = OFF_Z + SSD_WIDTH
OFF_Q = OFF_XBC + CONV_DIM
OFF_K = OFF_Q + RET_QK_WIDTH
OFF_V = OFF_K + RET_QK_WIDTH
OFF_G = OFF_V + RET_WIDTH
PROJ_MAIN = OFF_G + RET_WIDTH
INPROJ_TN = 512
CONV_TAIL = 16
MIXER_SEQS_PER_STEP = 2
CONV_K = 256
ROUTER_OFF = N_EGROUPS

VMEM_LIMIT = 56 * 1024 * 1024


def _cparams(sem):
    return pltpu.CompilerParams(dimension_semantics=sem, vmem_limit_bytes=VMEM_LIMIT)


def _const_spec(shape):
    nd = len(shape)
    return pl.BlockSpec(shape, lambda *_: (0,) * nd, pipeline_mode=pl.Buffered(1))


def _layer_spec(shape, layer):
    nd = len(shape)
    return pl.BlockSpec((None,) + tuple(shape), lambda *_: (layer,) + (0,) * nd, pipeline_mode=pl.Buffered(1))


def _alias_spec():
    return pl.BlockSpec(memory_space=pl.ANY)


def _rms(x):
    return x * lax.rsqrt(jnp.mean(x * x, axis=-1, keepdims=True) + RMS_EPS)


def _silu(x):
    return x * jax.nn.sigmoid(x)


def _softplus(x):
    return jnp.maximum(x, 0.0) + jnp.log1p(jnp.exp(-jnp.abs(x)))


def _split3(x):
    hi = x.astype(BF16)
    r = x - hi.astype(F32)
    mid = r.astype(BF16)
    lo = (r - mid.astype(F32)).astype(BF16)
    return hi, mid, lo


def _dot(a, b):
    return jnp.dot(a, b, preferred_element_type=F32)


def _dot_nt(a, b):
    return lax.dot_general(a, b, (((1,), (1,)), ((), ())), preferred_element_type=F32)


def _dot_sel(x, sel):
    hi, mid, lo = _split3(x)
    return _dot(hi, sel) + _dot(mid, sel) + _dot(lo, sel)


def _sel_dot(sel, x):
    hi, mid, lo = _split3(x)
    return _dot(sel, hi) + _dot(sel, mid) + _dot(sel, lo)


def _inproj_kernel(x_ref, g_ref, w_ref, wdt_ref, o_ref, odt_ref):
    h = (_rms(x_ref[...]) * g_ref[...]).astype(BF16)
    for j in range(PROJ_MAIN // INPROJ_TN):
        sl = slice(j * INPROJ_TN, (j + 1) * INPROJ_TN)
        o_ref[:, sl] = _dot(h, w_ref[:, sl]).astype(o_ref.dtype)
    odt_ref[...] = _dot(h, wdt_ref[...])


def _inproj(x, layer, pw, out_dtype, tm):
    t = x.shape[0]
    return pl.pallas_call(
        _inproj_kernel,
        grid=(t // tm,),
        in_specs=[
            pl.BlockSpec((tm, D_MODEL), lambda i: (i, 0)),
            _layer_spec((1, D_MODEL), layer),
            _layer_spec((D_MODEL, PROJ_MAIN), layer),
            _layer_spec((D_MODEL, LANES), layer),
        ],
        out_specs=[
            pl.BlockSpec((tm, PROJ_MAIN), lambda i: (i, 0)),
            pl.BlockSpec((tm, LANES), lambda i: (i, 0)),
        ],
        out_shape=[
            jax.ShapeDtypeStruct((t, PROJ_MAIN), out_dtype),
            jax.ShapeDtypeStruct((t, LANES), F32),
        ],
        compiler_params=_cparams(("parallel",)),
        name="inproj",
    )(x, pw["norm_mix"], pw["w_main"], pw["w_dt"])


def _dt_terms(dt_raw, dtb, alog, tri, expand):
    dt = _softplus(dt_raw + dtb)
    a = dt * (-jnp.exp(alog))
    acum = _sel_dot(tri, a)
    return dt, acum


def _ssd_out(y, xs, z, dskip, gain):
    y = (y + dskip * xs) * _silu(z)
    parts = []
    for g in range(SSD_GROUPS):
        parts.append(_rms(y[:, g * GROUP_WIDTH:(g + 1) * GROUP_WIDTH]))
    return jnp.concatenate(parts, axis=-1) * gain


def _rotary(x, cos, sin_signed):
    parts = []
    for h in range(RET_HEADS):
        xh = x[:, h * RET_QK_DIM:(h + 1) * RET_QK_DIM]
        parts.append(xh * cos + pltpu.roll(xh, RET_QK_DIM // 2, axis=1) * sin_signed)
    return parts


def _ret_out(o_heads, g, gain):
    o = jnp.concatenate([_rms(o) for o in o_heads], axis=-1)
    return o * gain * _silu(g)


def _mixer_prompt_kernel(proj_ref, dt_ref, *rest, nbb, ret_chunk_decay):
    params = rest[:14]
    y_ref, conv_ref, ssm_ref, ret_ref, xp_scr, st_scr = rest[-6:]
    for k in range(nbb):
        _mixer_prompt_chunk(proj_ref.at[k], dt_ref.at[k], *params, y_ref.at[k], conv_ref.at[k], ssm_ref.at[k],
                            ret_ref.at[k], xp_scr.at[k], st_scr.at[k], ret_chunk_decay=ret_chunk_decay)


def _mixer_prompt_chunk(proj_ref, dt_ref, convw_ref, convb_ref, dtb_ref, alog_ref, exp_ref, dskip_ref,
                        sgain_ref, rgain_ref, cos_ref, sin_ref, dmat_ref, kdec_ref, qdec_ref, shift_ref,
                        y_ref, conv_ref, ssm_ref, ret_ref, xp_scr, st_scr, *, ret_chunk_decay):
    cl = CHUNK
    c = pl.program_id(1)
    nc = pl.num_programs(1)

    @pl.when(c == 0)
    def _():
        xp_scr[...] = jnp.zeros_like(xp_scr)
        st_scr[...] = jnp.zeros_like(st_scr)
        ret_ref[...] = jnp.zeros_like(ret_ref)

    xbc_b = proj_ref[:, OFF_XBC:OFF_XBC + CONV_DIM]
    x_ext = jnp.concatenate(
        [xp_scr[c % 2], xbc_b, jnp.zeros((CONV_K - CONV_TAIL - cl, CONV_DIM), xbc_b.dtype)], axis=0)
    taps = _dot(shift_ref[...], x_ext)
    xbc_raw = xbc_b.astype(F32)
    acc = convb_ref[...] + xbc_raw * convw_ref[CONV_WIDTH - 1:CONV_WIDTH, :]
    for j in range(CONV_WIDTH - 1):
        acc = acc + taps[j * cl:(j + 1) * cl, :] * convw_ref[j:j + 1, :]
    conv_ref[...] = xbc_raw[cl - (CONV_WIDTH - 1):cl, :]
    xp_scr[(c + 1) % 2] = xbc_b[cl - CONV_TAIL:cl, :]
    xbc = _silu(acc)
    xs = xbc[:, 0:SSD_WIDTH]
    bm = xbc[:, SSD_WIDTH:SSD_WIDTH + SSD_GROUPS * SSD_STATE]
    cm = xbc[:, SSD_WIDTH + SSD_GROUPS * SSD_STATE:CONV_DIM]

    row = lax.broadcasted_iota(jnp.int32, (cl, cl), 0)
    col = lax.broadcasted_iota(jnp.int32, (cl, cl), 1)
    causal = row >= col
    tri = jnp.where(causal, 1.0, 0.0).astype(BF16)
    expand = exp_ref[...]
    dt, acum = _dt_terms(dt_ref[...], dtb_ref[...], alog_ref[...], tri, expand)
    acum_t = acum.T
    eacum = jnp.exp(acum)
    dt_x = _dot_sel(dt, expand)
    eacum_x = _dot_sel(eacum, expand)
    dte_x = _dot_sel(jnp.exp(acum[cl - 1:cl, :] - acum), expand)
    xdt = xs * dt_x
    xdt_b = xdt.astype(BF16)
    xdtd_b = (xdt * dte_x).astype(BF16)
    lane = lax.broadcasted_iota(jnp.int32, (cl, LANES), 1)

    y_parts = []
    upd_parts = []
    for g in range(SSD_GROUPS):
        bg = bm[:, g * SSD_STATE:(g + 1) * SSD_STATE]
        cg_b = cm[:, g * SSD_STATE:(g + 1) * SSD_STATE].astype(BF16)
        bg_b = bg.astype(BF16)
        cb = _dot_nt(cg_b, bg_b)
        gsl = slice(g * GROUP_WIDTH, (g + 1) * GROUP_WIDTH)
        y_off = _dot(cg_b, st_scr[:, gsl].astype(BF16)) * eacum_x[:, gsl]
        upd_parts.append(_dot(bg.T.astype(BF16), xdtd_b[:, gsl]))
        for j in range(GROUP_WIDTH // LANES):
            h0 = g * (SSD_HEADS // SSD_GROUPS) + 2 * j
            psl = slice(h0 * SSD_HEAD_DIM, (h0 + 2) * SSD_HEAD_DIM)
            res = []
            for hh in (h0, h0 + 1):
                seg = acum[:, hh:hh + 1] - acum_t[hh:hh + 1, :]
                lmat = jnp.exp(jnp.where(causal, seg, -jnp.inf))
                res.append(_dot((cb * lmat).astype(BF16), xdt_b[:, psl]))
            y_parts.append(jnp.where(lane < SSD_HEAD_DIM, res[0], res[1]) + y_off[:, j * LANES:(j + 1) * LANES])
    y = jnp.concatenate(y_parts, axis=-1)
    st_new = st_scr[...] * eacum_x[cl - 1:cl, :] + jnp.concatenate(upd_parts, axis=-1)
    st_scr[...] = st_new

    @pl.when(c == nc - 1)
    def _():
        ssm_ref[...] = st_new.T

    z = proj_ref[:, OFF_Z:OFF_Z + SSD_WIDTH].astype(F32)
    y_ref[:, 0:SSD_WIDTH] = _ssd_out(y, xs, z, dskip_ref[...], sgain_ref[...]).astype(y_ref.dtype)

    cos = cos_ref[...]
    sin = sin_ref[...]
    q_heads = _rotary(proj_ref[:, OFF_Q:OFF_Q + RET_QK_WIDTH].astype(F32), cos, sin)
    k_heads = _rotary(proj_ref[:, OFF_K:OFF_K + RET_QK_WIDTH].astype(F32), cos, sin)
    o_heads = []
    for h in range(RET_HEADS):
        qh = q_heads[h]
        kh = k_heads[h] * (RET_QK_DIM ** -0.5)
        vh = proj_ref[:, OFF_V + h * RET_V_DIM:OFF_V + (h + 1) * RET_V_DIM]
        hs = slice(h * RET_QK_DIM, (h + 1) * RET_QK_DIM)
        scores = _dot_nt(qh.astype(BF16), kh.astype(BF16)) * dmat_ref[h]
        o_intra = _dot(scores.astype(BF16), vh)
        s_in = ret_ref[h]
        o_cross = _dot((qh * qdec_ref[:, hs]).astype(BF16), s_in.astype(BF16))
        kv = _dot((kh * kdec_ref[:, hs]).T.astype(BF16), vh)
        ret_ref[h] = ret_chunk_decay[h] * s_in + kv
        o_heads.append(o_intra + o_cross)
    gate = proj_ref[:, OFF_G:OFF_G + RET_WIDTH].astype(F32)
    y_ref[:, SSD_WIDTH:SSD_WIDTH + RET_WIDTH] = _ret_out(o_heads, gate, rgain_ref[...]).astype(y_ref.dtype)


def _ret_tables(cl, reps):
    lg = np.log(1.0 - np.exp2(-5.0 - np.arange(RET_HEADS, dtype=np.float64)))
    idx = np.arange(cl, dtype=np.float64)
    rel = idx[:, None] - idx[None, :]
    dmat = np.where(rel[None] >= 0, np.exp(rel[None] * lg[:, None, None]), 0.0).astype(np.float32)
    kdec = np.exp((cl - 1 - idx)[:, None] * lg[None, :]).astype(np.float32)
    qdec = np.exp((idx + 1.0)[:, None] * lg[None, :]).astype(np.float32)
    kdec = np.tile(np.repeat(kdec, RET_QK_DIM, axis=1), (reps, 1))
    qdec = np.tile(np.repeat(qdec, RET_QK_DIM, axis=1), (reps, 1))
    chunk_decay = [float(v) for v in np.exp(cl * lg).astype(np.float32)]
    return dmat, kdec, qdec, chunk_decay


def _rope_tables(pos0, length, reps):
    half = RET_QK_DIM // 2
    inv = ROPE_BASE ** (-np.arange(half, dtype=np.float64) / half)
    pos = (pos0 + np.arange(length)).astype(np.float64)
    ang = pos[:, None] * inv[None, :]
    cos = np.cos(ang).astype(np.float32)
    sin = np.sin(ang).astype(np.float32)
    cos2 = np.tile(np.concatenate([cos, cos], axis=1), (reps, 1))
    sin2 = np.tile(np.concatenate([-sin, sin], axis=1), (reps, 1))
    return cos2, sin2


def _conv_shift_matrix():
    s = np.zeros(((CONV_WIDTH - 1) * CHUNK, CONV_K), np.float32)
    for j in range(CONV_WIDTH - 1):
        for t in range(CHUNK):
            s[j * CHUNK + t, CONV_TAIL + t - (CONV_WIDTH - 1) + j] = 1.0
    return s


def _expand_matrix():
    e = np.zeros((LANES, SSD_WIDTH), np.float32)
    for h in range(SSD_HEADS):
        e[h, h * SSD_HEAD_DIM:(h + 1) * SSD_HEAD_DIM] = 1.0
    return e


def _mixer_param_specs(layer):
    return [
        _layer_spec((CONV_WIDTH, CONV_DIM), layer),
        _layer_spec((1, CONV_DIM), layer),
        _layer_spec((1, LANES), layer),
        _layer_spec((1, LANES), layer),
        _const_spec((LANES, SSD_WIDTH)),
        _layer_spec((1, SSD_WIDTH), layer),
        _layer_spec((1, SSD_WIDTH), layer),
        _layer_spec((1, RET_WIDTH), layer),
    ]


def _mixer_params(pw):
    return (pw["conv_w"], pw["conv_b"], pw["dt_bias"], pw["a_log"], pw["expand"], pw["d_skip"],
            pw["ssd_gain"], pw["ret_gain"])


def _state_out_shapes(batch):
    return [
        jax.ShapeDtypeStruct((DEPTH, batch, CONV_WIDTH - 1, CONV_DIM), F32),
        jax.ShapeDtypeStruct((DEPTH, batch, SSD_WIDTH, SSD_STATE), F32),
        jax.ShapeDtypeStruct((DEPTH, batch, RET_HEADS, RET_QK_DIM, RET_V_DIM), F32),
    ]


def _mixer_prompt(layer, proj, dt_raw, pw, prev_states, batch, seq):
    nc = seq // CHUNK
    nbb = MIXER_SEQS_PER_STEP if batch % MIXER_SEQS_PER_STEP == 0 else 1
    dmat, kdec, qdec, chunk_decay = _ret_tables(CHUNK, 1)
    cos, sin = _rope_tables(0, seq, 1)
    seq_map = lambda b, c: (b, c, 0)
    kern = functools.partial(_mixer_prompt_kernel, nbb=nbb, ret_chunk_decay=chunk_decay)
    n_in = 16
    y, *states = pl.pallas_call(
        kern,
        grid=(batch // nbb, nc),
        in_specs=[
            pl.BlockSpec((nbb, CHUNK, PROJ_MAIN), seq_map),
            pl.BlockSpec((nbb, CHUNK, LANES), seq_map),
            *_mixer_param_specs(layer),
            pl.BlockSpec((CHUNK, LANES), lambda b, c: (c, 0)),
            pl.BlockSpec((CHUNK, LANES), lambda b, c: (c, 0)),
            _const_spec((RET_HEADS, CHUNK, CHUNK)),
            _const_spec((CHUNK, RET_QK_WIDTH)),
            _const_spec((CHUNK, RET_QK_WIDTH)),
            _const_spec(((CONV_WIDTH - 1) * CHUNK, CONV_K)),
            *[_alias_spec() for _ in prev_states],
        ],
        out_specs=[
            pl.BlockSpec((nbb, CHUNK, SSD_WIDTH + RET_WIDTH), seq_map),
            pl.BlockSpec((None, nbb, CONV_WIDTH - 1, CONV_DIM), lambda b, c: (layer, b, 0, 0)),
            pl.BlockSpec((None, nbb, SSD_WIDTH, SSD_STATE), lambda b, c: (layer, b, 0, 0)),
            pl.BlockSpec((None, nbb, RET_HEADS, RET_QK_DIM, RET_V_DIM), lambda b, c: (layer, b, 0, 0, 0)),
        ],
        out_shape=[jax.ShapeDtypeStruct((batch, seq, SSD_WIDTH + RET_WIDTH), BF16), *_state_out_shapes(batch)],
        input_output_aliases={n_in + k: 1 + k for k in range(len(prev_states))},
        scratch_shapes=[
            pltpu.VMEM((nbb, 2, CONV_TAIL, CONV_DIM), BF16),
            pltpu.VMEM((nbb, SSD_STATE, SSD_WIDTH), F32),
        ],
        compiler_params=_cparams(("parallel", "arbitrary")),
        name="mixer_prompt",
    )(proj.reshape(batch, seq, PROJ_MAIN), dt_raw.reshape(batch, seq, LANES), *_mixer_params(pw),
      jnp.asarray(cos), jnp.asarray(sin), jnp.asarray(dmat), jnp.asarray(kdec), jnp.asarray(qdec),
      jnp.asarray(_conv_shift_matrix(), dtype=BF16), *prev_states)
    return (y.reshape(batch * seq, SSD_WIDTH + RET_WIDTH), *states)


def _pad_rows(x, rows):
    return jnp.concatenate([x, jnp.zeros((rows - x.shape[0], x.shape[1]), x.dtype)], axis=0)


def _mixer_sample_kernel(proj_ref, dt_ref, convs_ref, ssm_in_ref, ret_in_ref, convw_ref, convb_ref, dtb_ref,
                         alog_ref, exp_ref, dskip_ref, sgain_ref, rgain_ref, cos_ref, sin_ref, dmat_ref,
                         kdec_ref, qdec_ref, tri_ref, *rest, bb, cl, ret_chunk_decay):
    y_ref, conv_ref, ssm_ref, ret_ref, xp_scr = rest[-5:]
    for i in range(bb):
        xp_scr[i, 8 - (CONV_WIDTH - 1):8, :] = convs_ref[i]
        xp_scr[i, 8:8 + cl, :] = proj_ref[i * cl:(i + 1) * cl, OFF_XBC:OFF_XBC + CONV_DIM]
        conv_ref[i] = xp_scr[i, 8 + cl - (CONV_WIDTH - 1):8 + cl, :]
    acc = None
    for j in range(CONV_WIDTH):
        s = 8 - (CONV_WIDTH - 1) + j
        tap = jnp.concatenate([xp_scr[i, s:s + cl, :] for i in range(bb)], axis=0) * convw_ref[j:j + 1, :]
        acc = convb_ref[...] + tap if acc is None else acc + tap
    xbc = _silu(acc)
    xs = xbc[:, 0:SSD_WIDTH]
    bm = xbc[:, SSD_WIDTH:SSD_WIDTH + SSD_GROUPS * SSD_STATE]
    cm = xbc[:, SSD_WIDTH + SSD_GROUPS * SSD_STATE:CONV_DIM]

    expand = exp_ref[...]
    dt, acum = _dt_terms(dt_ref[...], dtb_ref[...], alog_ref[...], tri_ref[...], expand)
    eacum = jnp.exp(acum)
    alast = jnp.concatenate(
        [jnp.broadcast_to(acum[(i + 1) * cl - 1:(i + 1) * cl, :], (cl, LANES)) for i in range(bb)], axis=0)
    dt_x = _dot_sel(dt, expand)
    eacum_x = _dot_sel(eacum, expand)
    dte_x = _dot_sel(jnp.exp(alast - acum), expand)
    xdt = xs * dt_x
    xdtd = xdt * dte_x
    row = lax.broadcasted_iota(jnp.int32, (cl, LANES), 0)
    col = lax.broadcasted_iota(jnp.int32, (cl, LANES), 1)
    causal = row >= col
    lane = col

    y_rows = []
    for i in range(bb):
        rs = slice(i * cl, (i + 1) * cl)
        acum_i = acum[rs]
        acum_t = _pad_rows(acum_i, LANES).T
        xdt_p = _pad_rows(xdt[rs], LANES).astype(BF16)
        xdtd_p = _pad_rows(xdtd[rs], LANES)
        chunk_decay = eacum[(i + 1) * cl - 1:(i + 1) * cl, :]
        y_parts = []
        for g in range(SSD_GROUPS):
            bg_p = _pad_rows(bm[rs, g * SSD_STATE:(g + 1) * SSD_STATE], LANES)
            cg_b = cm[rs, g * SSD_STATE:(g + 1) * SSD_STATE].astype(BF16)
            cb = _dot_nt(cg_b, bg_p.astype(BF16))
            gsl = slice(g * GROUP_WIDTH, (g + 1) * GROUP_WIDTH)
            st_g = ssm_in_ref[i, gsl, :]
            y_off = _dot_nt(cg_b, st_g.astype(BF16)) * eacum_x[rs, gsl]
            upd = _dot(xdtd_p[:, gsl].T.astype(BF16), bg_p.astype(BF16))
            for hh in range(SSD_HEADS // SSD_GROUPS):
                h = g * (SSD_HEADS // SSD_GROUPS) + hh
                hsl = slice(hh * SSD_HEAD_DIM, (hh + 1) * SSD_HEAD_DIM)
                ssm_ref[i, h * SSD_HEAD_DIM:(h + 1) * SSD_HEAD_DIM, :] = (
                    st_g[hsl, :] * chunk_decay[:, h:h + 1] + upd[hsl, :])
            for j in range(GROUP_WIDTH // LANES):
                h0 = g * (SSD_HEADS // SSD_GROUPS) + 2 * j
                psl = slice(h0 * SSD_HEAD_DIM, (h0 + 2) * SSD_HEAD_DIM)
                res = []
                for hh in (h0, h0 + 1):
                    seg = acum_i[:, hh:hh + 1] - acum_t[hh:hh + 1, :]
                    lmat = jnp.exp(jnp.where(causal, seg, -jnp.inf))
                    res.append(_dot((cb * lmat).astype(BF16), xdt_p[:, psl]))
                y_parts.append(jnp.where(lane < SSD_HEAD_DIM, res[0], res[1]) + y_off[:, j * LANES:(j + 1) * LANES])
        y_rows.append(jnp.concatenate(y_parts, axis=-1))
    y = jnp.concatenate(y_rows, axis=0)
    z = proj_ref[:, OFF_Z:OFF_Z + SSD_WIDTH]
    y_ref[:, 0:SSD_WIDTH] = _ssd_out(y, xs, z, dskip_ref[...], sgain_ref[...]).astype(y_ref.dtype)

    cos = cos_ref[...]
    sin = sin_ref[...]
    q_heads = _rotary(proj_ref[:, OFF_Q:OFF_Q + RET_QK_WIDTH], cos, sin)
    k_heads = _rotary(proj_ref[:, OFF_K:OFF_K + RET_QK_WIDTH], cos, sin)
    o_heads = []
    for h in range(RET_HEADS):
        hs = slice(h * RET_QK_DIM, (h + 1) * RET_QK_DIM)
        kh_all = k_heads[h] * (RET_QK_DIM ** -0.5)
        q_start = q_heads[h] * qdec_ref[:, hs]
        k_end = kh_all * kdec_ref[:, hs]
        o_rows = []
        for i in range(bb):
            rs = slice(i * cl, (i + 1) * cl)
            vh_p = _pad_rows(proj_ref[rs, OFF_V + h * RET_V_DIM:OFF_V + (h + 1) * RET_V_DIM], LANES).astype(BF16)
            kh_p = _pad_rows(kh_all[rs], LANES).astype(BF16)
            scores = _dot_nt(q_heads[h][rs].astype(BF16), kh_p) * dmat_ref[h]
            o_intra = _dot(scores.astype(BF16), vh_p)
            s_in = ret_in_ref[i, h]
            o_cross = _dot(q_start[rs].astype(BF16), s_in.astype(BF16))
            kv = _dot(_pad_rows(k_end[rs], LANES).T.astype(BF16), vh_p)
            ret_ref[i, h] = ret_chunk_decay[h] * s_in + kv
            o_rows.append(o_intra + o_cross)
        o_heads.append(jnp.concatenate(o_rows, axis=0))
    gate = proj_ref[:, OFF_G:OFF_G + RET_WIDTH]
    y_ref[:, SSD_WIDTH:SSD_WIDTH + RET_WIDTH] = _ret_out(o_heads, gate, rgain_ref[...]).astype(y_ref.dtype)


def _mixer_sample(layer, proj, dt_raw, conv_state, ssm_state, ret_state, pw, prev_states, batch, cl, bb):
    dmat, kdec, qdec, chunk_decay = _ret_tables(cl, bb)
    dmat = np.concatenate([dmat, np.zeros((RET_HEADS, cl, LANES - cl), np.float32)], axis=-1)
    cos, sin = _rope_tables(PAST_LEN, cl, bb)
    m = bb * cl
    tri = np.kron(np.eye(bb, dtype=np.float32), np.tril(np.ones((cl, cl), np.float32)))
    kern = functools.partial(_mixer_sample_kernel, bb=bb, cl=cl, ret_chunk_decay=chunk_decay)
    row_map = lambda i: (i, 0)
    state_specs = [
        pl.BlockSpec((None, bb, CONV_WIDTH - 1, CONV_DIM), lambda i: (layer, i, 0, 0)),
        pl.BlockSpec((None, bb, SSD_WIDTH, SSD_STATE), lambda i: (layer, i, 0, 0)),
        pl.BlockSpec((None, bb, RET_HEADS, RET_QK_DIM, RET_V_DIM), lambda i: (layer, i, 0, 0, 0)),
    ]
    n_in = 19
    return pl.pallas_call(
        kern,
        grid=(batch // bb,),
        in_specs=[
            pl.BlockSpec((m, PROJ_MAIN), row_map),
            pl.BlockSpec((m, LANES), row_map),
            *state_specs,
            *_mixer_param_specs(layer),
            _const_spec((m, LANES)),
            _const_spec((m, LANES)),
            _const_spec((RET_HEADS, cl, LANES)),
            _const_spec((m, RET_QK_WIDTH)),
            _const_spec((m, RET_QK_WIDTH)),
            _const_spec((m, m)),
            *[_alias_spec() for _ in prev_states],
        ],
        out_specs=[pl.BlockSpec((m, SSD_WIDTH + RET_WIDTH), row_map), *state_specs],
        out_shape=[jax.ShapeDtypeStruct((batch * cl, SSD_WIDTH + RET_WIDTH), BF16), *_state_out_shapes(batch)],
        input_output_aliases={n_in + k: 1 + k for k in range(len(prev_states))},
        scratch_shapes=[pltpu.VMEM((bb, 8 + cl, CONV_DIM), F32)],
        compiler_params=_cparams(("parallel",)),
        name="mixer_sample",
    )(proj, dt_raw, conv_state, ssm_state, ret_state, *_mixer_params(pw), jnp.asarray(cos), jnp.asarray(sin),
      jnp.asarray(dmat), jnp.asarray(kdec), jnp.asarray(qdec), jnp.asarray(tri, dtype=BF16), *prev_states)


def _outproj_kernel(x_ref, y_ref, w_ref, o_ref):
    o_ref[...] = x_ref[...] + _dot(y_ref[...], w_ref[...])


def _outproj(x, y, layer, pw, tm):
    t = x.shape[0]
    k = y.shape[1]
    return pl.pallas_call(
        _outproj_kernel,
        grid=(t // tm,),
        in_specs=[
            pl.BlockSpec((tm, D_MODEL), lambda i: (i, 0)),
            pl.BlockSpec((tm, k), lambda i: (i, 0)),
            _layer_spec((k, D_MODEL), layer),
        ],
        out_specs=pl.BlockSpec((tm, D_MODEL), lambda i: (i, 0)),
        out_shape=jax.ShapeDtypeStruct((t, D_MODEL), F32),
        compiler_params=_cparams(("parallel",)),
        name="outproj",
    )(x, y, pw["w_out"])


def _memproj_kernel(m_ref, w_ref, k_ref, v_ref, kb_ref, vb_ref, *, bb):
    r = _dot(m_ref[...].reshape(bb * N_MEM, D_MODEL).astype(BF16), w_ref[...])
    kb_ref[...] = r[:, 0:D_MODEL].astype(BF16).reshape(bb, N_MEM, D_MODEL)
    vb_ref[...] = r[:, D_MODEL:2 * D_MODEL].astype(BF16).reshape(bb, N_MEM, D_MODEL)
    for i in range(bb):
        rows = slice(i * N_MEM, (i + 1) * N_MEM)
        for hd in range(XA_HEADS):
            k_ref[i, :, hd, :] = r[rows, hd * XA_HEAD_DIM:(hd + 1) * XA_HEAD_DIM]
            v_ref[i, :, hd, :] = r[rows, D_MODEL + hd * XA_HEAD_DIM:D_MODEL + (hd + 1) * XA_HEAD_DIM]


def _memproj(mem, pw, bb):
    b = mem.shape[0]
    cache_spec = pl.BlockSpec((None, bb, N_MEM, XA_HEADS, XA_HEAD_DIM), lambda l, i: (l, i, 0, 0, 0))
    cache_shape = jax.ShapeDtypeStruct((DEPTH, b, N_MEM, XA_HEADS, XA_HEAD_DIM), F32)
    flat_spec = pl.BlockSpec((None, bb, N_MEM, D_MODEL), lambda l, i: (l, i, 0, 0))
    flat_shape = jax.ShapeDtypeStruct((DEPTH, b, N_MEM, D_MODEL), BF16)
    return pl.pallas_call(
        functools.partial(_memproj_kernel, bb=bb),
        grid=(DEPTH, b // bb),
        in_specs=[
            pl.BlockSpec((bb, N_MEM, D_MODEL), lambda l, i: (i, 0, 0)),
            pl.BlockSpec((None, D_MODEL, 2 * D_MODEL), lambda l, i: (l, 0, 0)),
        ],
        out_specs=[cache_spec, cache_spec, flat_spec, flat_spec],
        out_shape=[cache_shape, cache_shape, flat_shape, flat_shape],
        compiler_params=_cparams(("parallel", "parallel")),
        name="memproj",
    )(mem, pw["w_kv"])


def _softmax_rows(s):
    e = jnp.exp(s - jnp.max(s, axis=-1, keepdims=True))
    return e / jnp.sum(e, axis=-1, keepdims=True)


def _xattn_prompt_kernel(x_ref, g_ref, wq_ref, wo_ref, k_ref, v_ref, o_ref):
    x = x_ref[0]
    h = (_rms(x) * g_ref[...]).astype(BF16)
    q = _dot(h, wq_ref[...])
    scale = XA_HEAD_DIM ** -0.5
    heads = []
    for hd in range(XA_HEADS):
        hs = slice(hd * XA_HEAD_DIM, (hd + 1) * XA_HEAD_DIM)
        p = _softmax_rows(_dot_nt(q[:, hs].astype(BF16), k_ref[0, :, hs]) * scale)
        heads.append(_dot(p.astype(BF16), v_ref[0, :, hs]))
    o = jnp.concatenate(heads, axis=-1).astype(BF16)
    o_ref[0] = x + _dot(o, wo_ref[...])


def _xattn_cache_kernel(x_ref, g_ref, wq_ref, wo_ref, k_ref, v_ref, o_ref, *, bb, tm):
    x = x_ref[...].reshape(bb * tm, D_MODEL)
    h = (_rms(x) * g_ref[...]).astype(BF16)
    q = _dot(h, wq_ref[...])
    scale = XA_HEAD_DIM ** -0.5
    rows_flat = XA_HEADS * tm
    assert tm & (tm - 1) == 0 and XA_HEADS & (XA_HEADS - 1) == 0
    row_head = lax.shift_right_logical(lax.broadcasted_iota(jnp.int32, (rows_flat, N_MEM * XA_HEADS), 0),
                                       int(math.log2(tm)))
    col_head = lax.broadcasted_iota(jnp.int32, (rows_flat, N_MEM * XA_HEADS), 1) & (XA_HEADS - 1)
    own_head = row_head == col_head
    outs = []
    for i in range(bb):
        qi = q[i * tm:(i + 1) * tm]
        qf = jnp.concatenate([qi[:, hd * XA_HEAD_DIM:(hd + 1) * XA_HEAD_DIM] for hd in range(XA_HEADS)], axis=0)
        k_all = k_ref[i].reshape(N_MEM * XA_HEADS, XA_HEAD_DIM).astype(BF16)
        v_all = v_ref[i].reshape(N_MEM * XA_HEADS, XA_HEAD_DIM).astype(BF16)
        s = jnp.where(own_head, _dot_nt(qf.astype(BF16), k_all) * scale, -jnp.inf)
        of = _dot(_softmax_rows(s).astype(BF16), v_all)
        outs.append(jnp.concatenate([of[hd * tm:(hd + 1) * tm] for hd in range(XA_HEADS)], axis=-1))
    o = jnp.concatenate(outs, axis=0).astype(BF16)
    o_ref[...] = (x + _dot(o, wo_ref[...])).reshape(bb, tm, D_MODEL)


def _xattn_weight_specs(layer):
    return [_layer_spec((1, D_MODEL), layer), _layer_spec((D_MODEL, D_MODEL), layer),
            _layer_spec((D_MODEL, D_MODEL), layer)]


def _xattn_prompt(x, layer, pw, mem_k, mem_v, tm):
    b, l, _ = x.shape
    mem_spec = pl.BlockSpec((None, 1, N_MEM, D_MODEL), lambda i, j: (layer, i, 0, 0))
    return pl.pallas_call(
        _xattn_prompt_kernel,
        grid=(b, l // tm),
        in_specs=[pl.BlockSpec((1, tm, D_MODEL), lambda i, j: (i, j, 0)), *_xattn_weight_specs(layer),
                  mem_spec, mem_spec],
        out_specs=pl.BlockSpec((1, tm, D_MODEL), lambda i, j: (i, j, 0)),
        out_shape=jax.ShapeDtypeStruct((b, l, D_MODEL), F32),
        compiler_params=_cparams(("parallel", "parallel")),
        name="xattn_prompt",
    )(x, pw["norm_mem"], pw["w_mq"], pw["w_mo"], mem_k, mem_v)


def _xattn_cache(x, layer, pw, mem_k, mem_v, bb):
    b, l, _ = x.shape
    kern = functools.partial(_xattn_cache_kernel, bb=bb, tm=l)
    mem_spec = pl.BlockSpec((None, bb, N_MEM, XA_HEADS, XA_HEAD_DIM), lambda i: (layer, i, 0, 0, 0))
    return pl.pallas_call(
        kern,
        grid=(b // bb,),
        in_specs=[pl.BlockSpec((bb, l, D_MODEL), lambda i: (i, 0, 0)), *_xattn_weight_specs(layer),
                  mem_spec, mem_spec],
        out_specs=pl.BlockSpec((bb, l, D_MODEL), lambda i: (i, 0, 0)),
        out_shape=jax.ShapeDtypeStruct((b, l, D_MODEL), F32),
        compiler_params=_cparams(("parallel",)),
        name="xattn_cache",
    )(x, pw["norm_mem"], pw["w_mq"], pw["w_mo"], mem_k, mem_v)


def _router_gates(logits):
    m = logits.shape[0]
    lane_i = lax.broadcasted_iota(jnp.int32, (m, LANES), 1)
    lane = lane_i.astype(F32)
    big = float(LANES)
    is_g = lane_i < N_EGROUPS
    gl = jnp.where(is_g, logits, -jnp.inf)
    gmax = jnp.max(gl, axis=-1, keepdims=True)
    g_idx = jnp.min(jnp.where(is_g & (gl == gmax), lane, big), axis=-1, keepdims=True)
    g_prob = 1.0 / jnp.sum(jnp.exp(gl - gmax), axis=-1, keepdims=True)
    e_lane = lane_i - ROUTER_OFF
    e_group = lax.shift_right_arithmetic(e_lane, int(math.log2(EXPERTS_PER_GROUP))).astype(F32)
    sel = (e_lane >= 0) & (e_lane < N_EXPERTS) & (e_group == g_idx)
    el = jnp.where(sel, logits, -jnp.inf)
    emax = jnp.max(el, axis=-1, keepdims=True)
    ee = jnp.exp(el - emax)
    e_prob = ee / jnp.sum(ee, axis=-1, keepdims=True)
    p1 = jnp.max(jnp.where(sel, e_prob, -1.0), axis=-1, keepdims=True)
    i1 = jnp.min(jnp.where(sel & (e_prob == p1), lane, big), axis=-1, keepdims=True)
    sel2 = sel & (lane != i1)
    p2 = jnp.max(jnp.where(sel2, e_prob, -1.0), axis=-1, keepdims=True)
    i2 = jnp.min(jnp.where(sel2 & (e_prob == p2), lane, big), axis=-1, keepdims=True)
    denom = p1 + p2
    w1 = g_prob * p1 / denom
    w2 = g_prob * p2 / denom
    return jnp.where(lane == i1, w1, 0.0) + jnp.where(lane == i2, w2, 0.0), g_idx


def _moe_kernel(x_ref, g_ref, wr_hi_ref, wr_lo_ref, br_ref, wg_ref, wu_ref, wd_ref, gf_ref, o_ref,
                h_scr, gate_scr, *, final_norm):
    e = pl.program_id(1)

    @pl.when(e == 0)
    def _():
        h = _rms(x_ref[...]) * g_ref[...]
        h_hi = h.astype(BF16)
        h_lo = (h - h_hi.astype(F32)).astype(BF16)
        logits = (_dot(h_hi, wr_hi_ref[...]) + _dot(h_hi, wr_lo_ref[...]) + _dot(h_lo, wr_hi_ref[...])
                  + br_ref[...])
        h_scr[...] = h_hi
        gate_scr[...] = _router_gates(logits)[0]
        o_ref[...] = jnp.zeros_like(o_ref)

    hb = h_scr[...]
    lane = lax.broadcasted_iota(jnp.int32, gate_scr.shape, 1)
    gate_e = jnp.sum(jnp.where(lane == e + ROUTER_OFF, gate_scr[...], 0.0), axis=-1, keepdims=True)
    a = _silu(_dot(hb, wg_ref[...])) * _dot(hb, wu_ref[...]) * gate_e
    o_ref[...] += _dot(a.astype(BF16), wd_ref[...])

    @pl.when(e == pl.num_programs(1) - 1)
    def _():
        y = x_ref[...] + o_ref[...]
        if final_norm:
            y = _rms(y) * gf_ref[...]
        o_ref[...] = y


def _moe(x, layer, pw, final_norm, tm):
    t = x.shape[0]
    kern = functools.partial(_moe_kernel, final_norm=final_norm)
    return pl.pallas_call(
        kern,
        grid=(t // tm, N_EXPERTS),
        in_specs=[
            pl.BlockSpec((tm, D_MODEL), lambda i, e: (i, 0)),
            _layer_spec((1, D_MODEL), layer),
            _layer_spec((D_MODEL, LANES), layer),
            _layer_spec((D_MODEL, LANES), layer),
            _layer_spec((1, LANES), layer),
            pl.BlockSpec((None, None, D_MODEL, EXPERT_FF), lambda i, e: (layer, e, 0, 0)),
            pl.BlockSpec((None, None, D_MODEL, EXPERT_FF), lambda i, e: (layer, e, 0, 0)),
            pl.BlockSpec((None, None, EXPERT_FF, D_MODEL), lambda i, e: (layer, e, 0, 0)),
            _const_spec((1, D_MODEL)),
        ],
        out_specs=pl.BlockSpec((tm, D_MODEL), lambda i, e: (i, 0)),
        out_shape=jax.ShapeDtypeStruct((t, D_MODEL), F32),
        scratch_shapes=[pltpu.VMEM((tm, D_MODEL), BF16), pltpu.VMEM((tm, LANES), F32)],
        compiler_params=_cparams(("parallel", "arbitrary")),
        name="moe",
    )(x, pw["norm_ffn"], pw["wr_hi"], pw["wr_lo"], pw["b_router"], pw["w_gate"], pw["w_up"], pw["w_down"],
      pw["norm_final"])


MOE_TS = 2048
MOE_M = 256
MOE_NB = MOE_TS // MOE_M + N_EGROUPS - 1
MOE_R = MOE_NB * MOE_M
SUBLANES = 8
TOKEN_TILE = (SUBLANES, D_MODEL // SUBLANES)
PLAN_BLK = 512


def _moe_route_kernel(x_ref, g_ref, wr_hi_ref, wr_lo_ref, br_ref, h_ref, info_ref):
    h = _rms(x_ref[...]) * g_ref[...]
    h_hi = h.astype(BF16)
    h_lo = (h - h_hi.astype(F32)).astype(BF16)
    logits = (_dot(h_hi, wr_hi_ref[...]) + _dot(h_hi, wr_lo_ref[...]) + _dot(h_lo, wr_hi_ref[...])
              + br_ref[...])
    gates, g_idx = _router_gates(logits)
    lane = lax.broadcasted_iota(jnp.int32, gates.shape, 1)
    info_ref[...] = jnp.where(lane == 0, g_idx, gates)
    h_ref[...] = h.reshape(h.shape[0], *TOKEN_TILE)


def _moe_route(x, layer, pw, tm):
    t = x.shape[0]
    return pl.pallas_call(
        _moe_route_kernel,
        grid=(t // tm,),
        in_specs=[
            pl.BlockSpec((tm, D_MODEL), lambda i: (i, 0)),
            _layer_spec((1, D_MODEL), layer),
            _layer_spec((D_MODEL, LANES), layer),
            _layer_spec((D_MODEL, LANES), layer),
            _layer_spec((1, LANES), layer),
        ],
        out_specs=[pl.BlockSpec((tm, *TOKEN_TILE), lambda i: (i, 0, 0)),
                   pl.BlockSpec((tm, LANES), lambda i: (i, 0))],
        out_shape=[jax.ShapeDtypeStruct((t, *TOKEN_TILE), F32), jax.ShapeDtypeStruct((t, LANES), F32)],
        compiler_params=_cparams(("parallel",)),
        name="moe_route",
    )(x, pw["norm_ffn"], pw["wr_hi"], pw["wr_lo"], pw["b_router"])


def _lane_pick(v, idx):
    lane = lax.broadcasted_iota(jnp.int32, v.shape, 1)
    return jnp.sum(jnp.where(lane == idx, v, 0.0), axis=-1, keepdims=True)


def _moe_plan_kernel(info_ref, dest_ref, items_ref, *, n_tiles):
    row = lax.broadcasted_iota(jnp.int32, (PLAN_BLK, PLAN_BLK), 0)
    col = lax.broadcasted_iota(jnp.int32, (PLAN_BLK, PLAN_BLK), 1)
    tri = jnp.where(row >= col, 1.0, 0.0).astype(BF16)
    row = lax.broadcasted_iota(jnp.int32, (LANES, LANES), 0)
    col = lax.broadcasted_iota(jnp.int32, (LANES, LANES), 1)
    before = jnp.where(row < col, 1.0, 0.0).astype(BF16)
    lane = lax.broadcasted_iota(jnp.int32, (PLAN_BLK, LANES), 1)
    lane_row = lax.broadcasted_iota(jnp.int32, (1, LANES), 1)
    block_groups = jnp.zeros((1, LANES), F32)
    for s in range(n_tiles):
        carry = jnp.zeros((1, LANES), F32)
        parts = []
        for blk in range(MOE_TS // PLAN_BLK):
            r0 = s * MOE_TS + blk * PLAN_BLK
            gid = info_ref[r0:r0 + PLAN_BLK, 0:1]
            onehot = jnp.where((lane < N_EGROUPS) & (lane.astype(F32) == gid), 1.0, 0.0)
            cum = _dot(tri, onehot.astype(BF16)) + carry
            carry = cum[PLAN_BLK - 1:PLAN_BLK, :]
            parts.append((onehot, cum))
        counts = carry
        padded = jnp.ceil(counts * (1.0 / MOE_M)) * MOE_M
        seg_start = _dot_sel(jnp.broadcast_to(padded, (8, LANES)), before)[0:1, :]
        rows = []
        for onehot, cum in parts:
            dest_col = jnp.sum(onehot * (seg_start + cum - 1.0), axis=-1, keepdims=True)
            for q in range(PLAN_BLK // LANES):
                piece = jnp.broadcast_to(dest_col[q * LANES:(q + 1) * LANES, :], (LANES, LANES))
                rows.append(piece.T[0:1, :])
        n_rows = MOE_TS // LANES
        dest_ref[s * n_rows:(s + 1) * n_rows, :] = jnp.concatenate(rows, axis=0).astype(jnp.int32)
        blk_start = lane_row.astype(F32) * MOE_M
        group_of_block = jnp.full((1, LANES), -1.0, F32)
        for g in range(N_EGROUPS):
            start_g = _lane_pick(seg_start, g)
            size_g = _lane_pick(padded, g)
            group_of_block = jnp.where((blk_start >= start_g) & (blk_start < start_g + size_g), float(g),
                                       group_of_block)
        in_tile = jnp.where(lane_row < MOE_NB, group_of_block + 1.0, 0.0)
        if s:
            in_tile = pltpu.roll(jnp.broadcast_to(in_tile, (SUBLANES, LANES)), s * MOE_NB, axis=1)[0:1]
        block_groups = block_groups + in_tile

    n_items = n_tiles * MOE_NB
    group = block_groups - 1.0
    lane_f = lane_row.astype(F32)
    order_key = jnp.where(group >= 0, group, float(N_EGROUPS)) * LANES + lane_f
    order_key = jnp.where(lane_row < n_items, order_key, float((N_EGROUPS + 1) * LANES) + lane_f)
    key_by_lane = jnp.broadcast_to(order_key, (LANES, LANES))
    key_by_row = key_by_lane.T
    rank = jnp.sum(jnp.where(key_by_lane < key_by_row, 1.0, 0.0), axis=-1, keepdims=True)
    chosen = rank == lane_f
    block_id = lax.broadcasted_iota(jnp.int32, (LANES, LANES), 0).astype(F32)
    group_by_row = jnp.broadcast_to(group, (LANES, LANES)).T
    item_block = jnp.sum(jnp.where(chosen, block_id, 0.0), axis=0, keepdims=True)
    item_group = jnp.sum(jnp.where(chosen, group_by_row, 0.0), axis=0, keepdims=True)
    item_group = jnp.where(item_group >= 0, item_group, -float(N_EGROUPS))
    items_ref[0:1, :] = item_block.astype(jnp.int32)
    items_ref[1:2, :] = item_group.astype(jnp.int32)


def _moe_plan(info, n_tiles):
    t = info.shape[0]
    assert n_tiles * MOE_NB <= LANES
    return pl.pallas_call(
        functools.partial(_moe_plan_kernel, n_tiles=n_tiles),
        grid=(1,),
        in_specs=[pl.BlockSpec((t, LANES), lambda i: (0, 0))],
        out_specs=[pl.BlockSpec((t // LANES, LANES), lambda i: (0, 0)),
                   pl.BlockSpec((2, LANES), lambda i: (0, 0))],
        out_shape=[jax.ShapeDtypeStruct((t // LANES, LANES), jnp.int32),
                   jax.ShapeDtypeStruct((2, LANES), jnp.int32)],
        compiler_params=_cparams(("arbitrary",)),
        name="moe_plan",
    )(info)


def _dest_row(dest_ref, token):
    return dest_ref[token]


def _moe_permute_kernel(dest_ref, h_ref, info_ref, hs_ref, infos_ref):
    base = pl.program_id(0) * MOE_TS
    hs_ref[...] = jnp.zeros_like(hs_ref)
    infos_ref[...] = jnp.zeros_like(infos_ref)

    def body(t, carry):
        d = _dest_row(dest_ref, base + t)
        hs_ref[d] = h_ref[t]
        infos_ref[pl.ds(d, 1), :] = info_ref[pl.ds(t, 1), :]
        return carry

    lax.fori_loop(0, MOE_TS, body, 0, unroll=8)


def _moe_permute(dest, h, info, n_tiles):
    return pl.pallas_call(
        _moe_permute_kernel,
        grid_spec=pltpu.PrefetchScalarGridSpec(
            num_scalar_prefetch=1,
            grid=(n_tiles,),
            in_specs=[pl.BlockSpec((MOE_TS, *TOKEN_TILE), lambda s, d: (s, 0, 0)),
                      pl.BlockSpec((MOE_TS, LANES), lambda s, d: (s, 0))],
            out_specs=[pl.BlockSpec((MOE_R, *TOKEN_TILE), lambda s, d: (s, 0, 0)),
                       pl.BlockSpec((MOE_R, LANES), lambda s, d: (s, 0))],
        ),
        out_shape=[jax.ShapeDtypeStruct((n_tiles * MOE_R, *TOKEN_TILE), F32),
                   jax.ShapeDtypeStruct((n_tiles * MOE_R, LANES), F32)],
        compiler_params=_cparams(("arbitrary",)),
        name="moe_permute",
    )(dest, h, info)


def _moe_ffn_kernel(item_block_ref, item_group_ref, x_ref, gates_ref, wg_ref, wu_ref, wd_ref, y_ref):
    group = item_group_ref[pl.program_id(0)]

    @pl.when(group >= 0)
    def _():
        xb = x_ref[...].reshape(MOE_M, D_MODEL).astype(BF16)
        gates = gates_ref[...]
        lane = lax.broadcasted_iota(jnp.int32, gates.shape, 1)
        y = None
        for e in range(EXPERTS_PER_GROUP):
            gate_e = jnp.sum(jnp.where(lane == ROUTER_OFF + group * EXPERTS_PER_GROUP + e, gates, 0.0),
                             axis=-1, keepdims=True)
            a = _silu(_dot(xb, wg_ref[e])) * _dot(xb, wu_ref[e]) * gate_e
            d = _dot(a.astype(BF16), wd_ref[e])
            y = d if y is None else y + d
        y_ref[...] = y.reshape(MOE_M, *TOKEN_TILE)

    @pl.when(group < 0)
    def _():
        y_ref[...] = jnp.zeros_like(y_ref)


def _moe_ffn(item_block, item_group, hs, infos, layer, pw, n_tiles):
    def w_map(w, blk, grp):
        g = grp[w]
        return (layer, jnp.where(g >= 0, g, -1 - g), 0, 0, 0)

    def grouped(w):
        return w.reshape(DEPTH, N_EGROUPS, EXPERTS_PER_GROUP, *w.shape[2:])

    tile_spec = pl.BlockSpec((MOE_M, *TOKEN_TILE), lambda w, blk, grp: (blk[w], 0, 0))
    return pl.pallas_call(
        _moe_ffn_kernel,
        grid_spec=pltpu.PrefetchScalarGridSpec(
            num_scalar_prefetch=2,
            grid=(n_tiles * MOE_NB,),
            in_specs=[
                tile_spec,
                pl.BlockSpec((MOE_M, LANES), lambda w, blk, grp: (blk[w], 0)),
                pl.BlockSpec((None, None, EXPERTS_PER_GROUP, D_MODEL, EXPERT_FF), w_map),
                pl.BlockSpec((None, None, EXPERTS_PER_GROUP, D_MODEL, EXPERT_FF), w_map),
                pl.BlockSpec((None, None, EXPERTS_PER_GROUP, EXPERT_FF, D_MODEL), w_map),
            ],
            out_specs=tile_spec,
        ),
        out_shape=jax.ShapeDtypeStruct((n_tiles * MOE_R, *TOKEN_TILE), F32),
        compiler_params=_cparams(("arbitrary",)),
        name="moe_ffn",
    )(item_block, item_group, hs, infos, grouped(pw["w_gate"]), grouped(pw["w_up"]), grouped(pw["w_down"]))


def _moe_unpermute_kernel(dest_ref, x_ref, ys_ref, gf_ref, o_ref, y_scr, *, tm, final_norm):
    base = pl.program_id(0) * MOE_TS + pl.program_id(1) * tm

    def body(t, carry):
        y_scr[t] = ys_ref[_dest_row(dest_ref, base + t)]
        return carry

    lax.fori_loop(0, tm, body, 0, unroll=8)
    y = x_ref[...] + y_scr[...].reshape(tm, D_MODEL)
    if final_norm:
        y = _rms(y) * gf_ref[...]
    o_ref[...] = y


def _moe_unpermute(dest, x, ys, pw, final_norm, n_tiles, tm):
    per_tile = MOE_TS // tm
    return pl.pallas_call(
        functools.partial(_moe_unpermute_kernel, tm=tm, final_norm=final_norm),
        grid_spec=pltpu.PrefetchScalarGridSpec(
            num_scalar_prefetch=1,
            grid=(n_tiles, per_tile),
            in_specs=[
                pl.BlockSpec((tm, D_MODEL), lambda s, i, d: (s * per_tile + i, 0)),
                pl.BlockSpec((MOE_R, *TOKEN_TILE), lambda s, i, d: (s, 0, 0)),
                pl.BlockSpec((1, D_MODEL), lambda s, i, d: (0, 0), pipeline_mode=pl.Buffered(1)),
            ],
            out_specs=pl.BlockSpec((tm, D_MODEL), lambda s, i, d: (s * per_tile + i, 0)),
            scratch_shapes=[pltpu.VMEM((tm, *TOKEN_TILE), F32)],
        ),
        out_shape=jax.ShapeDtypeStruct(x.shape, F32),
        compiler_params=_cparams(("arbitrary", "arbitrary")),
        name="moe_unpermute",
    )(dest, x, ys, pw["norm_final"])


def _moe_sorted(x, layer, pw, final_norm):
    n_tiles = x.shape[0] // MOE_TS
    h, info = _moe_route(x, layer, pw, 1024)
    dest, items = _moe_plan(info, n_tiles)
    dest = dest.reshape(-1)
    hs, infos = _moe_permute(dest, h, info, n_tiles)
    ys = _moe_ffn(items[0], items[1], hs, infos, layer, pw, n_tiles)
    return _moe_unpermute(dest, x, ys, pw, final_norm, n_tiles, 1024)


def _row(v):
    return v.reshape(v.shape[0], 1, v.shape[1])


def _pad_lanes(v):
    return jnp.pad(v, ((0, 0),) * (v.ndim - 1) + ((0, LANES - v.shape[-1]),))


def _prep_weights(norm_mix, w_in, conv_w, conv_b, dt_bias, a_log, d_skip, ssd_gain, ret_gain, w_out,
                  norm_mem, w_mq, w_mk, w_mv, w_mo, norm_ffn, w_rg, b_rg, w_re, b_re, w_gate, w_up, w_down,
                  norm_final):
    dt_off = SSD_WIDTH + CONV_DIM
    w_main = jnp.concatenate([w_in[:, :, :dt_off], w_in[:, :, dt_off + SSD_HEADS:]], axis=2).astype(BF16)
    w_dt = _pad_lanes(w_in[:, :, dt_off:dt_off + SSD_HEADS]).astype(BF16)
    w_router = _pad_lanes(jnp.concatenate([w_rg, w_re], axis=2))
    wr_hi = w_router.astype(BF16)
    wr_lo = (w_router - wr_hi.astype(F32)).astype(BF16)
    return dict(
        norm_mix=_row(norm_mix), w_main=w_main, w_dt=w_dt,
        conv_w=conv_w, conv_b=_row(conv_b),
        dt_bias=_row(_pad_lanes(dt_bias)), a_log=_row(_pad_lanes(a_log)),
        expand=jnp.asarray(_expand_matrix(), dtype=BF16),
        d_skip=_row(jnp.repeat(d_skip, SSD_HEAD_DIM, axis=1)),
        ssd_gain=_row(ssd_gain), ret_gain=_row(ret_gain),
        w_out=w_out.astype(BF16),
        norm_mem=_row(norm_mem), w_mq=w_mq.astype(BF16), w_mo=w_mo.astype(BF16),
        w_kv=jnp.concatenate([w_mk, w_mv], axis=2).astype(BF16),
        norm_ffn=_row(norm_ffn), wr_hi=wr_hi, wr_lo=wr_lo,
        b_router=_row(_pad_lanes(jnp.concatenate([b_rg, b_re], axis=1))),
        w_gate=w_gate.astype(BF16), w_up=w_up.astype(BF16), w_down=w_down.astype(BF16),
        norm_final=norm_final.reshape(1, -1),
    )


def _token_tile(t, cap):
    tm = min(t, cap)
    assert t % tm == 0
    return tm


def _trunk(x, mixer_fn, xattn_fn, pw, proj_dtype):
    b, l, _ = x.shape
    t = b * l
    xf = x.reshape(t, D_MODEL)
    states = ()
    for layer in range(DEPTH):
        proj, dt_raw = _inproj(xf, layer, pw, proj_dtype, _token_tile(t, 512))
        y, *states = mixer_fn(layer, proj, dt_raw, tuple(states))
        xf = _outproj(xf, y, layer, pw, _token_tile(t, 1024))
        xf = xattn_fn(layer, xf.reshape(b, l, D_MODEL)).reshape(t, D_MODEL)
        if t % MOE_TS == 0:
            xf = _moe_sorted(xf, layer, pw, layer == DEPTH - 1)
        else:
            xf = _moe(xf, layer, pw, layer == DEPTH - 1, _token_tile(t, 1024))
    conv, ssm, ret = states
    return (xf.reshape(b, l, D_MODEL), ssm.reshape(DEPTH, b, SSD_HEADS, SSD_HEAD_DIM, SSD_STATE), conv, ret)


def kernel(x_prompt, x_sample, mem_prompt, state_ssm, state_conv, state_ret, cache_mem_k, cache_mem_v,
           norm_mix, w_in, conv_w, conv_b, dt_bias, a_log, d_skip, ssd_gain, ret_gain, w_out,
           norm_mem, w_mq, w_mk, w_mv, w_mo, norm_ffn, w_rg, b_rg, w_re, b_re, w_gate, w_up, w_down,
           norm_final):
    pw = _prep_weights(norm_mix, w_in, conv_w, conv_b, dt_bias, a_log, d_skip, ssd_gain, ret_gain, w_out,
                       norm_mem, w_mq, w_mk, w_mv, w_mo, norm_ffn, w_rg, b_rg, w_re, b_re, w_gate, w_up, w_down,
                       norm_final)
    bp, lp, _ = x_prompt.shape
    bs, ls, _ = x_sample.shape
    n_mem = mem_prompt.shape[1]

    assert n_mem == N_MEM
    mem_k_p, mem_v_p, mem_k_rows, mem_v_rows = _memproj(mem_prompt, pw, _token_tile(bp, 2))

    def mixer_p(layer, proj, dt_raw, prev_states):
        return _mixer_prompt(layer, proj, dt_raw, pw, prev_states, bp, lp)

    def xattn_p(layer, x):
        return _xattn_prompt(x, layer, pw, mem_k_rows, mem_v_rows, _token_tile(lp, 1024))

    y_prompt, ssm_p, conv_p, ret_p = _trunk(x_prompt, mixer_p, xattn_p, pw, BF16)

    sample_bb = 8
    ssm_in = state_ssm.reshape(DEPTH, bs, SSD_WIDTH, SSD_STATE)

    def mixer_s(layer, proj, dt_raw, prev_states):
        return _mixer_sample(layer, proj, dt_raw, state_conv, ssm_in, state_ret, pw, prev_states, bs, ls,
                             sample_bb)

    def xattn_s(layer, x):
        return _xattn_cache(x, layer, pw, cache_mem_k, cache_mem_v, sample_bb)

    y_sample, ssm_s, conv_s, ret_s = _trunk(x_sample, mixer_s, xattn_s, pw, F32)
    return (y_prompt, y_sample, ssm_p, conv_p, ret_p, mem_k_p, mem_v_p, ssm_s, conv_s, ret_s)
```

```python
import functools
import math

import numpy as np
import jax
import jax.numpy as jnp
from jax import lax
from jax.experimental import pallas as pl
from jax.experimental.pallas import tpu as pltpu

F32 = jnp.float32
BF16 = jnp.bfloat16

D_MODEL = 1024
DEPTH = 2
PAST_LEN = 16384
SSD_HEAD_DIM = 64
SSD_HEADS = 16
SSD_GROUPS = 2
SSD_STATE = 128
SSD_WIDTH = 1024
GROUP_WIDTH = SSD_WIDTH // SSD_GROUPS
CONV_WIDTH = 4
CONV_DIM = SSD_WIDTH + 2 * SSD_GROUPS * SSD_STATE
RET_HEADS = 4
RET_V_DIM = 256
RET_QK_DIM = 128
RET_WIDTH = 1024
RET_QK_WIDTH = RET_HEADS * RET_QK_DIM
ROPE_BASE = 10000.0
N_MEM = 256
XA_HEADS = 4
XA_HEAD_DIM = 256
N_EGROUPS = 4
EXPERTS_PER_GROUP = 4
N_EXPERTS = 16
EXPERT_FF = 512
RMS_EPS = 1e-6

LANES = 128
CHUNK = 128
OFF_Z = 0
OFF_XBC = OFF_Z + SSD_WIDTH
OFF_Q = OFF_XBC + CONV_DIM
OFF_K = OFF_Q + RET_QK_WIDTH
OFF_V = OFF_K + RET_QK_WIDTH
OFF_G = OFF_V + RET_WIDTH
PROJ_MAIN = OFF_G + RET_WIDTH
INPROJ_TN = 512
CONV_TAIL = 16
MIXER_SEQS_PER_STEP = 2
CONV_K = 256
ROUTER_OFF = N_EGROUPS

VMEM_LIMIT = 56 * 1024 * 1024


def _cparams(sem):
    return pltpu.CompilerParams(dimension_semantics=sem, vmem_limit_bytes=VMEM_LIMIT)


def _const_spec(shape):
    nd = len(shape)
    return pl.BlockSpec(shape, lambda *_: (0,) * nd, pipeline_mode=pl.Buffered(1))


def _layer_spec(shape, layer):
    nd = len(shape)
    return pl.BlockSpec((None,) + tuple(shape), lambda *_: (layer,) + (0,) * nd, pipeline_mode=pl.Buffered(1))


def _alias_spec():
    return pl.BlockSpec(memory_space=pl.ANY)


def _rms(x):
    return x * lax.rsqrt(jnp.mean(x * x, axis=-1, keepdims=True) + RMS_EPS)


def _silu(x):
    return x * (0.5 * jnp.tanh(0.5 * x) + 0.5)


def _softplus(x):
    return jnp.maximum(x, 0.0) + jnp.log(1.0 + jnp.exp(-jnp.abs(x)))


def _split3(x):
    hi = x.astype(BF16)
    r = x - hi.astype(F32)
    mid = r.astype(BF16)
    lo = (r - mid.astype(F32)).astype(BF16)
    return hi, mid, lo


def _dot(a, b):
    return jnp.dot(a, b, preferred_element_type=F32)


def _dot_nt(a, b):
    return lax.dot_general(a, b, (((1,), (1,)), ((), ())), preferred_element_type=F32)


def _dot_sel(x, sel):
    hi, mid, lo = _split3(x)
    return _dot(hi, sel) + _dot(mid, sel) + _dot(lo, sel)


def _sel_dot(sel, x):
    hi, mid, lo = _split3(x)
    return _dot(sel, hi) + _dot(sel, mid) + _dot(sel, lo)


def _inproj_kernel(x_ref, g_ref, w_ref, wdt_ref, o_ref, odt_ref):
    h = (_rms(x_ref[...]) * g_ref[...]).astype(BF16)
    for j in range(PROJ_MAIN // INPROJ_TN):
        sl = slice(j * INPROJ_TN, (j + 1) * INPROJ_TN)
        o_ref[:, sl] = _dot(h, w_ref[:, sl]).astype(o_ref.dtype)
    odt_ref[...] = _dot(h, wdt_ref[...])


def _inproj(x, layer, pw, out_dtype, tm):
    t = x.shape[0]
    return pl.pallas_call(
        _inproj_kernel,
        grid=(t // tm,),
        in_specs=[
            pl.BlockSpec((tm, D_MODEL), lambda i: (i, 0)),
            _layer_spec((1, D_MODEL), layer),
            _layer_spec((D_MODEL, PROJ_MAIN), layer),
            _layer_spec((D_MODEL, LANES), layer),
        ],
        out_specs=[
            pl.BlockSpec((tm, PROJ_MAIN), lambda i: (i, 0)),
            pl.BlockSpec((tm, LANES), lambda i: (i, 0)),
        ],
        out_shape=[
            jax.ShapeDtypeStruct((t, PROJ_MAIN), out_dtype),
            jax.ShapeDtypeStruct((t, LANES), F32),
        ],
        compiler_params=_cparams(("parallel",)),
        name="inproj",
    )(x, pw["norm_mix"], pw["w_main"], pw["w_dt"])


def _dt_terms(dt_raw, dtb, alog, tri, expand):
    dt = _softplus(dt_raw + dtb)
    a = dt * (-jnp.exp(alog))
    acum = _sel_dot(tri, a)
    return dt, acum


def _ssd_out(y, xs, z, dskip, gain):
    y = (y + dskip * xs) * _silu(z)
    parts = []
    for g in range(SSD_GROUPS):
        parts.append(_rms(y[:, g * GROUP_WIDTH:(g + 1) * GROUP_WIDTH]))
    return jnp.concatenate(parts, axis=-1) * gain


def _rotary(x, cos, sin_signed):
    parts = []
    for h in range(RET_HEADS):
        xh = x[:, h * RET_QK_DIM:(h + 1) * RET_QK_DIM]
        parts.append(xh * cos + pltpu.roll(xh, RET_QK_DIM // 2, axis=1) * sin_signed)
    return parts


def _ret_out(o_heads, g, gain):
    o = jnp.concatenate([_rms(o) for o in o_heads], axis=-1)
    return o * gain * _silu(g)


def _mixer_prompt_kernel(proj_ref, dt_ref, *rest, nbb, ret_chunk_decay):
    params = rest[:14]
    y_ref, conv_ref, ssm_ref, ret_ref, xp_scr, st_scr = rest[-6:]
    for k in range(nbb):
        _mixer_prompt_chunk(proj_ref.at[k], dt_ref.at[k], *params, y_ref.at[k], conv_ref.at[k], ssm_ref.at[k],
                            ret_ref.at[k], xp_scr.at[k], st_scr.at[k], ret_chunk_decay=ret_chunk_decay)


def _mixer_prompt_chunk(proj_ref, dt_ref, convw_ref, convb_ref, dtb_ref, alog_ref, exp_ref, dskip_ref,
                        sgain_ref, rgain_ref, cos_ref, sin_ref, dmat_ref, kdec_ref, qdec_ref, shift_ref,
                        y_ref, conv_ref, ssm_ref, ret_ref, xp_scr, st_scr, *, ret_chunk_decay):
    cl = CHUNK
    c = pl.program_id(1)
    nc = pl.num_programs(1)

    @pl.when(c == 0)
    def _():
        xp_scr[...] = jnp.zeros_like(xp_scr)
        st_scr[...] = jnp.zeros_like(st_scr)
        ret_ref[...] = jnp.zeros_like(ret_ref)

    xbc_b = proj_ref[:, OFF_XBC:OFF_XBC + CONV_DIM]
    x_ext = jnp.concatenate(
        [xp_scr[c % 2], xbc_b, jnp.zeros((CONV_K - CONV_TAIL - cl, CONV_DIM), xbc_b.dtype)], axis=0)
    taps = _dot(shift_ref[...], x_ext)
    xbc_raw = xbc_b.astype(F32)
    acc = convb_ref[...] + xbc_raw * convw_ref[CONV_WIDTH - 1:CONV_WIDTH, :]
    for j in range(CONV_WIDTH - 1):
        acc = acc + taps[j * cl:(j + 1) * cl, :] * convw_ref[j:j + 1, :]
    conv_ref[...] = xbc_raw[cl - (CONV_WIDTH - 1):cl, :]
    xp_scr[(c + 1) % 2] = xbc_b[cl - CONV_TAIL:cl, :]
    xbc = _silu(acc)
    xs = xbc[:, 0:SSD_WIDTH]
    bm = xbc[:, SSD_WIDTH:SSD_WIDTH + SSD_GROUPS * SSD_STATE]
    cm = xbc[:, SSD_WIDTH + SSD_GROUPS * SSD_STATE:CONV_DIM]

    row = lax.broadcasted_iota(jnp.int32, (cl, cl), 0)
    col = lax.broadcasted_iota(jnp.int32, (cl, cl), 1)
    causal = row >= col
    tri = jnp.where(causal, 1.0, 0.0).astype(BF16)
    expand = exp_ref[...]
    dt, acum = _dt_terms(dt_ref[...], dtb_ref[...], alog_ref[...], tri, expand)
    acum_t = acum.T
    eacum = jnp.exp(acum)
    dt_x = _dot_sel(dt, expand)
    eacum_x = _dot_sel(eacum, expand)
    dte_x = _dot_sel(jnp.exp(acum[cl - 1:cl, :] - acum), expand)
    xdt = xs * dt_x
    xdt_b = xdt.astype(BF16)
    xdtd_b = (xdt * dte_x).astype(BF16)
    lane = lax.broadcasted_iota(jnp.int32, (cl, LANES), 1)

    y_parts = []
    upd_parts = []
    for g in range(SSD_GROUPS):
        bg = bm[:, g * SSD_STATE:(g + 1) * SSD_STATE]
        cg_b = cm[:, g * SSD_STATE:(g + 1) * SSD_STATE].astype(BF16)
        bg_b = bg.astype(BF16)
        cb = _dot_nt(cg_b, bg_b)
        gsl = slice(g * GROUP_WIDTH, (g + 1) * GROUP_WIDTH)
        y_off = _dot(cg_b, st_scr[:, gsl].astype(BF16)) * eacum_x[:, gsl]
        upd_parts.append(_dot(bg.T.astype(BF16), xdtd_b[:, gsl]))
        for j in range(GROUP_WIDTH // LANES):
            h0 = g * (SSD_HEADS // SSD_GROUPS) + 2 * j
            psl = slice(h0 * SSD_HEAD_DIM, (h0 + 2) * SSD_HEAD_DIM)
            res = []
            for hh in (h0, h0 + 1):
                seg = acum[:, hh:hh + 1] - acum_t[hh:hh + 1, :]
                lmat = jnp.exp(jnp.where(causal, seg, -jnp.inf))
                res.append(_dot((cb * lmat).astype(BF16), xdt_b[:, psl]))
            y_parts.append(jnp.where(lane < SSD_HEAD_DIM, res[0], res[1]) + y_off[:, j * LANES:(j + 1) * LANES])
    y = jnp.concatenate(y_parts, axis=-1)
    st_new = st_scr[...] * eacum_x[cl - 1:cl, :] + jnp.concatenate(upd_parts, axis=-1)
    st_scr[...] = st_new

    @pl.when(c == nc - 1)
    def _():
        ssm_ref[...] = st_new.T

    z = proj_ref[:, OFF_Z:OFF_Z + SSD_WIDTH].astype(F32)
    y_ref[:, 0:SSD_WIDTH] = _ssd_out(y, xs, z, dskip_ref[...], sgain_ref[...]).astype(y_ref.dtype)


    cos = cos_ref[...]
    sin = sin_ref[...]
    q_heads = _rotary(proj_ref[:, OFF_Q:OFF_Q + RET_QK_WIDTH].astype(F32), cos, sin)
    k_heads = _rotary(proj_ref[:, OFF_K:OFF_K + RET_QK_WIDTH].astype(F32), cos, sin)
    o_heads = []
    for h in range(RET_HEADS):
        qh = q_heads[h]
        kh = k_heads[h] * (RET_QK_DIM ** -0.5)
        vh = proj_ref[:, OFF_V + h * RET_V_DIM:OFF_V + (h + 1) * RET_V_DIM]
        hs = slice(h * RET_QK_DIM, (h + 1) * RET_QK_DIM)
        scores = _dot_nt(qh.astype(BF16), kh.astype(BF16)) * dmat_ref[h]
        o_intra = _dot(scores.astype(BF16), vh)
        s_in = ret_ref[h]
        o_cross = _dot((qh * qdec_ref[:, hs]).astype(BF16), s_in.astype(BF16))
        kv = _dot((kh * kdec_ref[:, hs]).T.astype(BF16), vh)
        ret_ref[h] = ret_chunk_decay[h] * s_in + kv
        o_heads.append(o_intra + o_cross)
    gate = proj_ref[:, OFF_G:OFF_G + RET_WIDTH].astype(F32)
    y_ref[:, SSD_WIDTH:SSD_WIDTH + RET_WIDTH] = _ret_out(o_heads, gate, rgain_ref[...]).astype(y_ref.dtype)


def _ret_tables(cl, reps):
    lg = np.log(1.0 - np.exp2(-5.0 - np.arange(RET_HEADS, dtype=np.float64)))
    idx = np.arange(cl, dtype=np.float64)
    rel = idx[:, None] - idx[None, :]
    dmat = np.where(rel[None] >= 0, np.exp(rel[None] * lg[:, None, None]), 0.0).astype(np.float32)
    kdec = np.exp((cl - 1 - idx)[:, None] * lg[None, :]).astype(np.float32)
    qdec = np.exp((idx + 1.0)[:, None] * lg[None, :]).astype(np.float32)
    kdec = np.tile(np.repeat(kdec, RET_QK_DIM, axis=1), (reps, 1))
    qdec = np.tile(np.repeat(qdec, RET_QK_DIM, axis=1), (reps, 1))
    chunk_decay = [float(v) for v in np.exp(cl * lg).astype(np.float32)]
    return dmat, kdec, qdec, chunk_decay


def _rope_tables(pos0, length, reps):
    half = RET_QK_DIM // 2
    inv = ROPE_BASE ** (-np.arange(half, dtype=np.float64) / half)
    pos = (pos0 + np.arange(length)).astype(np.float64)
    ang = pos[:, None] * inv[None, :]
    cos = np.cos(ang).astype(np.float32)
    sin = np.sin(ang).astype(np.float32)
    cos2 = np.tile(np.concatenate([cos, cos], axis=1), (reps, 1))
    sin2 = np.tile(np.concatenate([-sin, sin], axis=1), (reps, 1))
    return cos2, sin2


def _conv_shift_matrix():
    s = np.zeros(((CONV_WIDTH - 1) * CHUNK, CONV_K), np.float32)
    for j in range(CONV_WIDTH - 1):
        for t in range(CHUNK):
            s[j * CHUNK + t, CONV_TAIL + t - (CONV_WIDTH - 1) + j] = 1.0
    return s


def _expand_matrix():
    e = np.zeros((LANES, SSD_WIDTH), np.float32)
    for h in range(SSD_HEADS):
        e[h, h * SSD_HEAD_DIM:(h + 1) * SSD_HEAD_DIM] = 1.0
    return e


def _mixer_param_specs(layer):
    return [
        _layer_spec((CONV_WIDTH, CONV_DIM), layer),
        _layer_spec((1, CONV_DIM), layer),
        _layer_spec((1, LANES), layer),
        _layer_spec((1, LANES), layer),
        _const_spec((LANES, SSD_WIDTH)),
        _layer_spec((1, SSD_WIDTH), layer),
        _layer_spec((1, SSD_WIDTH), layer),
        _layer_spec((1, RET_WIDTH), layer),
    ]


def _mixer_params(pw):
    return (pw["conv_w"], pw["conv_b"], pw["dt_bias"], pw["a_log"], pw["expand"], pw["d_skip"],
            pw["ssd_gain"], pw["ret_gain"])


def _state_out_shapes(batch):
    return [
        jax.ShapeDtypeStruct((DEPTH, batch, CONV_WIDTH - 1, CONV_DIM), F32),
        jax.ShapeDtypeStruct((DEPTH, batch, SSD_WIDTH, SSD_STATE), F32),
        jax.ShapeDtypeStruct((DEPTH, batch, RET_HEADS, RET_QK_DIM, RET_V_DIM), F32),
    ]


def _mixer_prompt(layer, proj, dt_raw, pw, prev_states, batch, seq):
    nc = seq // CHUNK
    nbb = MIXER_SEQS_PER_STEP if batch % MIXER_SEQS_PER_STEP == 0 else 1
    dmat, kdec, qdec, chunk_decay = _ret_tables(CHUNK, 1)
    cos, sin = _rope_tables(0, seq, 1)
    seq_map = lambda b, c: (b, c, 0)
    kern = functools.partial(_mixer_prompt_kernel, nbb=nbb, ret_chunk_decay=chunk_decay)
    n_in = 16
    y, *states = pl.pallas_call(
        kern,
        grid=(batch // nbb, nc),
        in_specs=[
            pl.BlockSpec((nbb, CHUNK, PROJ_MAIN), seq_map),
            pl.BlockSpec((nbb, CHUNK, LANES), seq_map),
            *_mixer_param_specs(layer),
            pl.BlockSpec((CHUNK, LANES), lambda b, c: (c, 0)),
            pl.BlockSpec((CHUNK, LANES), lambda b, c: (c, 0)),
            _const_spec((RET_HEADS, CHUNK, CHUNK)),
            _const_spec((CHUNK, RET_QK_WIDTH)),
            _const_spec((CHUNK, RET_QK_WIDTH)),
            _const_spec(((CONV_WIDTH - 1) * CHUNK, CONV_K)),
            *[_alias_spec() for _ in prev_states],
        ],
        out_specs=[
            pl.BlockSpec((nbb, CHUNK, SSD_WIDTH + RET_WIDTH), seq_map),
            pl.BlockSpec((None, nbb, CONV_WIDTH - 1, CONV_DIM), lambda b, c: (layer, b, 0, 0)),
            pl.BlockSpec((None, nbb, SSD_WIDTH, SSD_STATE), lambda b, c: (layer, b, 0, 0)),
            pl.BlockSpec((None, nbb, RET_HEADS, RET_QK_DIM, RET_V_DIM), lambda b, c: (layer, b, 0, 0, 0)),
        ],
        out_shape=[jax.ShapeDtypeStruct((batch, seq, SSD_WIDTH + RET_WIDTH), BF16), *_state_out_shapes(batch)],
        input_output_aliases={n_in + k: 1 + k for k in range(len(prev_states))},
        scratch_shapes=[
            pltpu.VMEM((nbb, 2, CONV_TAIL, CONV_DIM), BF16),
            pltpu.VMEM((nbb, SSD_STATE, SSD_WIDTH), F32),
        ],
        compiler_params=_cparams(("parallel", "arbitrary")),
        name="mixer_prompt",
    )(proj.reshape(batch, seq, PROJ_MAIN), dt_raw.reshape(batch, seq, LANES), *_mixer_params(pw),
      jnp.asarray(cos), jnp.asarray(sin), jnp.asarray(dmat), jnp.asarray(kdec), jnp.asarray(qdec),
      jnp.asarray(_conv_shift_matrix(), dtype=BF16), *prev_states)
    return (y.reshape(batch * seq, SSD_WIDTH + RET_WIDTH), *states)


def _pad_rows(x, rows):
    return jnp.concatenate([x, jnp.zeros((rows - x.shape[0], x.shape[1]), x.dtype)], axis=0)


def _mixer_sample_kernel(proj_ref, dt_ref, convs_ref, ssm_in_ref, ret_in_ref, convw_ref, convb_ref, dtb_ref,
                         alog_ref, exp_ref, dskip_ref, sgain_ref, rgain_ref, cos_ref, sin_ref, dmat_ref,
                         kdec_ref, qdec_ref, tri_ref, *rest, bb, cl, ret_chunk_decay):
    y_ref, conv_ref, ssm_ref, ret_ref, xp_scr = rest[-5:]
    for i in range(bb):
        xp_scr[i, 8 - (CONV_WIDTH - 1):8, :] = convs_ref[i]
        xp_scr[i, 8:8 + cl, :] = proj_ref[i * cl:(i + 1) * cl, OFF_XBC:OFF_XBC + CONV_DIM]
        conv_ref[i] = xp_scr[i, 8 + cl - (CONV_WIDTH - 1):8 + cl, :]
    acc = None
    for j in range(CONV_WIDTH):
        s = 8 - (CONV_WIDTH - 1) + j
        tap = jnp.concatenate([xp_scr[i, s:s + cl, :] for i in range(bb)], axis=0) * convw_ref[j:j + 1, :]
        acc = convb_ref[...] + tap if acc is None else acc + tap
    xbc = _silu(acc)
    xs = xbc[:, 0:SSD_WIDTH]
    bm = xbc[:, SSD_WIDTH:SSD_WIDTH + SSD_GROUPS * SSD_STATE]
    cm = xbc[:, SSD_WIDTH + SSD_GROUPS * SSD_STATE:CONV_DIM]

    expand = exp_ref[...]
    dt, acum = _dt_terms(dt_ref[...], dtb_ref[...], alog_ref[...], tri_ref[...], expand)
    eacum = jnp.exp(acum)
    alast = jnp.concatenate(
        [jnp.broadcast_to(acum[(i + 1) * cl - 1:(i + 1) * cl, :], (cl, LANES)) for i in range(bb)], axis=0)
    dt_x = _dot_sel(dt, expand)
    eacum_x = _dot_sel(eacum, expand)
    dte_x = _dot_sel(jnp.exp(alast - acum), expand)
    xdt = xs * dt_x
    xdtd = xdt * dte_x
    row = lax.broadcasted_iota(jnp.int32, (cl, LANES), 0)
    col = lax.broadcasted_iota(jnp.int32, (cl, LANES), 1)
    causal = row >= col
    lane = col

    y_rows = []
    for i in range(bb):
        rs = slice(i * cl, (i + 1) * cl)
        acum_i = acum[rs]
        acum_t = _pad_rows(acum_i, LANES).T
        xdt_p = _pad_rows(xdt[rs], LANES).astype(BF16)
        xdtd_p = _pad_rows(xdtd[rs], LANES)
        chunk_decay = eacum[(i + 1) * cl - 1:(i + 1) * cl, :]
        y_parts = []
        for g in range(SSD_GROUPS):
            bg_p = _pad_rows(bm[rs, g * SSD_STATE:(g + 1) * SSD_STATE], LANES)
            cg_b = cm[rs, g * SSD_STATE:(g + 1) * SSD_STATE].astype(BF16)
            cb = _dot_nt(cg_b, bg_p.astype(BF16))
            gsl = slice(g * GROUP_WIDTH, (g + 1) * GROUP_WIDTH)
            st_g = ssm_in_ref[i, gsl, :]
            y_off = _dot_nt(cg_b, st_g.astype(BF16)) * eacum_x[rs, gsl]
            upd = _dot(xdtd_p[:, gsl].T.astype(BF16), bg_p.astype(BF16))
            for hh in range(SSD_HEADS // SSD_GROUPS):
                h = g * (SSD_HEADS // SSD_GROUPS) + hh
                hsl = slice(hh * SSD_HEAD_DIM, (hh + 1) * SSD_HEAD_DIM)
                ssm_ref[i, h * SSD_HEAD_DIM:(h + 1) * SSD_HEAD_DIM, :] = (
                    st_g[hsl, :] * chunk_decay[:, h:h + 1] + upd[hsl, :])
            for j in range(GROUP_WIDTH // LANES):
                h0 = g * (SSD_HEADS // SSD_GROUPS) + 2 * j
                psl = slice(h0 * SSD_HEAD_DIM, (h0 + 2) * SSD_HEAD_DIM)
                res = []
                for hh in (h0, h0 + 1):
                    seg = acum_i[:, hh:hh + 1] - acum_t[hh:hh + 1, :]
                    lmat = jnp.exp(jnp.where(causal, seg, -jnp.inf))
                    res.append(_dot((cb * lmat).astype(BF16), xdt_p[:, psl]))
                y_parts.append(jnp.where(lane < SSD_HEAD_DIM, res[0], res[1]) + y_off[:, j * LANES:(j + 1) * LANES])
        y_rows.append(jnp.concatenate(y_parts, axis=-1))
    y = jnp.concatenate(y_rows, axis=0)
    z = proj_ref[:, OFF_Z:OFF_Z + SSD_WIDTH]
    y_ref[:, 0:SSD_WIDTH] = _ssd_out(y, xs, z, dskip_ref[...], sgain_ref[...]).astype(y_ref.dtype)

    cos = cos_ref[...]
    sin = sin_ref[...]
    q_heads = _rotary(proj_ref[:, OFF_Q:OFF_Q + RET_QK_WIDTH], cos, sin)
    k_heads = _rotary(proj_ref[:, OFF_K:OFF_K + RET_QK_WIDTH], cos, sin)
    o_heads = []
    for h in range(RET_HEADS):
        hs = slice(h * RET_QK_DIM, (h + 1) * RET_QK_DIM)
        kh_all = k_heads[h] * (RET_QK_DIM ** -0.5)
        q_start = q_heads[h] * qdec_ref[:, hs]
        k_end = kh_all * kdec_ref[:, hs]
        o_rows = []
        for i in range(bb):
            rs = slice(i * cl, (i + 1) * cl)
            vh_p = _pad_rows(proj_ref[rs, OFF_V + h * RET_V_DIM:OFF_V + (h + 1) * RET_V_DIM], LANES).astype(BF16)
            kh_p = _pad_rows(kh_all[rs], LANES).astype(BF16)
            scores = _dot_nt(q_heads[h][rs].astype(BF16), kh_p) * dmat_ref[h]
            o_intra = _dot(scores.astype(BF16), vh_p)
            s_in = ret_in_ref[i, h]
            o_cross = _dot(q_start[rs].astype(BF16), s_in.astype(BF16))
            kv = _dot(_pad_rows(k_end[rs], LANES).T.astype(BF16), vh_p)
            ret_ref[i, h] = ret_chunk_decay[h] * s_in + kv
            o_rows.append(o_intra + o_cross)
        o_heads.append(jnp.concatenate(o_rows, axis=0))
    gate = proj_ref[:, OFF_G:OFF_G + RET_WIDTH]
    y_ref[:, SSD_WIDTH:SSD_WIDTH + RET_WIDTH] = _ret_out(o_heads, gate, rgain_ref[...]).astype(y_ref.dtype)


def _mixer_sample(layer, proj, dt_raw, conv_state, ssm_state, ret_state, pw, prev_states, batch, cl, bb):
    dmat, kdec, qdec, chunk_decay = _ret_tables(cl, bb)
    dmat = np.concatenate([dmat, np.zeros((RET_HEADS, cl, LANES - cl), np.float32)], axis=-1)
    cos, sin = _rope_tables(PAST_LEN, cl, bb)
    m = bb * cl
    tri = np.kron(np.eye(bb, dtype=np.float32), np.tril(np.ones((cl, cl), np.float32)))
    kern = functools.partial(_mixer_sample_kernel, bb=bb, cl=cl, ret_chunk_decay=chunk_decay)
    row_map = lambda i: (i, 0)
    state_specs = [
        pl.BlockSpec((None, bb, CONV_WIDTH - 1, CONV_DIM), lambda i: (layer, i, 0, 0)),
        pl.BlockSpec((None, bb, SSD_WIDTH, SSD_STATE), lambda i: (layer, i, 0, 0)),
        pl.BlockSpec((None, bb, RET_HEADS, RET_QK_DIM, RET_V_DIM), lambda i: (layer, i, 0, 0, 0)),
    ]
    n_in = 19
    return pl.pallas_call(
        kern,
        grid=(batch // bb,),
        in_specs=[
            pl.BlockSpec((m, PROJ_MAIN), row_map),
            pl.BlockSpec((m, LANES), row_map),
            *state_specs,
            *_mixer_param_specs(layer),
            _const_spec((m, LANES)),
            _const_spec((m, LANES)),
            _const_spec((RET_HEADS, cl, LANES)),
            _const_spec((m, RET_QK_WIDTH)),
            _const_spec((m, RET_QK_WIDTH)),
            _const_spec((m, m)),
            *[_alias_spec() for _ in prev_states],
        ],
        out_specs=[pl.BlockSpec((m, SSD_WIDTH + RET_WIDTH), row_map), *state_specs],
        out_shape=[jax.ShapeDtypeStruct((batch * cl, SSD_WIDTH + RET_WIDTH), BF16), *_state_out_shapes(batch)],
        input_output_aliases={n_in + k: 1 + k for k in range(len(prev_states))},
        scratch_shapes=[pltpu.VMEM((bb, 8 + cl, CONV_DIM), F32)],
        compiler_params=_cparams(("parallel",)),
        name="mixer_sample",
    )(proj, dt_raw, conv_state, ssm_state, ret_state, *_mixer_params(pw), jnp.asarray(cos), jnp.asarray(sin),
      jnp.asarray(dmat), jnp.asarray(kdec), jnp.asarray(qdec), jnp.asarray(tri, dtype=BF16), *prev_states)


def _outproj_kernel(x_ref, y_ref, w_ref, o_ref):
    o_ref[...] = x_ref[...] + _dot(y_ref[...], w_ref[...])


def _outproj(x, y, layer, pw, tm):
    t = x.shape[0]
    k = y.shape[1]
    return pl.pallas_call(
        _outproj_kernel,
        grid=(t // tm,),
        in_specs=[
            pl.BlockSpec((tm, D_MODEL), lambda i: (i, 0)),
            pl.BlockSpec((tm, k), lambda i: (i, 0)),
            _layer_spec((k, D_MODEL), layer),
        ],
        out_specs=pl.BlockSpec((tm, D_MODEL), lambda i: (i, 0)),
        out_shape=jax.ShapeDtypeStruct((t, D_MODEL), F32),
        compiler_params=_cparams(("parallel",)),
        name="outproj",
    )(x, y, pw["w_out"])


def _memproj_kernel(m_ref, w_ref, k_ref, v_ref, kb_ref, vb_ref, *, bb):
    r = _dot(m_ref[...].reshape(bb * N_MEM, D_MODEL).astype(BF16), w_ref[...])
    kb_ref[...] = r[:, 0:D_MODEL].astype(BF16).reshape(bb, N_MEM, D_MODEL)
    vb_ref[...] = r[:, D_MODEL:2 * D_MODEL].astype(BF16).reshape(bb, N_MEM, D_MODEL)
    for i in range(bb):
        rows = slice(i * N_MEM, (i + 1) * N_MEM)
        for hd in range(XA_HEADS):
            k_ref[i, :, hd, :] = r[rows, hd * XA_HEAD_DIM:(hd + 1) * XA_HEAD_DIM]
            v_ref[i, :, hd, :] = r[rows, D_MODEL + hd * XA_HEAD_DIM:D_MODEL + (hd + 1) * XA_HEAD_DIM]


def _memproj(mem, pw, bb):
    b = mem.shape[0]
    cache_spec = pl.BlockSpec((None, bb, N_MEM, XA_HEADS, XA_HEAD_DIM), lambda l, i: (l, i, 0, 0, 0))
    cache_shape = jax.ShapeDtypeStruct((DEPTH, b, N_MEM, XA_HEADS, XA_HEAD_DIM), F32)
    flat_spec = pl.BlockSpec((None, bb, N_MEM, D_MODEL), lambda l, i: (l, i, 0, 0))
    flat_shape = jax.ShapeDtypeStruct((DEPTH, b, N_MEM, D_MODEL), BF16)
    return pl.pallas_call(
        functools.partial(_memproj_kernel, bb=bb),
        grid=(DEPTH, b // bb),
        in_specs=[
            pl.BlockSpec((bb, N_MEM, D_MODEL), lambda l, i: (i, 0, 0)),
            pl.BlockSpec((None, D_MODEL, 2 * D_MODEL), lambda l, i: (l, 0, 0)),
        ],
        out_specs=[cache_spec, cache_spec, flat_spec, flat_spec],
        out_shape=[cache_shape, cache_shape, flat_shape, flat_shape],
        compiler_params=_cparams(("parallel", "parallel")),
        name="memproj",
    )(mem, pw["w_kv"])


def _softmax_rows(s):
    e = jnp.exp(s - jnp.max(s, axis=-1, keepdims=True))
    return e / jnp.sum(e, axis=-1, keepdims=True)


def _xattn_prompt_kernel(x_ref, g_ref, wq_ref, wo_ref, k_ref, v_ref, gffn_ref, wr_hi_ref, wr_lo_ref, br_ref,
                         o_ref, h_ref, info_ref):
    x = x_ref[0]
    h = (_rms(x) * g_ref[...]).astype(BF16)
    q = _dot(h, wq_ref[...])
    scale = XA_HEAD_DIM ** -0.5
    heads = []
    for hd in range(XA_HEADS):
        hs = slice(hd * XA_HEAD_DIM, (hd + 1) * XA_HEAD_DIM)
        p = _softmax_rows(_dot_nt(q[:, hs].astype(BF16), k_ref[0, :, hs]) * scale)
        heads.append(_dot(p.astype(BF16), v_ref[0, :, hs]))
    o = jnp.concatenate(heads, axis=-1).astype(BF16)
    xo = x + _dot(o, wo_ref[...])
    o_ref[0] = xo
    _route_rows(xo, gffn_ref, wr_hi_ref, wr_lo_ref, br_ref, h_ref, info_ref)


def _xattn_cache_kernel(x_ref, g_ref, wq_ref, wo_ref, k_ref, v_ref, o_ref, *, bb, tm):
    x = x_ref[...].reshape(bb * tm, D_MODEL)
    h = (_rms(x) * g_ref[...]).astype(BF16)
    q = _dot(h, wq_ref[...])
    scale = XA_HEAD_DIM ** -0.5
    rows_flat = XA_HEADS * tm
    assert tm & (tm - 1) == 0 and XA_HEADS & (XA_HEADS - 1) == 0
    row_head = lax.shift_right_logical(lax.broadcasted_iota(jnp.int32, (rows_flat, N_MEM * XA_HEADS), 0),
                                       int(math.log2(tm)))
    col_head = lax.broadcasted_iota(jnp.int32, (rows_flat, N_MEM * XA_HEADS), 1) & (XA_HEADS - 1)
    own_head = row_head == col_head
    outs = []
    for i in range(bb):
        qi = q[i * tm:(i + 1) * tm]
        qf = jnp.concatenate([qi[:, hd * XA_HEAD_DIM:(hd + 1) * XA_HEAD_DIM] for hd in range(XA_HEADS)], axis=0)
        k_all = k_ref[i].reshape(N_MEM * XA_HEADS, XA_HEAD_DIM).astype(BF16)
        v_all = v_ref[i].reshape(N_MEM * XA_HEADS, XA_HEAD_DIM).astype(BF16)
        s = jnp.where(own_head, _dot_nt(qf.astype(BF16), k_all) * scale, -jnp.inf)
        of = _dot(_softmax_rows(s).astype(BF16), v_all)
        outs.append(jnp.concatenate([of[hd * tm:(hd + 1) * tm] for hd in range(XA_HEADS)], axis=-1))
    o = jnp.concatenate(outs, axis=0).astype(BF16)
    o_ref[...] = (x + _dot(o, wo_ref[...])).reshape(bb, tm, D_MODEL)


def _xattn_weight_specs(layer):
    return [_layer_spec((1, D_MODEL), layer), _layer_spec((D_MODEL, D_MODEL), layer),
            _layer_spec((D_MODEL, D_MODEL), layer)]


def _xattn_prompt(x, layer, pw, mem_k, mem_v, tm):
    b, l, _ = x.shape
    per_seq = l // tm
    mem_spec = pl.BlockSpec((None, 1, N_MEM, D_MODEL), lambda i, j: (layer, i, 0, 0))
    return pl.pallas_call(
        _xattn_prompt_kernel,
        grid=(b, per_seq),
        in_specs=[pl.BlockSpec((1, tm, D_MODEL), lambda i, j: (i, j, 0)), *_xattn_weight_specs(layer),
                  mem_spec, mem_spec, *_router_specs(layer)],
        out_specs=[pl.BlockSpec((1, tm, D_MODEL), lambda i, j: (i, j, 0)),
                   pl.BlockSpec((tm, *TOKEN_TILE), lambda i, j: (i * per_seq + j, 0, 0)),
                   pl.BlockSpec((tm, LANES), lambda i, j: (i * per_seq + j, 0))],
        out_shape=[jax.ShapeDtypeStruct((b, l, D_MODEL), F32),
                   jax.ShapeDtypeStruct((b * l, *TOKEN_TILE), F32),
                   jax.ShapeDtypeStruct((b * l, LANES), F32)],
        compiler_params=_cparams(("parallel", "parallel")),
        name="xattn_prompt",
    )(x, pw["norm_mem"], pw["w_mq"], pw["w_mo"], mem_k, mem_v, *_router_params(pw))


def _xattn_cache(x, layer, pw, mem_k, mem_v, bb):
    b, l, _ = x.shape
    kern = functools.partial(_xattn_cache_kernel, bb=bb, tm=l)
    mem_spec = pl.BlockSpec((None, bb, N_MEM, XA_HEADS, XA_HEAD_DIM), lambda i: (layer, i, 0, 0, 0))
    return pl.pallas_call(
        kern,
        grid=(b // bb,),
        in_specs=[pl.BlockSpec((bb, l, D_MODEL), lambda i: (i, 0, 0)), *_xattn_weight_specs(layer),
                  mem_spec, mem_spec],
        out_specs=pl.BlockSpec((bb, l, D_MODEL), lambda i: (i, 0, 0)),
        out_shape=jax.ShapeDtypeStruct((b, l, D_MODEL), F32),
        compiler_params=_cparams(("parallel",)),
        name="xattn_cache",
    )(x, pw["norm_mem"], pw["w_mq"], pw["w_mo"], mem_k, mem_v)


def _router_gates(logits):
    m = logits.shape[0]
    lane_i = lax.broadcasted_iota(jnp.int32, (m, LANES), 1)
    lane = lane_i.astype(F32)
    big = float(LANES)
    is_g = lane_i < N_EGROUPS
    gl = jnp.where(is_g, logits, -jnp.inf)
    gmax = jnp.max(gl, axis=-1, keepdims=True)
    g_idx = jnp.min(jnp.where(is_g & (gl == gmax), lane, big), axis=-1, keepdims=True)
    g_prob = 1.0 / jnp.sum(jnp.exp(gl - gmax), axis=-1, keepdims=True)
    e_lane = lane_i - ROUTER_OFF
    e_group = lax.shift_right_arithmetic(e_lane, int(math.log2(EXPERTS_PER_GROUP))).astype(F32)
    sel = (e_lane >= 0) & (e_lane < N_EXPERTS) & (e_group == g_idx)
    el = jnp.where(sel, logits, -jnp.inf)
    emax = jnp.max(el, axis=-1, keepdims=True)
    ee = jnp.exp(el - emax)
    e_prob = ee / jnp.sum(ee, axis=-1, keepdims=True)
    p1 = jnp.max(jnp.where(sel, e_prob, -1.0), axis=-1, keepdims=True)
    i1 = jnp.min(jnp.where(sel & (e_prob == p1), lane, big), axis=-1, keepdims=True)
    sel2 = sel & (lane != i1)
    p2 = jnp.max(jnp.where(sel2, e_prob, -1.0), axis=-1, keepdims=True)
    i2 = jnp.min(jnp.where(sel2 & (e_prob == p2), lane, big), axis=-1, keepdims=True)
    denom = p1 + p2
    w1 = g_prob * p1 / denom
    w2 = g_prob * p2 / denom
    return jnp.where(lane == i1, w1, 0.0) + jnp.where(lane == i2, w2, 0.0), g_idx


def _moe_kernel(x_ref, g_ref, wr_hi_ref, wr_lo_ref, br_ref, wg_ref, wu_ref, wd_ref, gf_ref, o_ref,
                h_scr, gate_scr, *, final_norm):
    e = pl.program_id(1)

    @pl.when(e == 0)
    def _():
        h = _rms(x_ref[...]) * g_ref[...]
        h_hi = h.astype(BF16)
        h_lo = (h - h_hi.astype(F32)).astype(BF16)
        logits = (_dot(h_hi, wr_hi_ref[...]) + _dot(h_hi, wr_lo_ref[...]) + _dot(h_lo, wr_hi_ref[...])
                  + br_ref[...])
        h_scr[...] = h_hi
        gate_scr[...] = _router_gates(logits)[0]
        o_ref[...] = jnp.zeros_like(o_ref)

    hb = h_scr[...]
    lane = lax.broadcasted_iota(jnp.int32, gate_scr.shape, 1)
    gate_e = jnp.sum(jnp.where(lane == e + ROUTER_OFF, gate_scr[...], 0.0), axis=-1, keepdims=True)
    a = _silu(_dot(hb, wg_ref[...])) * _dot(hb, wu_ref[...]) * gate_e
    o_ref[...] += _dot(a.astype(BF16), wd_ref[...])

    @pl.when(e == pl.num_programs(1) - 1)
    def _():
        y = x_ref[...] + o_ref[...]
        if final_norm:
            y = _rms(y) * gf_ref[...]
        o_ref[...] = y


def _moe(x, layer, pw, final_norm, tm):
    t = x.shape[0]
    kern = functools.partial(_moe_kernel, final_norm=final_norm)
    return pl.pallas_call(
        kern,
        grid=(t // tm, N_EXPERTS),
        in_specs=[
            pl.BlockSpec((tm, D_MODEL), lambda i, e: (i, 0)),
            _layer_spec((1, D_MODEL), layer),
            _layer_spec((D_MODEL, LANES), layer),
            _layer_spec((D_MODEL, LANES), layer),
            _layer_spec((1, LANES), layer),
            pl.BlockSpec((None, None, D_MODEL, EXPERT_FF), lambda i, e: (layer, e, 0, 0)),
            pl.BlockSpec((None, None, D_MODEL, EXPERT_FF), lambda i, e: (layer, e, 0, 0)),
            pl.BlockSpec((None, None, EXPERT_FF, D_MODEL), lambda i, e: (layer, e, 0, 0)),
            _const_spec((1, D_MODEL)),
        ],
        out_specs=pl.BlockSpec((tm, D_MODEL), lambda i, e: (i, 0)),
        out_shape=jax.ShapeDtypeStruct((t, D_MODEL), F32),
        scratch_shapes=[pltpu.VMEM((tm, D_MODEL), BF16), pltpu.VMEM((tm, LANES), F32)],
        compiler_params=_cparams(("parallel", "arbitrary")),
        name="moe",
    )(x, pw["norm_ffn"], pw["wr_hi"], pw["wr_lo"], pw["b_router"], pw["w_gate"], pw["w_up"], pw["w_down"],
      pw["norm_final"])


MOE_TS = 2048
MOE_M = 256
MOE_NB = MOE_TS // MOE_M + N_EGROUPS - 1
MOE_R = MOE_NB * MOE_M
SUBLANES = 8
TOKEN_TILE = (SUBLANES, D_MODEL // SUBLANES)
PLAN_BLK = 512


def _route_rows(x, g_ref, wr_hi_ref, wr_lo_ref, br_ref, h_ref, info_ref):
    h = _rms(x) * g_ref[...]
    h_hi = h.astype(BF16)
    h_lo = (h - h_hi.astype(F32)).astype(BF16)
    both = _dot(h_hi, jnp.concatenate([wr_hi_ref[...], wr_lo_ref[...]], axis=1))
    logits = both[:, 0:LANES] + both[:, LANES:2 * LANES] + _dot(h_lo, wr_hi_ref[...]) + br_ref[...]
    gates, g_idx = _router_gates(logits)
    lane = lax.broadcasted_iota(jnp.int32, gates.shape, 1)
    info_ref[...] = jnp.where(lane == 0, g_idx, gates)
    h_ref[...] = h.reshape(h.shape[0], *TOKEN_TILE)


def _moe_route_kernel(x_ref, g_ref, wr_hi_ref, wr_lo_ref, br_ref, h_ref, info_ref):
    _route_rows(x_ref[...], g_ref, wr_hi_ref, wr_lo_ref, br_ref, h_ref, info_ref)


def _router_specs(layer):
    return [_layer_spec((1, D_MODEL), layer), _layer_spec((D_MODEL, LANES), layer),
            _layer_spec((D_MODEL, LANES), layer), _layer_spec((1, LANES), layer)]


def _router_params(pw):
    return (pw["norm_ffn"], pw["wr_hi"], pw["wr_lo"], pw["b_router"])


def _moe_route(x, layer, pw, tm):
    t = x.shape[0]
    return pl.pallas_call(
        _moe_route_kernel,
        grid=(t // tm,),
        in_specs=[pl.BlockSpec((tm, D_MODEL), lambda i: (i, 0)), *_router_specs(layer)],
        out_specs=[pl.BlockSpec((tm, *TOKEN_TILE), lambda i: (i, 0, 0)),
                   pl.BlockSpec((tm, LANES), lambda i: (i, 0))],
        out_shape=[jax.ShapeDtypeStruct((t, *TOKEN_TILE), F32), jax.ShapeDtypeStruct((t, LANES), F32)],
        compiler_params=_cparams(("parallel",)),
        name="moe_route",
    )(x, *_router_params(pw))


def _lane_pick(v, idx):
    lane = lax.broadcasted_iota(jnp.int32, v.shape, 1)
    return jnp.sum(jnp.where(lane == idx, v, 0.0), axis=-1, keepdims=True)


def _moe_plan_kernel(info_ref, dest_ref, items_ref, *, n_tiles):
    row = lax.broadcasted_iota(jnp.int32, (PLAN_BLK, PLAN_BLK), 0)
    col = lax.broadcasted_iota(jnp.int32, (PLAN_BLK, PLAN_BLK), 1)
    tri = jnp.where(row >= col, 1.0, 0.0).astype(BF16)
    row = lax.broadcasted_iota(jnp.int32, (LANES, LANES), 0)
    col = lax.broadcasted_iota(jnp.int32, (LANES, LANES), 1)
    before = jnp.where(row < col, 1.0, 0.0).astype(BF16)
    lane = lax.broadcasted_iota(jnp.int32, (PLAN_BLK, LANES), 1)
    lane_row = lax.broadcasted_iota(jnp.int32, (1, LANES), 1)
    block_groups = jnp.zeros((1, LANES), F32)
    for s in range(n_tiles):
        carry = jnp.zeros((1, LANES), F32)
        parts = []
        for blk in range(MOE_TS // PLAN_BLK):
            r0 = s * MOE_TS + blk * PLAN_BLK
            gid = info_ref[r0:r0 + PLAN_BLK, 0:1]
            onehot = jnp.where((lane < N_EGROUPS) & (lane.astype(F32) == gid), 1.0, 0.0)
            cum = _dot(tri, onehot.astype(BF16)) + carry
            carry = cum[PLAN_BLK - 1:PLAN_BLK, :]
            parts.append((onehot, cum))
        counts = carry
        padded = jnp.ceil(counts * (1.0 / MOE_M)) * MOE_M
        seg_start = _dot_sel(jnp.broadcast_to(padded, (8, LANES)), before)[0:1, :]
        rows = []
        for onehot, cum in parts:
            dest_col = jnp.sum(onehot * (seg_start + cum - 1.0), axis=-1, keepdims=True)
            for q in range(PLAN_BLK // LANES):
                piece = jnp.broadcast_to(dest_col[q * LANES:(q + 1) * LANES, :], (LANES, LANES))
                rows.append(piece.T[0:1, :])
        n_rows = MOE_TS // LANES
        dest_ref[s * n_rows:(s + 1) * n_rows, :] = jnp.concatenate(rows, axis=0).astype(jnp.int32)
        blk_start = lane_row.astype(F32) * MOE_M
        group_of_block = jnp.full((1, LANES), -1.0, F32)
        for g in range(N_EGROUPS):
            start_g = _lane_pick(seg_start, g)
            size_g = _lane_pick(padded, g)
            group_of_block = jnp.where((blk_start >= start_g) & (blk_start < start_g + size_g), float(g),
                                       group_of_block)
        in_tile = jnp.where(lane_row < MOE_NB, group_of_block + 1.0, 0.0)
        if s:
            in_tile = pltpu.roll(jnp.broadcast_to(in_tile, (SUBLANES, LANES)), s * MOE_NB, axis=1)[0:1]
        block_groups = block_groups + in_tile

    n_items = n_tiles * MOE_NB
    group = block_groups - 1.0
    lane_f = lane_row.astype(F32)
    order_key = jnp.where(group >= 0, group, float(N_EGROUPS)) * LANES + lane_f
    order_key = jnp.where(lane_row < n_items, order_key, float((N_EGROUPS + 1) * LANES) + lane_f)
    key_by_lane = jnp.broadcast_to(order_key, (LANES, LANES))
    key_by_row = key_by_lane.T
    rank = jnp.sum(jnp.where(key_by_lane < key_by_row, 1.0, 0.0), axis=-1, keepdims=True)
    chosen = rank == lane_f
    block_id = lax.broadcasted_iota(jnp.int32, (LANES, LANES), 0).astype(F32)
    group_by_row = jnp.broadcast_to(group, (LANES, LANES)).T
    item_block = jnp.sum(jnp.where(chosen, block_id, 0.0), axis=0, keepdims=True)
    item_group = jnp.sum(jnp.where(chosen, group_by_row, 0.0), axis=0, keepdims=True)
    item_group = jnp.where(item_group >= 0, item_group, -float(N_EGROUPS))
    items_ref[0:1, :] = item_block.astype(jnp.int32)
    items_ref[1:2, :] = item_group.astype(jnp.int32)


def _moe_plan(info, n_tiles):
    t = info.shape[0]
    assert n_tiles * MOE_NB <= LANES
    return pl.pallas_call(
        functools.partial(_moe_plan_kernel, n_tiles=n_tiles),
        grid=(1,),
        in_specs=[pl.BlockSpec((t, LANES), lambda i: (0, 0))],
        out_specs=[pl.BlockSpec((t // LANES, LANES), lambda i: (0, 0)),
                   pl.BlockSpec((2, LANES), lambda i: (0, 0))],
        out_shape=[jax.ShapeDtypeStruct((t // LANES, LANES), jnp.int32),
                   jax.ShapeDtypeStruct((2, LANES), jnp.int32)],
        compiler_params=_cparams(("arbitrary",)),
        name="moe_plan",
    )(info)


def _dest_row(dest_ref, token):
    return dest_ref[token]


def _moe_permute_kernel(dest_ref, h_ref, info_ref, hs_ref, infos_ref):
    base = pl.program_id(0) * MOE_TS
    hs_ref[...] = jnp.zeros_like(hs_ref)
    infos_ref[...] = jnp.zeros_like(infos_ref)

    def body(t, carry):
        d = _dest_row(dest_ref, base + t)
        hs_ref[d] = h_ref[t]
        infos_ref[pl.ds(d, 1), :] = info_ref[pl.ds(t, 1), :]
        return carry

    lax.fori_loop(0, MOE_TS, body, 0, unroll=8)


def _moe_permute(dest, h, info, n_tiles):
    return pl.pallas_call(
        _moe_permute_kernel,
        grid_spec=pltpu.PrefetchScalarGridSpec(
            num_scalar_prefetch=1,
            grid=(n_tiles,),
            in_specs=[pl.BlockSpec((MOE_TS, *TOKEN_TILE), lambda s, d: (s, 0, 0)),
                      pl.BlockSpec((MOE_TS, LANES), lambda s, d: (s, 0))],
            out_specs=[pl.BlockSpec((MOE_R, *TOKEN_TILE), lambda s, d: (s, 0, 0)),
                       pl.BlockSpec((MOE_R, LANES), lambda s, d: (s, 0))],
        ),
        out_shape=[jax.ShapeDtypeStruct((n_tiles * MOE_R, *TOKEN_TILE), F32),
                   jax.ShapeDtypeStruct((n_tiles * MOE_R, LANES), F32)],
        compiler_params=_cparams(("arbitrary",)),
        name="moe_permute",
    )(dest, h, info)


def _moe_ffn_kernel(item_block_ref, item_group_ref, x_ref, gates_ref, wg_ref, wu_ref, wd_ref, y_ref):
    group = item_group_ref[pl.program_id(0)]

    @pl.when(group >= 0)
    def _():
        xb = x_ref[...].reshape(MOE_M, D_MODEL).astype(BF16)
        gates = gates_ref[...]
        lane = lax.broadcasted_iota(jnp.int32, gates.shape, 1)
        y = None
        for e in range(EXPERTS_PER_GROUP):
            gate_e = jnp.sum(jnp.where(lane == ROUTER_OFF + group * EXPERTS_PER_GROUP + e, gates, 0.0),
                             axis=-1, keepdims=True)
            a = _silu(_dot(xb, wg_ref[e])) * _dot(xb, wu_ref[e]) * gate_e
            d = _dot(a.astype(BF16), wd_ref[e])
            y = d if y is None else y + d
        y_ref[...] = y.reshape(MOE_M, *TOKEN_TILE)

    @pl.when(group < 0)
    def _():
        y_ref[...] = jnp.zeros_like(y_ref)


def _moe_ffn(item_block, item_group, hs, infos, layer, pw, n_tiles):
    def w_map(w, blk, grp):
        g = grp[w]
        return (layer, jnp.where(g >= 0, g, -1 - g), 0, 0, 0)

    def grouped(w):
        return w.reshape(DEPTH, N_EGROUPS, EXPERTS_PER_GROUP, *w.shape[2:])

    tile_spec = pl.BlockSpec((MOE_M, *TOKEN_TILE), lambda w, blk, grp: (blk[w], 0, 0))
    return pl.pallas_call(
        _moe_ffn_kernel,
        grid_spec=pltpu.PrefetchScalarGridSpec(
            num_scalar_prefetch=2,
            grid=(n_tiles * MOE_NB,),
            in_specs=[
                tile_spec,
                pl.BlockSpec((MOE_M, LANES), lambda w, blk, grp: (blk[w], 0)),
                pl.BlockSpec((None, None, EXPERTS_PER_GROUP, D_MODEL, EXPERT_FF), w_map),
                pl.BlockSpec((None, None, EXPERTS_PER_GROUP, D_MODEL, EXPERT_FF), w_map),
                pl.BlockSpec((None, None, EXPERTS_PER_GROUP, EXPERT_FF, D_MODEL), w_map),
            ],
            out_specs=tile_spec,
        ),
        out_shape=jax.ShapeDtypeStruct((n_tiles * MOE_R, *TOKEN_TILE), F32),
        compiler_params=_cparams(("arbitrary",)),
        name="moe_ffn",
    )(item_block, item_group, hs, infos, grouped(pw["w_gate"]), grouped(pw["w_up"]), grouped(pw["w_down"]))


def _moe_unpermute_kernel(dest_ref, x_ref, ys_ref, gf_ref, o_ref, y_scr, *, tm, final_norm):
    base = pl.program_id(0) * MOE_TS + pl.program_id(1) * tm

    def body(t, carry):
        y_scr[t] = ys_ref[_dest_row(dest_ref, base + t)]
        return carry

    lax.fori_loop(0, tm, body, 0, unroll=8)
    y = x_ref[...] + y_scr[...].reshape(tm, D_MODEL)
    if final_norm:
        y = _rms(y) * gf_ref[...]
    o_ref[...] = y


def _moe_unpermute(dest, x, ys, pw, final_norm, n_tiles, tm):
    per_tile = MOE_TS // tm
    return pl.pallas_call(
        functools.partial(_moe_unpermute_kernel, tm=tm, final_norm=final_norm),
        grid_spec=pltpu.PrefetchScalarGridSpec(
            num_scalar_prefetch=1,
            grid=(n_tiles, per_tile),
            in_specs=[
                pl.BlockSpec((tm, D_MODEL), lambda s, i, d: (s * per_tile + i, 0)),
                pl.BlockSpec((MOE_R, *TOKEN_TILE), lambda s, i, d: (s, 0, 0)),
                pl.BlockSpec((1, D_MODEL), lambda s, i, d: (0, 0), pipeline_mode=pl.Buffered(1)),
            ],
            out_specs=pl.BlockSpec((tm, D_MODEL), lambda s, i, d: (s * per_tile + i, 0)),
            scratch_shapes=[pltpu.VMEM((tm, *TOKEN_TILE), F32)],
        ),
        out_shape=jax.ShapeDtypeStruct(x.shape, F32),
        compiler_params=_cparams(("arbitrary", "arbitrary")),
        name="moe_unpermute",
    )(dest, x, ys, pw["norm_final"])


def _moe_sorted(x, layer, pw, final_norm, routed=None):
    n_tiles = x.shape[0] // MOE_TS
    h, info = routed if routed is not None else _moe_route(x, layer, pw, 1024)
    dest, items = _moe_plan(info, n_tiles)
    dest = dest.reshape(-1)
    hs, infos = _moe_permute(dest, h, info, n_tiles)
    ys = _moe_ffn(items[0], items[1], hs, infos, layer, pw, n_tiles)
    return _moe_unpermute(dest, x, ys, pw, final_norm, n_tiles, 1024)


def _row(v):
    return v.reshape(v.shape[0], 1, v.shape[1])


def _pad_lanes(v):
    return jnp.pad(v, ((0, 0),) * (v.ndim - 1) + ((0, LANES - v.shape[-1]),))


def _prep_weights(norm_mix, w_in, conv_w, conv_b, dt_bias, a_log, d_skip, ssd_gain, ret_gain, w_out,
                  norm_mem, w_mq, w_mk, w_mv, w_mo, norm_ffn, w_rg, b_rg, w_re, b_re, w_gate, w_up, w_down,
                  norm_final):
    dt_off = SSD_WIDTH + CONV_DIM
    w_main = jnp.concatenate([w_in[:, :, :dt_off], w_in[:, :, dt_off + SSD_HEADS:]], axis=2).astype(BF16)
    w_dt = _pad_lanes(w_in[:, :, dt_off:dt_off + SSD_HEADS]).astype(BF16)
    w_router = _pad_lanes(jnp.concatenate([w_rg, w_re], axis=2))
    wr_hi = w_router.astype(BF16)
    wr_lo = (w_router - wr_hi.astype(F32)).astype(BF16)
    return dict(
        norm_mix=_row(norm_mix), w_main=w_main, w_dt=w_dt,
        conv_w=conv_w, conv_b=_row(conv_b),
        dt_bias=_row(_pad_lanes(dt_bias)), a_log=_row(_pad_lanes(a_log)),
        expand=jnp.asarray(_expand_matrix(), dtype=BF16),
        d_skip=_row(jnp.repeat(d_skip, SSD_HEAD_DIM, axis=1)),
        ssd_gain=_row(ssd_gain), ret_gain=_row(ret_gain),
        w_out=w_out.astype(BF16),
        norm_mem=_row(norm_mem), w_mq=w_mq.astype(BF16), w_mo=w_mo.astype(BF16),
        w_kv=jnp.concatenate([w_mk, w_mv], axis=2).astype(BF16),
        norm_ffn=_row(norm_ffn), wr_hi=wr_hi, wr_lo=wr_lo,
        b_router=_row(_pad_lanes(jnp.concatenate([b_rg, b_re], axis=1))),
        w_gate=w_gate.astype(BF16), w_up=w_up.astype(BF16), w_down=w_down.astype(BF16),
        norm_final=norm_final.reshape(1, -1),
    )


def _token_tile(t, cap):
    tm = min(t, cap)
    assert t % tm == 0
    return tm


def _trunk(x, mixer_fn, xattn_fn, pw, proj_dtype):
    b, l, _ = x.shape
    t = b * l
    xf = x.reshape(t, D_MODEL)
    states = ()
    for layer in range(DEPTH):
        proj, dt_raw = _inproj(xf, layer, pw, proj_dtype, _token_tile(t, 512))
        y, *states = mixer_fn(layer, proj, dt_raw, tuple(states))
        xf = _outproj(xf, y, layer, pw, _token_tile(t, 1024))
        xf, routed = xattn_fn(layer, xf.reshape(b, l, D_MODEL))
        xf = xf.reshape(t, D_MODEL)
        if t % MOE_TS == 0:
            xf = _moe_sorted(xf, layer, pw, layer == DEPTH - 1, routed)
        else:
            xf = _moe(xf, layer, pw, layer == DEPTH - 1, _token_tile(t, 1024))
    conv, ssm, ret = states
    return (xf.reshape(b, l, D_MODEL), ssm.reshape(DEPTH, b, SSD_HEADS, SSD_HEAD_DIM, SSD_STATE), conv, ret)


def kernel(x_prompt, x_sample, mem_prompt, state_ssm, state_conv, state_ret, cache_mem_k, cache_mem_v,
           norm_mix, w_in, conv_w, conv_b, dt_bias, a_log, d_skip, ssd_gain, ret_gain, w_out,
           norm_mem, w_mq, w_mk, w_mv, w_mo, norm_ffn, w_rg, b_rg, w_re, b_re, w_gate, w_up, w_down,
           norm_final):
    pw = _prep_weights(norm_mix, w_in, conv_w, conv_b, dt_bias, a_log, d_skip, ssd_gain, ret_gain, w_out,
                       norm_mem, w_mq, w_mk, w_mv, w_mo, norm_ffn, w_rg, b_rg, w_re, b_re, w_gate, w_up, w_down,
                       norm_final)
    bp, lp, _ = x_prompt.shape
    bs, ls, _ = x_sample.shape
    n_mem = mem_prompt.shape[1]

    assert n_mem == N_MEM
    mem_k_p, mem_v_p, mem_k_rows, mem_v_rows = _memproj(mem_prompt, pw, _token_tile(bp, 2))

    def mixer_p(layer, proj, dt_raw, prev_states):
        return _mixer_prompt(layer, proj, dt_raw, pw, prev_states, bp, lp)

    def xattn_p(layer, x):
        xo, h, info = _xattn_prompt(x, layer, pw, mem_k_rows, mem_v_rows, _token_tile(lp, 1024))
        return xo, (h, info)

    y_prompt, ssm_p, conv_p, ret_p = _trunk(x_prompt, mixer_p, xattn_p, pw, BF16)

    sample_bb = 8
    ssm_in = state_ssm.reshape(DEPTH, bs, SSD_WIDTH, SSD_STATE)

    def mixer_s(layer, proj, dt_raw, prev_states):
        return _mixer_sample(layer, proj, dt_raw, state_conv, ssm_in, state_ret, pw, prev_states, bs, ls,
                             sample_bb)

    def xattn_s(layer, x):
        return _xattn_cache(x, layer, pw, cache_mem_k, cache_mem_v, sample_bb), None

    y_sample, ssm_s, conv_s, ret_s = _trunk(x_sample, mixer_s, xattn_s, pw, F32)
    return (y_prompt, y_sample, ssm_p, conv_p, ret_p, mem_k_p, mem_v_p, ssm_s, conv_s, ret_s)
```

```python
import functools
import math

import numpy as np
import jax
import jax.numpy as jnp
from jax import lax
from jax.experimental import pallas as pl
from jax.experimental.pallas import tpu as pltpu

F32 = jnp.float32
BF16 = jnp.bfloat16

D_MODEL = 1024
DEPTH = 2
PAST_LEN = 16384
SSD_HEAD_DIM = 64
SSD_HEADS = 16
SSD_GROUPS = 2
SSD_STATE = 128
SSD_WIDTH = 1024
GROUP_WIDTH = SSD_WIDTH // SSD_GROUPS
CONV_WIDTH = 4
CONV_DIM = SSD_WIDTH + 2 * SSD_GROUPS * SSD_STATE
RET_HEADS = 4
RET_V_DIM = 256
RET_QK_DIM = 128
RET_WIDTH = 1024
RET_QK_WIDTH = RET_HEADS * RET_QK_DIM
ROPE_BASE = 10000.0
N_MEM = 256
XA_HEADS = 4
XA_HEAD_DIM = 256
N_EGROUPS = 4
EXPERTS_PER_GROUP = 4
N_EXPERTS = 16
EXPERT_FF = 512
RMS_EPS = 1e-6

LANES = 128
CHUNK = 128
OFF_Z = 0
OFF_XBC = OFF_Z + SSD_WIDTH
OFF_Q = OFF_XBC + CONV_DIM
OFF_K = OFF_Q + RET_QK_WIDTH
OFF_V = OFF_K + RET_QK_WIDTH
OFF_G = OFF_V + RET_WIDTH
PROJ_MAIN = OFF_G + RET_WIDTH
INPROJ_TN = 512
CONV_TAIL = 16
MIXER_SEQS_PER_STEP = 2
CONV_K = 256
ROUTER_OFF = N_EGROUPS

VMEM_LIMIT = 56 * 1024 * 1024


def _cparams(sem):
    return pltpu.CompilerParams(dimension_semantics=sem, vmem_limit_bytes=VMEM_LIMIT)


def _const_spec(shape):
    nd = len(shape)
    return pl.BlockSpec(shape, lambda *_: (0,) * nd, pipeline_mode=pl.Buffered(1))


def _layer_spec(shape, layer):
    nd = len(shape)
    return pl.BlockSpec((None,) + tuple(shape), lambda *_: (layer,) + (0,) * nd, pipeline_mode=pl.Buffered(1))


def _alias_spec():
    return pl.BlockSpec(memory_space=pl.ANY)


def _rms(x):
    return x * lax.rsqrt(jnp.mean(x * x, axis=-1, keepdims=True) + RMS_EPS)


def _silu(x):
    return x * (0.5 * jnp.tanh(0.5 * x) + 0.5)


def _softplus(x):
    return jnp.maximum(x, 0.0) + jnp.log(1.0 + jnp.exp(-jnp.abs(x)))


def _split3(x):
    hi = x.astype(BF16)
    r = x - hi.astype(F32)
    mid = r.astype(BF16)
    lo = (r - mid.astype(F32)).astype(BF16)
    return hi, mid, lo


def _dot(a, b):
    return jnp.dot(a, b, preferred_element_type=F32)


def _dot_nt(a, b):
    return lax.dot_general(a, b, (((1,), (1,)), ((), ())), preferred_element_type=F32)


def _dot_sel(x, sel):
    hi, mid, lo = _split3(x)
    return _dot(hi, sel) + _dot(mid, sel) + _dot(lo, sel)


def _sel_dot(sel, x):
    hi, mid, lo = _split3(x)
    return _dot(sel, hi) + _dot(sel, mid) + _dot(sel, lo)


def _inproj_kernel(x_ref, g_ref, w_ref, wdt_ref, o_ref, odt_ref):
    h = (_rms(x_ref[...]) * g_ref[...]).astype(BF16)
    for j in range(PROJ_MAIN // INPROJ_TN):
        sl = slice(j * INPROJ_TN, (j + 1) * INPROJ_TN)
        o_ref[:, sl] = _dot(h, w_ref[:, sl]).astype(o_ref.dtype)
    odt_ref[...] = _dot(h, wdt_ref[...])


def _inproj(x, layer, pw, out_dtype, tm):
    t = x.shape[0]
    return pl.pallas_call(
        _inproj_kernel,
        grid=(t // tm,),
        in_specs=[
            pl.BlockSpec((tm, D_MODEL), lambda i: (i, 0)),
            _layer_spec((1, D_MODEL), layer),
            _layer_spec((D_MODEL, PROJ_MAIN), layer),
            _layer_spec((D_MODEL, LANES), layer),
        ],
        out_specs=[
            pl.BlockSpec((tm, PROJ_MAIN), lambda i: (i, 0)),
            pl.BlockSpec((tm, LANES), lambda i: (i, 0)),
        ],
        out_shape=[
            jax.ShapeDtypeStruct((t, PROJ_MAIN), out_dtype),
            jax.ShapeDtypeStruct((t, LANES), F32),
        ],
        compiler_params=_cparams(("parallel",)),
        name="inproj",
    )(x, pw["norm_mix"], pw["w_main"], pw["w_dt"])


def _dt_terms(dt_raw, dtb, alog, tri, expand):
    dt = _softplus(dt_raw + dtb)
    a = dt * (-jnp.exp(alog))
    acum = _sel_dot(tri, a)
    return dt, acum


def _ssd_out(y, xs, z, dskip, gain):
    y = (y + dskip * xs) * _silu(z)
    parts = []
    for g in range(SSD_GROUPS):
        parts.append(_rms(y[:, g * GROUP_WIDTH:(g + 1) * GROUP_WIDTH]))
    return jnp.concatenate(parts, axis=-1) * gain


def _rotary(x, cos, sin_signed):
    parts = []
    for h in range(RET_HEADS):
        xh = x[:, h * RET_QK_DIM:(h + 1) * RET_QK_DIM]
        parts.append(xh * cos + pltpu.roll(xh, RET_QK_DIM // 2, axis=1) * sin_signed)
    return parts


def _ret_out(o_heads, g, gain):
    o = jnp.concatenate([_rms(o) for o in o_heads], axis=-1)
    return o * gain * _silu(g)


def _mixer_prompt_kernel(proj_ref, dt_ref, *rest, nbb, ret_chunk_decay):
    params = rest[:14]
    y_ref, conv_ref, ssm_ref, ret_ref, xp_scr, st_scr = rest[-6:]
    for k in range(nbb):
        _mixer_prompt_chunk(proj_ref.at[k], dt_ref.at[k], *params, y_ref.at[k], conv_ref.at[k], ssm_ref.at[k],
                            ret_ref.at[k], xp_scr.at[k], st_scr.at[k], ret_chunk_decay=ret_chunk_decay)


def _mixer_prompt_chunk(proj_ref, dt_ref, convw_ref, convb_ref, dtb_ref, alog_ref, exp_ref, dskip_ref,
                        sgain_ref, rgain_ref, cos_ref, sin_ref, dmat_ref, kdec_ref, qdec_ref, shift_ref,
                        y_ref, conv_ref, ssm_ref, ret_ref, xp_scr, st_scr, *, ret_chunk_decay):
    cl = CHUNK
    c = pl.program_id(1)
    nc = pl.num_programs(1)

    @pl.when(c == 0)
    def _():
        xp_scr[...] = jnp.zeros_like(xp_scr)
        st_scr[...] = jnp.zeros_like(st_scr)
        ret_ref[...] = jnp.zeros_like(ret_ref)

    xbc_b = proj_ref[:, OFF_XBC:OFF_XBC + CONV_DIM]
    x_ext = jnp.concatenate(
        [xp_scr[c % 2], xbc_b, jnp.zeros((CONV_K - CONV_TAIL - cl, CONV_DIM), xbc_b.dtype)], axis=0)
    taps = _dot(shift_ref[...], x_ext)
    xbc_raw = xbc_b.astype(F32)
    acc = convb_ref[...] + xbc_raw * convw_ref[CONV_WIDTH - 1:CONV_WIDTH, :]
    for j in range(CONV_WIDTH - 1):
        acc = acc + taps[j * cl:(j + 1) * cl, :] * convw_ref[j:j + 1, :]
    conv_ref[...] = xbc_raw[cl - (CONV_WIDTH - 1):cl, :]
    xp_scr[(c + 1) % 2] = xbc_b[cl - CONV_TAIL:cl, :]
    xbc = _silu(acc)
    xs = xbc[:, 0:SSD_WIDTH]
    bm = xbc[:, SSD_WIDTH:SSD_WIDTH + SSD_GROUPS * SSD_STATE]
    cm = xbc[:, SSD_WIDTH + SSD_GROUPS * SSD_STATE:CONV_DIM]

    row = lax.broadcasted_iota(jnp.int32, (cl, cl), 0)
    col = lax.broadcasted_iota(jnp.int32, (cl, cl), 1)
    causal = row >= col
    tri = jnp.where(causal, 1.0, 0.0).astype(BF16)
    expand = exp_ref[...]
    dt, acum = _dt_terms(dt_ref[...], dtb_ref[...], alog_ref[...], tri, expand)
    acum_t = acum.T
    eacum = jnp.exp(acum)
    dt_x = _dot_sel(dt, expand)
    eacum_x = _dot_sel(eacum, expand)
    dte_x = _dot_sel(jnp.exp(acum[cl - 1:cl, :] - acum), expand)
    xdt = xs * dt_x
    xdt_b = xdt.astype(BF16)
    xdtd_b = (xdt * dte_x).astype(BF16)
    lane = lax.broadcasted_iota(jnp.int32, (cl, LANES), 1)

    y_parts = []
    upd_parts = []
    for g in range(SSD_GROUPS):
        bg = bm[:, g * SSD_STATE:(g + 1) * SSD_STATE]
        cg_b = cm[:, g * SSD_STATE:(g + 1) * SSD_STATE].astype(BF16)
        bg_b = bg.astype(BF16)
        cb = _dot_nt(cg_b, bg_b)
        gsl = slice(g * GROUP_WIDTH, (g + 1) * GROUP_WIDTH)
        y_off = _dot(cg_b, st_scr[:, gsl].astype(BF16)) * eacum_x[:, gsl]
        upd_parts.append(_dot(bg.T.astype(BF16), xdtd_b[:, gsl]))
        for j in range(GROUP_WIDTH // LANES):
            h0 = g * (SSD_HEADS // SSD_GROUPS) + 2 * j
            psl = slice(h0 * SSD_HEAD_DIM, (h0 + 2) * SSD_HEAD_DIM)
            res = []
            for hh in (h0, h0 + 1):
                seg = acum[:, hh:hh + 1] - acum_t[hh:hh + 1, :]
                lmat = jnp.exp(jnp.where(causal, seg, -jnp.inf))
                res.append(_dot((cb * lmat).astype(BF16), xdt_b[:, psl]))
            y_parts.append(jnp.where(lane < SSD_HEAD_DIM, res[0], res[1]) + y_off[:, j * LANES:(j + 1) * LANES])
    y = jnp.concatenate(y_parts, axis=-1)
    st_new = st_scr[...] * eacum_x[cl - 1:cl, :] + jnp.concatenate(upd_parts, axis=-1)
    st_scr[...] = st_new

    @pl.when(c == nc - 1)
    def _():
        ssm_ref[...] = st_new.T

    z = proj_ref[:, OFF_Z:OFF_Z + SSD_WIDTH].astype(F32)
    y_ref[:, 0:SSD_WIDTH] = _ssd_out(y, xs, z, dskip_ref[...], sgain_ref[...]).astype(y_ref.dtype)


    cos = cos_ref[...]
    sin = sin_ref[...]
    q_heads = _rotary(proj_ref[:, OFF_Q:OFF_Q + RET_QK_WIDTH].astype(F32), cos, sin)
    k_heads = _rotary(proj_ref[:, OFF_K:OFF_K + RET_QK_WIDTH].astype(F32), cos, sin)
    o_heads = []
    for h in range(RET_HEADS):
        qh = q_heads[h]
        kh = k_heads[h] * (RET_QK_DIM ** -0.5)
        vh = proj_ref[:, OFF_V + h * RET_V_DIM:OFF_V + (h + 1) * RET_V_DIM]
        hs = slice(h * RET_QK_DIM, (h + 1) * RET_QK_DIM)
        scores = _dot_nt(qh.astype(BF16), kh.astype(BF16)) * dmat_ref[h]
        o_intra = _dot(scores.astype(BF16), vh)
        s_in = ret_ref[h]
        o_cross = _dot((qh * qdec_ref[:, hs]).astype(BF16), s_in.astype(BF16))
        kv = _dot((kh * kdec_ref[:, hs]).T.astype(BF16), vh)
        ret_ref[h] = ret_chunk_decay[h] * s_in + kv
        o_heads.append(o_intra + o_cross)
    gate = proj_ref[:, OFF_G:OFF_G + RET_WIDTH].astype(F32)
    y_ref[:, SSD_WIDTH:SSD_WIDTH + RET_WIDTH] = _ret_out(o_heads, gate, rgain_ref[...]).astype(y_ref.dtype)


def _ret_tables(cl, reps):
    lg = np.log(1.0 - np.exp2(-5.0 - np.arange(RET_HEADS, dtype=np.float64)))
    idx = np.arange(cl, dtype=np.float64)
    rel = idx[:, None] - idx[None, :]
    dmat = np.where(rel[None] >= 0, np.exp(rel[None] * lg[:, None, None]), 0.0).astype(np.float32)
    kdec = np.exp((cl - 1 - idx)[:, None] * lg[None, :]).astype(np.float32)
    qdec = np.exp((idx + 1.0)[:, None] * lg[None, :]).astype(np.float32)
    kdec = np.tile(np.repeat(kdec, RET_QK_DIM, axis=1), (reps, 1))
    qdec = np.tile(np.repeat(qdec, RET_QK_DIM, axis=1), (reps, 1))
    chunk_decay = [float(v) for v in np.exp(cl * lg).astype(np.float32)]
    return dmat, kdec, qdec, chunk_decay


def _rope_tables(pos0, length, reps):
    half = RET_QK_DIM // 2
    inv = ROPE_BASE ** (-np.arange(half, dtype=np.float64) / half)
    pos = (pos0 + np.arange(length)).astype(np.float64)
    ang = pos[:, None] * inv[None, :]
    cos = np.cos(ang).astype(np.float32)
    sin = np.sin(ang).astype(np.float32)
    cos2 = np.tile(np.concatenate([cos, cos], axis=1), (reps, 1))
    sin2 = np.tile(np.concatenate([-sin, sin], axis=1), (reps, 1))
    return cos2, sin2


def _conv_shift_matrix():
    s = np.zeros(((CONV_WIDTH - 1) * CHUNK, CONV_K), np.float32)
    for j in range(CONV_WIDTH - 1):
        for t in range(CHUNK):
            s[j * CHUNK + t, CONV_TAIL + t - (CONV_WIDTH - 1) + j] = 1.0
    return s


def _expand_matrix():
    e = np.zeros((LANES, SSD_WIDTH), np.float32)
    for h in range(SSD_HEADS):
        e[h, h * SSD_HEAD_DIM:(h + 1) * SSD_HEAD_DIM] = 1.0
    return e


def _mixer_param_specs(layer):
    return [
        _layer_spec((CONV_WIDTH, CONV_DIM), layer),
        _layer_spec((1, CONV_DIM), layer),
        _layer_spec((1, LANES), layer),
        _layer_spec((1, LANES), layer),
        _const_spec((LANES, SSD_WIDTH)),
        _layer_spec((1, SSD_WIDTH), layer),
        _layer_spec((1, SSD_WIDTH), layer),
        _layer_spec((1, RET_WIDTH), layer),
    ]


def _mixer_params(pw):
    return (pw["conv_w"], pw["conv_b"], pw["dt_bias"], pw["a_log"], pw["expand"], pw["d_skip"],
            pw["ssd_gain"], pw["ret_gain"])


def _state_out_shapes(batch):
    return [
        jax.ShapeDtypeStruct((DEPTH, batch, CONV_WIDTH - 1, CONV_DIM), F32),
        jax.ShapeDtypeStruct((DEPTH, batch, SSD_WIDTH, SSD_STATE), F32),
        jax.ShapeDtypeStruct((DEPTH, batch, RET_HEADS, RET_QK_DIM, RET_V_DIM), F32),
    ]


def _mixer_prompt(layer, proj, dt_raw, pw, prev_states, batch, seq):
    nc = seq // CHUNK
    nbb = MIXER_SEQS_PER_STEP if batch % MIXER_SEQS_PER_STEP == 0 else 1
    dmat, kdec, qdec, chunk_decay = _ret_tables(CHUNK, 1)
    cos, sin = _rope_tables(0, seq, 1)
    seq_map = lambda b, c: (b, c, 0)
    kern = functools.partial(_mixer_prompt_kernel, nbb=nbb, ret_chunk_decay=chunk_decay)
    n_in = 16
    y, *states = pl.pallas_call(
        kern,
        grid=(batch // nbb, nc),
        in_specs=[
            pl.BlockSpec((nbb, CHUNK, PROJ_MAIN), seq_map),
            pl.BlockSpec((nbb, CHUNK, LANES), seq_map),
            *_mixer_param_specs(layer),
            pl.BlockSpec((CHUNK, LANES), lambda b, c: (c, 0)),
            pl.BlockSpec((CHUNK, LANES), lambda b, c: (c, 0)),
            _const_spec((RET_HEADS, CHUNK, CHUNK)),
            _const_spec((CHUNK, RET_QK_WIDTH)),
            _const_spec((CHUNK, RET_QK_WIDTH)),
            _const_spec(((CONV_WIDTH - 1) * CHUNK, CONV_K)),
            *[_alias_spec() for _ in prev_states],
        ],
        out_specs=[
            pl.BlockSpec((nbb, CHUNK, SSD_WIDTH + RET_WIDTH), seq_map),
            pl.BlockSpec((None, nbb, CONV_WIDTH - 1, CONV_DIM), lambda b, c: (layer, b, 0, 0)),
            pl.BlockSpec((None, nbb, SSD_WIDTH, SSD_STATE), lambda b, c: (layer, b, 0, 0)),
            pl.BlockSpec((None, nbb, RET_HEADS, RET_QK_DIM, RET_V_DIM), lambda b, c: (layer, b, 0, 0, 0)),
        ],
        out_shape=[jax.ShapeDtypeStruct((batch, seq, SSD_WIDTH + RET_WIDTH), BF16), *_state_out_shapes(batch)],
        input_output_aliases={n_in + k: 1 + k for k in range(len(prev_states))},
        scratch_shapes=[
            pltpu.VMEM((nbb, 2, CONV_TAIL, CONV_DIM), BF16),
            pltpu.VMEM((nbb, SSD_STATE, SSD_WIDTH), F32),
        ],
        compiler_params=_cparams(("parallel", "arbitrary")),
        name="mixer_prompt",
    )(proj.reshape(batch, seq, PROJ_MAIN), dt_raw.reshape(batch, seq, LANES), *_mixer_params(pw),
      jnp.asarray(cos), jnp.asarray(sin), jnp.asarray(dmat), jnp.asarray(kdec), jnp.asarray(qdec),
      jnp.asarray(_conv_shift_matrix(), dtype=BF16), *prev_states)
    return (y.reshape(batch * seq, SSD_WIDTH + RET_WIDTH), *states)


def _pad_rows(x, rows):
    return jnp.concatenate([x, jnp.zeros((rows - x.shape[0], x.shape[1]), x.dtype)], axis=0)


def _mixer_sample_kernel(proj_ref, dt_ref, convs_ref, ssm_in_ref, ret_in_ref, convw_ref, convb_ref, dtb_ref,
                         alog_ref, exp_ref, dskip_ref, sgain_ref, rgain_ref, cos_ref, sin_ref, dmat_ref,
                         kdec_ref, qdec_ref, tri_ref, *rest, bb, cl, ret_chunk_decay):
    y_ref, conv_ref, ssm_ref, ret_ref, xp_scr = rest[-5:]
    for i in range(bb):
        xp_scr[i, 8 - (CONV_WIDTH - 1):8, :] = convs_ref[i]
        xp_scr[i, 8:8 + cl, :] = proj_ref[i * cl:(i + 1) * cl, OFF_XBC:OFF_XBC + CONV_DIM]
        conv_ref[i] = xp_scr[i, 8 + cl - (CONV_WIDTH - 1):8 + cl, :]
    acc = None
    for j in range(CONV_WIDTH):
        s = 8 - (CONV_WIDTH - 1) + j
        tap = jnp.concatenate([xp_scr[i, s:s + cl, :] for i in range(bb)], axis=0) * convw_ref[j:j + 1, :]
        acc = convb_ref[...] + tap if acc is None else acc + tap
    xbc = _silu(acc)
    xs = xbc[:, 0:SSD_WIDTH]
    bm = xbc[:, SSD_WIDTH:SSD_WIDTH + SSD_GROUPS * SSD_STATE]
    cm = xbc[:, SSD_WIDTH + SSD_GROUPS * SSD_STATE:CONV_DIM]

    expand = exp_ref[...]
    dt, acum = _dt_terms(dt_ref[...], dtb_ref[...], alog_ref[...], tri_ref[...], expand)
    eacum = jnp.exp(acum)
    alast = jnp.concatenate(
        [jnp.broadcast_to(acum[(i + 1) * cl - 1:(i + 1) * cl, :], (cl, LANES)) for i in range(bb)], axis=0)
    dt_x = _dot_sel(dt, expand)
    eacum_x = _dot_sel(eacum, expand)
    dte_x = _dot_sel(jnp.exp(alast - acum), expand)
    xdt = xs * dt_x
    xdtd = xdt * dte_x
    row = lax.broadcasted_iota(jnp.int32, (cl, LANES), 0)
    col = lax.broadcasted_iota(jnp.int32, (cl, LANES), 1)
    causal = row >= col
    lane = col

    y_rows = []
    for i in range(bb):
        rs = slice(i * cl, (i + 1) * cl)
        acum_i = acum[rs]
        acum_t = _pad_rows(acum_i, LANES).T
        xdt_p = _pad_rows(xdt[rs], LANES).astype(BF16)
        xdtd_p = _pad_rows(xdtd[rs], LANES)
        chunk_decay = eacum[(i + 1) * cl - 1:(i + 1) * cl, :]
        y_parts = []
        for g in range(SSD_GROUPS):
            bg_p = _pad_rows(bm[rs, g * SSD_STATE:(g + 1) * SSD_STATE], LANES)
            cg_b = cm[rs, g * SSD_STATE:(g + 1) * SSD_STATE].astype(BF16)
            cb = _dot_nt(cg_b, bg_p.astype(BF16))
            gsl = slice(g * GROUP_WIDTH, (g + 1) * GROUP_WIDTH)
            st_g = ssm_in_ref[i, gsl, :]
            y_off = _dot_nt(cg_b, st_g.astype(BF16)) * eacum_x[rs, gsl]
            upd = _dot(xdtd_p[:, gsl].T.astype(BF16), bg_p.astype(BF16))
            for hh in range(SSD_HEADS // SSD_GROUPS):
                h = g * (SSD_HEADS // SSD_GROUPS) + hh
                hsl = slice(hh * SSD_HEAD_DIM, (hh + 1) * SSD_HEAD_DIM)
                ssm_ref[i, h * SSD_HEAD_DIM:(h + 1) * SSD_HEAD_DIM, :] = (
                    st_g[hsl, :] * chunk_decay[:, h:h + 1] + upd[hsl, :])
            for j in range(GROUP_WIDTH // LANES):
                h0 = g * (SSD_HEADS // SSD_GROUPS) + 2 * j
                psl = slice(h0 * SSD_HEAD_DIM, (h0 + 2) * SSD_HEAD_DIM)
                res = []
                for hh in (h0, h0 + 1):
                    seg = acum_i[:, hh:hh + 1] - acum_t[hh:hh + 1, :]
                    lmat = jnp.exp(jnp.where(causal, seg, -jnp.inf))
                    res.append(_dot((cb * lmat).astype(BF16), xdt_p[:, psl]))
                y_parts.append(jnp.where(lane < SSD_HEAD_DIM, res[0], res[1]) + y_off[:, j * LANES:(j + 1) * LANES])
        y_rows.append(jnp.concatenate(y_parts, axis=-1))
    y = jnp.concatenate(y_rows, axis=0)
    z = proj_ref[:, OFF_Z:OFF_Z + SSD_WIDTH]
    y_ref[:, 0:SSD_WIDTH] = _ssd_out(y, xs, z, dskip_ref[...], sgain_ref[...]).astype(y_ref.dtype)

    cos = cos_ref[...]
    sin = sin_ref[...]
    q_heads = _rotary(proj_ref[:, OFF_Q:OFF_Q + RET_QK_WIDTH], cos, sin)
    k_heads = _rotary(proj_ref[:, OFF_K:OFF_K + RET_QK_WIDTH], cos, sin)
    o_heads = []
    for h in range(RET_HEADS):
        hs = slice(h * RET_QK_DIM, (h + 1) * RET_QK_DIM)
        kh_all = k_heads[h] * (RET_QK_DIM ** -0.5)
        q_start = q_heads[h] * qdec_ref[:, hs]
        k_end = kh_all * kdec_ref[:, hs]
        o_rows = []
        for i in range(bb):
            rs = slice(i * cl, (i + 1) * cl)
            vh_p = _pad_rows(proj_ref[rs, OFF_V + h * RET_V_DIM:OFF_V + (h + 1) * RET_V_DIM], LANES).astype(BF16)
            kh_p = _pad_rows(kh_all[rs], LANES).astype(BF16)
            scores = _dot_nt(q_heads[h][rs].astype(BF16), kh_p) * dmat_ref[h]
            o_intra = _dot(scores.astype(BF16), vh_p)
            s_in = ret_in_ref[i, h]
            o_cross = _dot(q_start[rs].astype(BF16), s_in.astype(BF16))
            kv = _dot(_pad_rows(k_end[rs], LANES).T.astype(BF16), vh_p)
            ret_ref[i, h] = ret_chunk_decay[h] * s_in + kv
            o_rows.append(o_intra + o_cross)
        o_heads.append(jnp.concatenate(o_rows, axis=0))
    gate = proj_ref[:, OFF_G:OFF_G + RET_WIDTH]
    y_ref[:, SSD_WIDTH:SSD_WIDTH + RET_WIDTH] = _ret_out(o_heads, gate, rgain_ref[...]).astype(y_ref.dtype)


def _mixer_sample(layer, proj, dt_raw, conv_state, ssm_state, ret_state, pw, prev_states, batch, cl, bb):
    dmat, kdec, qdec, chunk_decay = _ret_tables(cl, bb)
    dmat = np.concatenate([dmat, np.zeros((RET_HEADS, cl, LANES - cl), np.float32)], axis=-1)
    cos, sin = _rope_tables(PAST_LEN, cl, bb)
    m = bb * cl
    tri = np.kron(np.eye(bb, dtype=np.float32), np.tril(np.ones((cl, cl), np.float32)))
    kern = functools.partial(_mixer_sample_kernel, bb=bb, cl=cl, ret_chunk_decay=chunk_decay)
    row_map = lambda i: (i, 0)
    state_specs = [
        pl.BlockSpec((None, bb, CONV_WIDTH - 1, CONV_DIM), lambda i: (layer, i, 0, 0)),
        pl.BlockSpec((None, bb, SSD_WIDTH, SSD_STATE), lambda i: (layer, i, 0, 0)),
        pl.BlockSpec((None, bb, RET_HEADS, RET_QK_DIM, RET_V_DIM), lambda i: (layer, i, 0, 0, 0)),
    ]
    n_in = 19
    return pl.pallas_call(
        kern,
        grid=(batch // bb,),
        in_specs=[
            pl.BlockSpec((m, PROJ_MAIN), row_map),
            pl.BlockSpec((m, LANES), row_map),
            *state_specs,
            *_mixer_param_specs(layer),
            _const_spec((m, LANES)),
            _const_spec((m, LANES)),
            _const_spec((RET_HEADS, cl, LANES)),
            _const_spec((m, RET_QK_WIDTH)),
            _const_spec((m, RET_QK_WIDTH)),
            _const_spec((m, m)),
            *[_alias_spec() for _ in prev_states],
        ],
        out_specs=[pl.BlockSpec((m, SSD_WIDTH + RET_WIDTH), row_map), *state_specs],
        out_shape=[jax.ShapeDtypeStruct((batch * cl, SSD_WIDTH + RET_WIDTH), BF16), *_state_out_shapes(batch)],
        input_output_aliases={n_in + k: 1 + k for k in range(len(prev_states))},
        scratch_shapes=[pltpu.VMEM((bb, 8 + cl, CONV_DIM), F32)],
        compiler_params=_cparams(("parallel",)),
        name="mixer_sample",
    )(proj, dt_raw, conv_state, ssm_state, ret_state, *_mixer_params(pw), jnp.asarray(cos), jnp.asarray(sin),
      jnp.asarray(dmat), jnp.asarray(kdec), jnp.asarray(qdec), jnp.asarray(tri, dtype=BF16), *prev_states)


def _outproj_kernel(x_ref, y_ref, w_ref, o_ref):
    o_ref[...] = x_ref[...] + _dot(y_ref[...], w_ref[...])


def _outproj(x, y, layer, pw, tm):
    t = x.shape[0]
    k = y.shape[1]
    return pl.pallas_call(
        _outproj_kernel,
        grid=(t // tm,),
        in_specs=[
            pl.BlockSpec((tm, D_MODEL), lambda i: (i, 0)),
            pl.BlockSpec((tm, k), lambda i: (i, 0)),
            _layer_spec((k, D_MODEL), layer),
        ],
        out_specs=pl.BlockSpec((tm, D_MODEL), lambda i: (i, 0)),
        out_shape=jax.ShapeDtypeStruct((t, D_MODEL), F32),
        compiler_params=_cparams(("parallel",)),
        name="outproj",
    )(x, y, pw["w_out"])


def _memproj_kernel(m_ref, w_ref, k_ref, v_ref, kb_ref, vb_ref, *, bb):
    r = _dot(m_ref[...].reshape(bb * N_MEM, D_MODEL).astype(BF16), w_ref[...])
    kb_ref[...] = r[:, 0:D_MODEL].astype(BF16).reshape(bb, N_MEM, D_MODEL)
    vb_ref[...] = r[:, D_MODEL:2 * D_MODEL].astype(BF16).reshape(bb, N_MEM, D_MODEL)
    for i in range(bb):
        rows = slice(i * N_MEM, (i + 1) * N_MEM)
        for hd in range(XA_HEADS):
            k_ref[i, :, hd, :] = r[rows, hd * XA_HEAD_DIM:(hd + 1) * XA_HEAD_DIM]
            v_ref[i, :, hd, :] = r[rows, D_MODEL + hd * XA_HEAD_DIM:D_MODEL + (hd + 1) * XA_HEAD_DIM]


def _memproj(mem, pw, bb):
    b = mem.shape[0]
    cache_spec = pl.BlockSpec((None, bb, N_MEM, XA_HEADS, XA_HEAD_DIM), lambda l, i: (l, i, 0, 0, 0))
    cache_shape = jax.ShapeDtypeStruct((DEPTH, b, N_MEM, XA_HEADS, XA_HEAD_DIM), F32)
    flat_spec = pl.BlockSpec((None, bb, N_MEM, D_MODEL), lambda l, i: (l, i, 0, 0))
    flat_shape = jax.ShapeDtypeStruct((DEPTH, b, N_MEM, D_MODEL), BF16)
    return pl.pallas_call(
        functools.partial(_memproj_kernel, bb=bb),
        grid=(DEPTH, b // bb),
        in_specs=[
            pl.BlockSpec((bb, N_MEM, D_MODEL), lambda l, i: (i, 0, 0)),
            pl.BlockSpec((None, D_MODEL, 2 * D_MODEL), lambda l, i: (l, 0, 0)),
        ],
        out_specs=[cache_spec, cache_spec, flat_spec, flat_spec],
        out_shape=[cache_shape, cache_shape, flat_shape, flat_shape],
        compiler_params=_cparams(("parallel", "parallel")),
        name="memproj",
    )(mem, pw["w_kv"])


def _softmax_rows(s):
    e = jnp.exp(s - jnp.max(s, axis=-1, keepdims=True))
    return e / jnp.sum(e, axis=-1, keepdims=True)


def _xattn_prompt_kernel(x_ref, g_ref, wq_ref, wo_ref, k_ref, v_ref, gffn_ref, wr_hi_ref, wr_lo_ref, br_ref,
                         o_ref, h_ref, info_ref):
    x = x_ref[0]
    h = (_rms(x) * g_ref[...]).astype(BF16)
    q = _dot(h, wq_ref[...])
    scale = XA_HEAD_DIM ** -0.5
    heads = []
    for hd in range(XA_HEADS):
        hs = slice(hd * XA_HEAD_DIM, (hd + 1) * XA_HEAD_DIM)
        p = _softmax_rows(_dot_nt(q[:, hs].astype(BF16), k_ref[0, :, hs]) * scale)
        heads.append(_dot(p.astype(BF16), v_ref[0, :, hs]))
    o = jnp.concatenate(heads, axis=-1).astype(BF16)
    xo = x + _dot(o, wo_ref[...])
    o_ref[0] = xo
    _route_rows(xo, gffn_ref, wr_hi_ref, wr_lo_ref, br_ref, h_ref, info_ref)


def _xattn_cache_kernel(x_ref, g_ref, wq_ref, wo_ref, k_ref, v_ref, o_ref, *, bb, tm):
    x = x_ref[...].reshape(bb * tm, D_MODEL)
    h = (_rms(x) * g_ref[...]).astype(BF16)
    q = _dot(h, wq_ref[...])
    scale = XA_HEAD_DIM ** -0.5
    rows_flat = XA_HEADS * tm
    assert tm & (tm - 1) == 0 and XA_HEADS & (XA_HEADS - 1) == 0
    row_head = lax.shift_right_logical(lax.broadcasted_iota(jnp.int32, (rows_flat, N_MEM * XA_HEADS), 0),
                                       int(math.log2(tm)))
    col_head = lax.broadcasted_iota(jnp.int32, (rows_flat, N_MEM * XA_HEADS), 1) & (XA_HEADS - 1)
    own_head = row_head == col_head
    outs = []
    for i in range(bb):
        qi = q[i * tm:(i + 1) * tm]
        qf = jnp.concatenate([qi[:, hd * XA_HEAD_DIM:(hd + 1) * XA_HEAD_DIM] for hd in range(XA_HEADS)], axis=0)
        k_all = k_ref[i].reshape(N_MEM * XA_HEADS, XA_HEAD_DIM).astype(BF16)
        v_all = v_ref[i].reshape(N_MEM * XA_HEADS, XA_HEAD_DIM).astype(BF16)
        s = jnp.where(own_head, _dot_nt(qf.astype(BF16), k_all) * scale, -jnp.inf)
        of = _dot(_softmax_rows(s).astype(BF16), v_all)
        outs.append(jnp.concatenate([of[hd * tm:(hd + 1) * tm] for hd in range(XA_HEADS)], axis=-1))
    o = jnp.concatenate(outs, axis=0).astype(BF16)
    o_ref[...] = (x + _dot(o, wo_ref[...])).reshape(bb, tm, D_MODEL)


def _xattn_weight_specs(layer):
    return [_layer_spec((1, D_MODEL), layer), _layer_spec((D_MODEL, D_MODEL), layer),
            _layer_spec((D_MODEL, D_MODEL), layer)]


def _xattn_prompt(x, layer, pw, mem_k, mem_v, tm):
    b, l, _ = x.shape
    per_seq = l // tm
    mem_spec = pl.BlockSpec((None, 1, N_MEM, D_MODEL), lambda i, j: (layer, i, 0, 0))
    return pl.pallas_call(
        _xattn_prompt_kernel,
        grid=(b, per_seq),
        in_specs=[pl.BlockSpec((1, tm, D_MODEL), lambda i, j: (i, j, 0)), *_xattn_weight_specs(layer),
                  mem_spec, mem_spec, *_router_specs(layer)],
        out_specs=[pl.BlockSpec((1, tm, D_MODEL), lambda i, j: (i, j, 0)),
                   pl.BlockSpec((tm, *TOKEN_TILE), lambda i, j: (i * per_seq + j, 0, 0)),
                   pl.BlockSpec((tm, LANES), lambda i, j: (i * per_seq + j, 0))],
        out_shape=[jax.ShapeDtypeStruct((b, l, D_MODEL), F32),
                   jax.ShapeDtypeStruct((b * l, *TOKEN_TILE), F32),
                   jax.ShapeDtypeStruct((b * l, LANES), F32)],
        compiler_params=_cparams(("parallel", "parallel")),
        name="xattn_prompt",
    )(x, pw["norm_mem"], pw["w_mq"], pw["w_mo"], mem_k, mem_v, *_router_params(pw))


def _xattn_cache(x, layer, pw, mem_k, mem_v, bb):
    b, l, _ = x.shape
    kern = functools.partial(_xattn_cache_kernel, bb=bb, tm=l)
    mem_spec = pl.BlockSpec((None, bb, N_MEM, XA_HEADS, XA_HEAD_DIM), lambda i: (layer, i, 0, 0, 0))
    return pl.pallas_call(
        kern,
        grid=(b // bb,),
        in_specs=[pl.BlockSpec((bb, l, D_MODEL), lambda i: (i, 0, 0)), *_xattn_weight_specs(layer),
                  mem_spec, mem_spec],
        out_specs=pl.BlockSpec((bb, l, D_MODEL), lambda i: (i, 0, 0)),
        out_shape=jax.ShapeDtypeStruct((b, l, D_MODEL), F32),
        compiler_params=_cparams(("parallel",)),
        name="xattn_cache",
    )(x, pw["norm_mem"], pw["w_mq"], pw["w_mo"], mem_k, mem_v)


def _router_gates(logits):
    m = logits.shape[0]
    lane_i = lax.broadcasted_iota(jnp.int32, (m, LANES), 1)
    lane = lane_i.astype(F32)
    big = float(LANES)
    is_g = lane_i < N_EGROUPS
    gl = jnp.where(is_g, logits, -jnp.inf)
    gmax = jnp.max(gl, axis=-1, keepdims=True)
    g_idx = jnp.min(jnp.where(is_g & (gl == gmax), lane, big), axis=-1, keepdims=True)
    g_prob = 1.0 / jnp.sum(jnp.exp(gl - gmax), axis=-1, keepdims=True)
    e_lane = lane_i - ROUTER_OFF
    e_group = lax.shift_right_arithmetic(e_lane, int(math.log2(EXPERTS_PER_GROUP))).astype(F32)
    sel = (e_lane >= 0) & (e_lane < N_EXPERTS) & (e_group == g_idx)
    el = jnp.where(sel, logits, -jnp.inf)
    emax = jnp.max(el, axis=-1, keepdims=True)
    ee = jnp.exp(el - emax)
    e_prob = ee / jnp.sum(ee, axis=-1, keepdims=True)
    p1 = jnp.max(jnp.where(sel, e_prob, -1.0), axis=-1, keepdims=True)
    i1 = jnp.min(jnp.where(sel & (e_prob == p1), lane, big), axis=-1, keepdims=True)
    sel2 = sel & (lane != i1)
    p2 = jnp.max(jnp.where(sel2, e_prob, -1.0), axis=-1, keepdims=True)
    i2 = jnp.min(jnp.where(sel2 & (e_prob == p2), lane, big), axis=-1, keepdims=True)
    denom = p1 + p2
    w1 = g_prob * p1 / denom
    w2 = g_prob * p2 / denom
    return jnp.where(lane == i1, w1, 0.0) + jnp.where(lane == i2, w2, 0.0), g_idx


def _moe_kernel(x_ref, g_ref, wr_hi_ref, wr_lo_ref, br_ref, wg_ref, wu_ref, wd_ref, gf_ref, o_ref,
                h_scr, gate_scr, *, final_norm):
    e = pl.program_id(1)

    @pl.when(e == 0)
    def _():
        h = _rms(x_ref[...]) * g_ref[...]
        h_hi = h.astype(BF16)
        h_lo = (h - h_hi.astype(F32)).astype(BF16)
        logits = (_dot(h_hi, wr_hi_ref[...]) + _dot(h_hi, wr_lo_ref[...]) + _dot(h_lo, wr_hi_ref[...])
                  + br_ref[...])
        h_scr[...] = h_hi
        gate_scr[...] = _router_gates(logits)[0]
        o_ref[...] = jnp.zeros_like(o_ref)

    hb = h_scr[...]
    lane = lax.broadcasted_iota(jnp.int32, gate_scr.shape, 1)
    gate_e = jnp.sum(jnp.where(lane == e + ROUTER_OFF, gate_scr[...], 0.0), axis=-1, keepdims=True)
    a = _silu(_dot(hb, wg_ref[...])) * _dot(hb, wu_ref[...]) * gate_e
    o_ref[...] += _dot(a.astype(BF16), wd_ref[...])

    @pl.when(e == pl.num_programs(1) - 1)
    def _():
        y = x_ref[...] + o_ref[...]
        if final_norm:
            y = _rms(y) * gf_ref[...]
        o_ref[...] = y


def _moe(x, layer, pw, final_norm, tm):
    t = x.shape[0]
    kern = functools.partial(_moe_kernel, final_norm=final_norm)
    return pl.pallas_call(
        kern,
        grid=(t // tm, N_EXPERTS),
        in_specs=[
            pl.BlockSpec((tm, D_MODEL), lambda i, e: (i, 0)),
            _layer_spec((1, D_MODEL), layer),
            _layer_spec((D_MODEL, LANES), layer),
            _layer_spec((D_MODEL, LANES), layer),
            _layer_spec((1, LANES), layer),
            pl.BlockSpec((None, None, D_MODEL, EXPERT_FF), lambda i, e: (layer, e, 0, 0)),
            pl.BlockSpec((None, None, D_MODEL, EXPERT_FF), lambda i, e: (layer, e, 0, 0)),
            pl.BlockSpec((None, None, EXPERT_FF, D_MODEL), lambda i, e: (layer, e, 0, 0)),
            _const_spec((1, D_MODEL)),
        ],
        out_specs=pl.BlockSpec((tm, D_MODEL), lambda i, e: (i, 0)),
        out_shape=jax.ShapeDtypeStruct((t, D_MODEL), F32),
        scratch_shapes=[pltpu.VMEM((tm, D_MODEL), BF16), pltpu.VMEM((tm, LANES), F32)],
        compiler_params=_cparams(("parallel", "arbitrary")),
        name="moe",
    )(x, pw["norm_ffn"], pw["wr_hi"], pw["wr_lo"], pw["b_router"], pw["w_gate"], pw["w_up"], pw["w_down"],
      pw["norm_final"])


SORT_TILE = 2048
MOE_M = 256


def _nb(ts):
    return ts // MOE_M + N_EGROUPS - 1


def _rows(ts):
    return _nb(ts) * MOE_M


def _sort_tile(t):
    if t % SORT_TILE == 0:
        return SORT_TILE
    if t < SORT_TILE and t % PLAN_BLK == 0:
        return t
    return None
SUBLANES = 8
TOKEN_TILE = (SUBLANES, D_MODEL // SUBLANES)
PLAN_BLK = 512


def _route_rows(x, g_ref, wr_hi_ref, wr_lo_ref, br_ref, h_ref, info_ref):
    h = _rms(x) * g_ref[...]
    h_hi = h.astype(BF16)
    h_lo = (h - h_hi.astype(F32)).astype(BF16)
    both = _dot(h_hi, jnp.concatenate([wr_hi_ref[...], wr_lo_ref[...]], axis=1))
    logits = both[:, 0:LANES] + both[:, LANES:2 * LANES] + _dot(h_lo, wr_hi_ref[...]) + br_ref[...]
    gates, g_idx = _router_gates(logits)
    lane = lax.broadcasted_iota(jnp.int32, gates.shape, 1)
    info_ref[...] = jnp.where(lane == 0, g_idx, gates)
    h_ref[...] = h.reshape(h.shape[0], *TOKEN_TILE)


def _moe_route_kernel(x_ref, g_ref, wr_hi_ref, wr_lo_ref, br_ref, h_ref, info_ref):
    _route_rows(x_ref[...], g_ref, wr_hi_ref, wr_lo_ref, br_ref, h_ref, info_ref)


def _router_specs(layer):
    return [_layer_spec((1, D_MODEL), layer), _layer_spec((D_MODEL, LANES), layer),
            _layer_spec((D_MODEL, LANES), layer), _layer_spec((1, LANES), layer)]


def _router_params(pw):
    return (pw["norm_ffn"], pw["wr_hi"], pw["wr_lo"], pw["b_router"])


def _moe_route(x, layer, pw, tm):
    t = x.shape[0]
    return pl.pallas_call(
        _moe_route_kernel,
        grid=(t // tm,),
        in_specs=[pl.BlockSpec((tm, D_MODEL), lambda i: (i, 0)), *_router_specs(layer)],
        out_specs=[pl.BlockSpec((tm, *TOKEN_TILE), lambda i: (i, 0, 0)),
                   pl.BlockSpec((tm, LANES), lambda i: (i, 0))],
        out_shape=[jax.ShapeDtypeStruct((t, *TOKEN_TILE), F32), jax.ShapeDtypeStruct((t, LANES), F32)],
        compiler_params=_cparams(("parallel",)),
        name="moe_route",
    )(x, *_router_params(pw))


def _lane_pick(v, idx):
    lane = lax.broadcasted_iota(jnp.int32, v.shape, 1)
    return jnp.sum(jnp.where(lane == idx, v, 0.0), axis=-1, keepdims=True)


def _moe_plan_kernel(info_ref, dest_ref, items_ref, *, n_tiles, ts):
    row = lax.broadcasted_iota(jnp.int32, (PLAN_BLK, PLAN_BLK), 0)
    col = lax.broadcasted_iota(jnp.int32, (PLAN_BLK, PLAN_BLK), 1)
    tri = jnp.where(row >= col, 1.0, 0.0).astype(BF16)
    row = lax.broadcasted_iota(jnp.int32, (LANES, LANES), 0)
    col = lax.broadcasted_iota(jnp.int32, (LANES, LANES), 1)
    before = jnp.where(row < col, 1.0, 0.0).astype(BF16)
    lane = lax.broadcasted_iota(jnp.int32, (PLAN_BLK, LANES), 1)
    lane_row = lax.broadcasted_iota(jnp.int32, (1, LANES), 1)
    block_groups = jnp.zeros((1, LANES), F32)
    for s in range(n_tiles):
        carry = jnp.zeros((1, LANES), F32)
        parts = []
        for blk in range(ts // PLAN_BLK):
            r0 = s * ts + blk * PLAN_BLK
            gid = info_ref[r0:r0 + PLAN_BLK, 0:1]
            onehot = jnp.where((lane < N_EGROUPS) & (lane.astype(F32) == gid), 1.0, 0.0)
            cum = _dot(tri, onehot.astype(BF16)) + carry
            carry = cum[PLAN_BLK - 1:PLAN_BLK, :]
            parts.append((onehot, cum))
        counts = carry
        padded = jnp.ceil(counts * (1.0 / MOE_M)) * MOE_M
        seg_start = _dot_sel(jnp.broadcast_to(padded, (8, LANES)), before)[0:1, :]
        rows = []
        for onehot, cum in parts:
            dest_col = jnp.sum(onehot * (seg_start + cum - 1.0), axis=-1, keepdims=True)
            for q in range(PLAN_BLK // LANES):
                piece = jnp.broadcast_to(dest_col[q * LANES:(q + 1) * LANES, :], (LANES, LANES))
                rows.append(piece.T[0:1, :])
        n_rows = ts // LANES
        dest_ref[s * n_rows:(s + 1) * n_rows, :] = jnp.concatenate(rows, axis=0).astype(jnp.int32)
        blk_start = lane_row.astype(F32) * MOE_M
        group_of_block = jnp.full((1, LANES), -1.0, F32)
        for g in range(N_EGROUPS):
            start_g = _lane_pick(seg_start, g)
            size_g = _lane_pick(padded, g)
            group_of_block = jnp.where((blk_start >= start_g) & (blk_start < start_g + size_g), float(g),
                                       group_of_block)
        in_tile = jnp.where(lane_row < _nb(ts), group_of_block + 1.0, 0.0)
        if s:
            in_tile = pltpu.roll(jnp.broadcast_to(in_tile, (SUBLANES, LANES)), s * _nb(ts), axis=1)[0:1]
        block_groups = block_groups + in_tile

    n_items = n_tiles * _nb(ts)
    group = block_groups - 1.0
    lane_f = lane_row.astype(F32)
    order_key = jnp.where(group >= 0, group, float(N_EGROUPS)) * LANES + lane_f
    order_key = jnp.where(lane_row < n_items, order_key, float((N_EGROUPS + 1) * LANES) + lane_f)
    key_by_lane = jnp.broadcast_to(order_key, (LANES, LANES))
    key_by_row = key_by_lane.T
    rank = jnp.sum(jnp.where(key_by_lane < key_by_row, 1.0, 0.0), axis=-1, keepdims=True)
    chosen = rank == lane_f
    block_id = lax.broadcasted_iota(jnp.int32, (LANES, LANES), 0).astype(F32)
    group_by_row = jnp.broadcast_to(group, (LANES, LANES)).T
    item_block = jnp.sum(jnp.where(chosen, block_id, 0.0), axis=0, keepdims=True)
    item_group = jnp.sum(jnp.where(chosen, group_by_row, 0.0), axis=0, keepdims=True)
    item_group = jnp.where(item_group >= 0, item_group, -float(N_EGROUPS))
    items_ref[0:1, :] = item_block.astype(jnp.int32)
    items_ref[1:2, :] = item_group.astype(jnp.int32)


def _moe_plan(info, n_tiles, ts):
    t = info.shape[0]
    assert n_tiles * _nb(ts) <= LANES
    return pl.pallas_call(
        functools.partial(_moe_plan_kernel, n_tiles=n_tiles, ts=ts),
        grid=(1,),
        in_specs=[pl.BlockSpec((t, LANES), lambda i: (0, 0))],
        out_specs=[pl.BlockSpec((t // LANES, LANES), lambda i: (0, 0)),
                   pl.BlockSpec((2, LANES), lambda i: (0, 0))],
        out_shape=[jax.ShapeDtypeStruct((t // LANES, LANES), jnp.int32),
                   jax.ShapeDtypeStruct((2, LANES), jnp.int32)],
        compiler_params=_cparams(("arbitrary",)),
        name="moe_plan",
    )(info)


def _dest_row(dest_ref, token):
    return dest_ref[token]


def _moe_permute_kernel(dest_ref, h_ref, info_ref, hs_ref, infos_ref, *, ts):
    base = pl.program_id(0) * ts
    hs_ref[...] = jnp.zeros_like(hs_ref)
    infos_ref[...] = jnp.zeros_like(infos_ref)

    def body(t, carry):
        d = _dest_row(dest_ref, base + t)
        hs_ref[d] = h_ref[t]
        infos_ref[pl.ds(d, 1), :] = info_ref[pl.ds(t, 1), :]
        return carry

    lax.fori_loop(0, ts, body, 0, unroll=8)


def _moe_permute(dest, h, info, n_tiles, ts):
    return pl.pallas_call(
        functools.partial(_moe_permute_kernel, ts=ts),
        grid_spec=pltpu.PrefetchScalarGridSpec(
            num_scalar_prefetch=1,
            grid=(n_tiles,),
            in_specs=[pl.BlockSpec((ts, *TOKEN_TILE), lambda s, d: (s, 0, 0)),
                      pl.BlockSpec((ts, LANES), lambda s, d: (s, 0))],
            out_specs=[pl.BlockSpec((_rows(ts), *TOKEN_TILE), lambda s, d: (s, 0, 0)),
                       pl.BlockSpec((_rows(ts), LANES), lambda s, d: (s, 0))],
        ),
        out_shape=[jax.ShapeDtypeStruct((n_tiles * _rows(ts), *TOKEN_TILE), F32),
                   jax.ShapeDtypeStruct((n_tiles * _rows(ts), LANES), F32)],
        compiler_params=_cparams(("arbitrary",)),
        name="moe_permute",
    )(dest, h, info)


def _moe_ffn_kernel(item_block_ref, item_group_ref, x_ref, gates_ref, wg_ref, wu_ref, wd_ref, y_ref):
    group = item_group_ref[pl.program_id(0)]

    @pl.when(group >= 0)
    def _():
        xb = x_ref[...].reshape(MOE_M, D_MODEL).astype(BF16)
        gates = gates_ref[...]
        lane = lax.broadcasted_iota(jnp.int32, gates.shape, 1)
        y = None
        for e in range(EXPERTS_PER_GROUP):
            gate_e = jnp.sum(jnp.where(lane == ROUTER_OFF + group * EXPERTS_PER_GROUP + e, gates, 0.0),
                             axis=-1, keepdims=True)
            a = _silu(_dot(xb, wg_ref[e])) * _dot(xb, wu_ref[e]) * gate_e
            d = _dot(a.astype(BF16), wd_ref[e])
            y = d if y is None else y + d
        y_ref[...] = y.reshape(MOE_M, *TOKEN_TILE)

    @pl.when(group < 0)
    def _():
        y_ref[...] = jnp.zeros_like(y_ref)


def _moe_ffn(item_block, item_group, hs, infos, layer, pw, n_tiles, ts):
    def w_map(w, blk, grp):
        g = grp[w]
        return (layer, jnp.where(g >= 0, g, -1 - g), 0, 0, 0)

    def grouped(w):
        return w.reshape(DEPTH, N_EGROUPS, EXPERTS_PER_GROUP, *w.shape[2:])

    tile_spec = pl.BlockSpec((MOE_M, *TOKEN_TILE), lambda w, blk, grp: (blk[w], 0, 0))
    return pl.pallas_call(
        _moe_ffn_kernel,
        grid_spec=pltpu.PrefetchScalarGridSpec(
            num_scalar_prefetch=2,
            grid=(n_tiles * _nb(ts),),
            in_specs=[
                tile_spec,
                pl.BlockSpec((MOE_M, LANES), lambda w, blk, grp: (blk[w], 0)),
                pl.BlockSpec((None, None, EXPERTS_PER_GROUP, D_MODEL, EXPERT_FF), w_map),
                pl.BlockSpec((None, None, EXPERTS_PER_GROUP, D_MODEL, EXPERT_FF), w_map),
                pl.BlockSpec((None, None, EXPERTS_PER_GROUP, EXPERT_FF, D_MODEL), w_map),
            ],
            out_specs=tile_spec,
        ),
        out_shape=jax.ShapeDtypeStruct((n_tiles * _rows(ts), *TOKEN_TILE), F32),
        compiler_params=_cparams(("arbitrary",)),
        name="moe_ffn",
    )(item_block, item_group, hs, infos, grouped(pw["w_gate"]), grouped(pw["w_up"]), grouped(pw["w_down"]))


def _moe_unpermute_kernel(dest_ref, x_ref, ys_ref, gf_ref, o_ref, y_scr, *, ts, tm, final_norm):
    base = pl.program_id(0) * ts + pl.program_id(1) * tm

    def body(t, carry):
        y_scr[t] = ys_ref[_dest_row(dest_ref, base + t)]
        return carry

    lax.fori_loop(0, tm, body, 0, unroll=8)
    y = x_ref[...] + y_scr[...].reshape(tm, D_MODEL)
    if final_norm:
        y = _rms(y) * gf_ref[...]
    o_ref[...] = y


def _moe_unpermute(dest, x, ys, pw, final_norm, n_tiles, ts, tm):
    per_tile = ts // tm
    return pl.pallas_call(
        functools.partial(_moe_unpermute_kernel, ts=ts, tm=tm, final_norm=final_norm),
        grid_spec=pltpu.PrefetchScalarGridSpec(
            num_scalar_prefetch=1,
            grid=(n_tiles, per_tile),
            in_specs=[
                pl.BlockSpec((tm, D_MODEL), lambda s, i, d: (s * per_tile + i, 0)),
                pl.BlockSpec((_rows(ts), *TOKEN_TILE), lambda s, i, d: (s, 0, 0)),
                pl.BlockSpec((1, D_MODEL), lambda s, i, d: (0, 0), pipeline_mode=pl.Buffered(1)),
            ],
            out_specs=pl.BlockSpec((tm, D_MODEL), lambda s, i, d: (s * per_tile + i, 0)),
            scratch_shapes=[pltpu.VMEM((tm, *TOKEN_TILE), F32)],
        ),
        out_shape=jax.ShapeDtypeStruct(x.shape, F32),
        compiler_params=_cparams(("arbitrary", "arbitrary")),
        name="moe_unpermute",
    )(dest, x, ys, pw["norm_final"])


def _moe_sorted(x, layer, pw, final_norm, ts, routed=None):
    n_tiles = x.shape[0] // ts
    tm = min(ts, 1024)
    h, info = routed if routed is not None else _moe_route(x, layer, pw, tm)
    dest, items = _moe_plan(info, n_tiles, ts)
    dest = dest.reshape(-1)
    hs, infos = _moe_permute(dest, h, info, n_tiles, ts)
    ys = _moe_ffn(items[0], items[1], hs, infos, layer, pw, n_tiles, ts)
    return _moe_unpermute(dest, x, ys, pw, final_norm, n_tiles, ts, tm)


def _row(v):
    return v.reshape(v.shape[0], 1, v.shape[1])


def _pad_lanes(v):
    return jnp.pad(v, ((0, 0),) * (v.ndim - 1) + ((0, LANES - v.shape[-1]),))


def _prep_weights(norm_mix, w_in, conv_w, conv_b, dt_bias, a_log, d_skip, ssd_gain, ret_gain, w_out,
                  norm_mem, w_mq, w_mk, w_mv, w_mo, norm_ffn, w_rg, b_rg, w_re, b_re, w_gate, w_up, w_down,
                  norm_final):
    dt_off = SSD_WIDTH + CONV_DIM
    w_main = jnp.concatenate([w_in[:, :, :dt_off], w_in[:, :, dt_off + SSD_HEADS:]], axis=2).astype(BF16)
    w_dt = _pad_lanes(w_in[:, :, dt_off:dt_off + SSD_HEADS]).astype(BF16)
    w_router = _pad_lanes(jnp.concatenate([w_rg, w_re], axis=2))
    wr_hi = w_router.astype(BF16)
    wr_lo = (w_router - wr_hi.astype(F32)).astype(BF16)
    return dict(
        norm_mix=_row(norm_mix), w_main=w_main, w_dt=w_dt,
        conv_w=conv_w, conv_b=_row(conv_b),
        dt_bias=_row(_pad_lanes(dt_bias)), a_log=_row(_pad_lanes(a_log)),
        expand=jnp.asarray(_expand_matrix(), dtype=BF16),
        d_skip=_row(jnp.repeat(d_skip, SSD_HEAD_DIM, axis=1)),
        ssd_gain=_row(ssd_gain), ret_gain=_row(ret_gain),
        w_out=w_out.astype(BF16),
        norm_mem=_row(norm_mem), w_mq=w_mq.astype(BF16), w_mo=w_mo.astype(BF16),
        w_kv=jnp.concatenate([w_mk, w_mv], axis=2).astype(BF16),
        norm_ffn=_row(norm_ffn), wr_hi=wr_hi, wr_lo=wr_lo,
        b_router=_row(_pad_lanes(jnp.concatenate([b_rg, b_re], axis=1))),
        w_gate=w_gate.astype(BF16), w_up=w_up.astype(BF16), w_down=w_down.astype(BF16),
        norm_final=norm_final.reshape(1, -1),
    )


def _token_tile(t, cap):
    tm = min(t, cap)
    assert t % tm == 0
    return tm


def _trunk(x, mixer_fn, xattn_fn, pw, proj_dtype):
    b, l, _ = x.shape
    t = b * l
    xf = x.reshape(t, D_MODEL)
    states = ()
    for layer in range(DEPTH):
        proj, dt_raw = _inproj(xf, layer, pw, proj_dtype, _token_tile(t, 512))
        y, *states = mixer_fn(layer, proj, dt_raw, tuple(states))
        xf = _outproj(xf, y, layer, pw, _token_tile(t, 1024))
        xf, routed = xattn_fn(layer, xf.reshape(b, l, D_MODEL))
        xf = xf.reshape(t, D_MODEL)
        ts = _sort_tile(t)
        if ts is not None:
            xf = _moe_sorted(xf, layer, pw, layer == DEPTH - 1, ts, routed)
        else:
            xf = _moe(xf, layer, pw, layer == DEPTH - 1, _token_tile(t, 1024))
    conv, ssm, ret = states
    return (xf.reshape(b, l, D_MODEL), ssm.reshape(DEPTH, b, SSD_HEADS, SSD_HEAD_DIM, SSD_STATE), conv, ret)


def kernel(x_prompt, x_sample, mem_prompt, state_ssm, state_conv, state_ret, cache_mem_k, cache_mem_v,
           norm_mix, w_in, conv_w, conv_b, dt_bias, a_log, d_skip, ssd_gain, ret_gain, w_out,
           norm_mem, w_mq, w_mk, w_mv, w_mo, norm_ffn, w_rg, b_rg, w_re, b_re, w_gate, w_up, w_down,
           norm_final):
    pw = _prep_weights(norm_mix, w_in, conv_w, conv_b, dt_bias, a_log, d_skip, ssd_gain, ret_gain, w_out,
                       norm_mem, w_mq, w_mk, w_mv, w_mo, norm_ffn, w_rg, b_rg, w_re, b_re, w_gate, w_up, w_down,
                       norm_final)
    bp, lp, _ = x_prompt.shape
    bs, ls, _ = x_sample.shape
    n_mem = mem_prompt.shape[1]

    assert n_mem == N_MEM
    mem_k_p, mem_v_p, mem_k_rows, mem_v_rows = _memproj(mem_prompt, pw, _token_tile(bp, 2))

    def mixer_p(layer, proj, dt_raw, prev_states):
        return _mixer_prompt(layer, proj, dt_raw, pw, prev_states, bp, lp)

    def xattn_p(layer, x):
        xo, h, info = _xattn_prompt(x, layer, pw, mem_k_rows, mem_v_rows, _token_tile(lp, 1024))
        return xo, (h, info)

    y_prompt, ssm_p, conv_p, ret_p = _trunk(x_prompt, mixer_p, xattn_p, pw, BF16)

    sample_bb = 8
    ssm_in = state_ssm.reshape(DEPTH, bs, SSD_WIDTH, SSD_STATE)

    def mixer_s(layer, proj, dt_raw, prev_states):
        return _mixer_sample(layer, proj, dt_raw, state_conv, ssm_in, state_ret, pw, prev_states, bs, ls,
                             sample_bb)

    def xattn_s(layer, x):
        return _xattn_cache(x, layer, pw, cache_mem_k, cache_mem_v, sample_bb), None

    y_sample, ssm_s, conv_s, ret_s = _trunk(x_sample, mixer_s, xattn_s, pw, F32)
    return (y_prompt, y_sample, ssm_p, conv_p, ret_p, mem_k_p, mem_v_p, ssm_s, conv_s, ret_s)
```
